```python
import jax
import jax.numpy as jnp
from jax import lax
import numpy as np

D_MODEL = 1024
BATCH = 4
SEQ = 4096
DEPTH = 4

GRID_W = 64
CTX_LEN = 256
N_MIXERS = 2
N_A_LAYERS = (DEPTH + 1) // 2
N_B_LAYERS = DEPTH // 2
N_DENSE_LAYERS = (DEPTH + 1) // 2
N_MOE_LAYERS = DEPTH // 2
EPS = 1e-6
NEG_BIG = -1e30

HG_HEADS = 8
HG_DK = 128
HG_DV = D_MODEL // HG_HEADS
HG_FDIM = HG_HEADS * HG_DK
HG_IN_DIM = 3 * HG_FDIM + 2 * D_MODEL
HG_CHUNK = 32

SW_HEADS = 16
SW_KV_HEADS = 4
SW_GROUP = SW_HEADS // SW_KV_HEADS
SW_HEAD_DIM = 64
SW_Q_DIM = SW_HEADS * SW_HEAD_DIM
SW_KV_DIM = SW_KV_HEADS * SW_HEAD_DIM
SW_WINDOW = 128
SW_BLOCK = 128
ROPE_THETA = 10000.0

D_FF = 2816
N_EXPERTS = 8
TOP_K = 2
D_EXPERT = 3584

kernel_name = 'hybrid_hgrn2_swa_moe_diffusion_trunk'


def rms_norm(x, w=None, eps=EPS):
    xf = x.astype(jnp.float32)
    y = xf * lax.rsqrt(jnp.mean(xf * xf, axis=-1, keepdims=True) + eps)
    if w is not None:
        y = y * w.astype(jnp.float32)
    return y.astype(x.dtype)


def modulate(x, shift, scale):
    return rms_norm(x) * (1.0 + scale) + shift


def axial_rope_tables(n_tokens, head_dim):
    rows = n_tokens // GRID_W
    row = jnp.repeat(jnp.arange(rows, dtype=jnp.float32), GRID_W)
    col = jnp.tile(jnp.arange(GRID_W, dtype=jnp.float32), rows)
    n_freq = head_dim // 4
    inv_freq = ROPE_THETA ** (-jnp.arange(n_freq, dtype=jnp.float32) / n_freq)
    ang = jnp.stack([row, col])[:, :, None] * inv_freq
    return jnp.cos(ang), jnp.sin(ang)


def rotate_half(x, cos, sin):
    x1, x2 = jnp.split(x, 2, axis=-1)
    return jnp.concatenate([x1 * cos - x2 * sin, x2 * cos + x1 * sin], axis=-1)


def apply_axial_rope(x, cos, sin):
    xf = x.astype(jnp.float32)
    half = x.shape[-1] // 2
    y = jnp.concatenate([rotate_half(xf[..., :half], cos[0], sin[0]),
                         rotate_half(xf[..., half:], cos[1], sin[1])], axis=-1)
    return y.astype(x.dtype)


def gla_chunk_scan(q, k, v, log_f, s0):
    b, h, l, dk = q.shape
    dv = v.shape[-1]
    c = HG_CHUNK
    n = l // c
    chunks = lambda t: jnp.moveaxis(t.reshape(b, h, n, c, t.shape[-1]), 2, 0)
    causal = jnp.tril(jnp.ones((c, c), dtype=bool))[:, :, None]

    def step(state, xs):
        q_c, k_c, v_c, lf_c = xs
        a = jnp.cumsum(lf_c, axis=2)
        a_end = a[:, :, -1:, :]
        rel = a[:, :, :, None, :] - a[:, :, None, :, :]
        decay = jnp.exp(jnp.where(causal, rel, NEG_BIG))
        scores = jnp.einsum('bhtk,bhsk,bhtsk->bhts', q_c, k_c, decay)
        o_c = (jnp.einsum('bhts,bhsv->bhtv', scores, v_c)
               + jnp.einsum('bhtk,bhkv->bhtv', q_c * jnp.exp(a), state))
        state = (jnp.exp(a_end[:, :, 0])[..., None] * state
                 + jnp.einsum('bhsk,bhsv->bhkv', k_c * jnp.exp(a_end - a), v_c))
        return state, o_c

    s_final, o = lax.scan(step, s0, (chunks(q), chunks(k), chunks(v), chunks(log_f)))
    return jnp.moveaxis(o, 0, 2).reshape(b, h, l, dv), s_final


def hgrn2_mixer(h_lat, h_ctx, w_in, lower_bound, norm_w, w_out, with_ctx_out):
    dt = h_lat.dtype
    splits = [HG_FDIM, HG_FDIM + D_MODEL, 2 * HG_FDIM + D_MODEL, 3 * HG_FDIM + D_MODEL]

    def project(h):
        b, n, _ = h.shape
        q, v, zf, zb, g = jnp.split((h @ w_in).astype(jnp.float32), splits, axis=-1)
        heads = lambda t, d: t.reshape(b, n, HG_HEADS, d).transpose(0, 2, 1, 3)

        def gate(z, lb):
            f = lb + (1.0 - lb) * jax.nn.sigmoid(z)
            log_f = jnp.log(jnp.maximum(f, 1e-30))
            key = (1.0 - lb) * jax.nn.sigmoid(-z)
            return heads(key, HG_DK), heads(log_f, HG_DK)

        kf, lf = gate(zf, lower_bound[0])
        kb, lb_ = gate(zb, lower_bound[1])
        return heads(jax.nn.silu(q), HG_DK), heads(v, HG_DV), kf, lf, kb, lb_, g

    qc, vc, kfc, lfc, kbc, lbc, gc = project(h_ctx)
    ql, vl, kfl, lfl, kbl, lbl, gl = project(h_lat)
    b = h_lat.shape[0]
    s0 = jnp.zeros((b, HG_HEADS, HG_DK, HG_DV), jnp.float32)
    rev = lambda t: jnp.flip(t, axis=2)
    o_cf, s_cf = gla_chunk_scan(qc, kfc, vc, lfc, s0)
    o_cb, s_cb = gla_chunk_scan(rev(qc), rev(kbc), rev(vc), rev(lbc), s0)
    o_lf, _ = gla_chunk_scan(ql, kfl, vl, lfl, s_cf)
    o_lb, _ = gla_chunk_scan(rev(ql), rev(kbl), rev(vl), rev(lbl), s_cb)

    def readout(o, g):
        bb, _, n, _ = o.shape
        o = rms_norm(o, norm_w).transpose(0, 2, 1, 3).reshape(bb, n, D_MODEL)
        return (o * jax.nn.silu(g)).astype(dt) @ w_out

    y_lat = readout(o_lf + rev(o_lb), gl)
    y_ctx = readout(o_cf + rev(o_cb), gc) if with_ctx_out else None
    return y_lat, y_ctx


def swa_mixer(h_lat, h_ctx, w_qkv, q_norm_w, k_norm_w, sink, w_out, with_ctx_out):
    b, l, _ = h_lat.shape
    dt = h_lat.dtype
    scale = SW_HEAD_DIM ** -0.5

    def project(h):
        n = h.shape[1]
        q, k, v = jnp.split(h @ w_qkv, [SW_Q_DIM, SW_Q_DIM + SW_KV_DIM], axis=-1)
        q = rms_norm(q.reshape(b, n, SW_KV_HEADS, SW_GROUP, SW_HEAD_DIM), q_norm_w).transpose(0, 2, 3, 1, 4)
        k = rms_norm(k.reshape(b, n, SW_KV_HEADS, SW_HEAD_DIM), k_norm_w).transpose(0, 2, 1, 3)
        v = v.reshape(b, n, SW_KV_HEADS, SW_HEAD_DIM).transpose(0, 2, 1, 3)
        return q, k, v

    q, k, v = project(h_lat)
    qc, kc, vc = project(h_ctx)
    n_ctx = kc.shape[2]
    cos, sin = axial_rope_tables(l, SW_HEAD_DIM)
    q = apply_axial_rope(q, cos, sin)
    k = apply_axial_rope(k, cos, sin)
    sink_logit = sink.astype(jnp.float32).reshape(SW_KV_HEADS, SW_GROUP, 1, 1)
    pad = ((0, 0), (0, 0), (SW_BLOCK, SW_BLOCK), (0, 0))
    k_pad = jnp.pad(k, pad)
    v_pad = jnp.pad(v, pad)
    band = 3 * SW_BLOCK

    def attend_block(j):
        start = j * SW_BLOCK
        qb = lax.dynamic_slice_in_dim(q, start, SW_BLOCK, axis=3)
        kb = lax.dynamic_slice_in_dim(k_pad, start, band, axis=2)
        vb = lax.dynamic_slice_in_dim(v_pad, start, band, axis=2)
        q_pos = start + jnp.arange(SW_BLOCK)
        k_pos = start - SW_BLOCK + jnp.arange(band)
        valid = ((jnp.abs(q_pos[:, None] - k_pos[None, :]) <= SW_WINDOW)
                 & (k_pos >= 0)[None, :] & (k_pos < l)[None, :])
        s_loc = jnp.einsum('bkgqd,bksd->bkgqs', qb, kb).astype(jnp.float32) * scale
        s_loc = jnp.where(valid, s_loc, NEG_BIG)
        s_ctx = jnp.einsum('bkgqd,bksd->bkgqs', qb, kc).astype(jnp.float32) * scale
        s_sink = jnp.broadcast_to(sink_logit, s_ctx.shape[:-1] + (1,))
        p = jax.nn.softmax(jnp.concatenate([s_loc, s_ctx, s_sink], axis=-1), axis=-1).astype(dt)
        return (jnp.einsum('bkgqs,bksd->bkgqd', p[..., :band], vb)
                + jnp.einsum('bkgqs,bksd->bkgqd', p[..., band:band + n_ctx], vc))

    o = lax.map(attend_block, jnp.arange(l // SW_BLOCK))
    o = o.transpose(1, 0, 4, 2, 3, 5).reshape(b, l, SW_Q_DIM)
    y_lat = o @ w_out
    y_ctx = None
    if with_ctx_out:
        s = jnp.einsum('bkgqd,bksd->bkgqs', qc, kc).astype(jnp.float32) * scale
        s_sink = jnp.broadcast_to(sink_logit, s.shape[:-1] + (1,))
        p = jax.nn.softmax(jnp.concatenate([s, s_sink], axis=-1), axis=-1).astype(dt)
        oc = jnp.einsum('bkgqs,bksd->bkgqd', p[..., :n_ctx], vc)
        y_ctx = oc.transpose(0, 3, 1, 2, 4).reshape(b, n_ctx, SW_Q_DIM) @ w_out
    return y_lat, y_ctx


def swiglu(h, w_gate_up, w_down):
    g, u = jnp.split(h @ w_gate_up, 2, axis=-1)
    return (jax.nn.silu(g) * u) @ w_down


def moe_swiglu(h, router, w_gate_up, w_down):
    logits = (h @ router).astype(jnp.float32)
    top_logit, top_idx = lax.top_k(logits, TOP_K)
    top_w = jax.nn.softmax(top_logit, axis=-1)
    gates = jnp.sum(jax.nn.one_hot(top_idx, N_EXPERTS, dtype=jnp.float32) * top_w[..., None], axis=-2).astype(h.dtype)
    out = jnp.zeros_like(h)
    for e in range(N_EXPERTS):
        out = out + gates[..., e:e + 1] * swiglu(h, w_gate_up[e], w_down[e])
    return out


def setup_inputs(seed: int = 0) -> dict:
    key = jax.random.key(seed)
    ks = jax.random.split(key, 20)
    d = D_MODEL
    nrm = lambda k, shape, s: jax.random.normal(k, shape, jnp.float32) * s
    return {
        'x': nrm(ks[0], (BATCH, SEQ, d), 1.0),
        'c': nrm(ks[1], (BATCH, d), 1.0),
        'ctx': nrm(ks[2], (BATCH, CTX_LEN, d), 1.0),
        'c_ctx': nrm(ks[3], (d,), 1.0),
        'w_mod': nrm(ks[4], (DEPTH, d, 6 * d), 0.5 * d ** -0.5),
        'b_mod': nrm(ks[5], (DEPTH, 6 * d), 0.01),
        'hg_w_in': nrm(ks[6], (N_A_LAYERS, d, HG_IN_DIM), d ** -0.5),
        'hg_lb_logits': nrm(ks[7], (N_A_LAYERS, 2, HG_FDIM), 1.0),
        'hg_norm_w': 1.0 + nrm(ks[8], (N_A_LAYERS, HG_DV), 0.1),
        'hg_w_out': nrm(ks[9], (N_A_LAYERS, d, d), d ** -0.5),
        'sw_w_qkv': nrm(ks[10], (N_B_LAYERS, d, SW_Q_DIM + 2 * SW_KV_DIM), d ** -0.5),
        'sw_q_norm': 1.0 + nrm(ks[11], (N_B_LAYERS, SW_HEAD_DIM), 0.1),
        'sw_k_norm': 1.0 + nrm(ks[12], (N_B_LAYERS, SW_HEAD_DIM), 0.1),
        'sw_sink': nrm(ks[13], (N_B_LAYERS, SW_HEADS), 1.0),
        'sw_w_out': nrm(ks[14], (N_B_LAYERS, SW_Q_DIM, d), SW_Q_DIM ** -0.5),
        'ff_w_gate_up': nrm(ks[15], (N_DENSE_LAYERS, d, 2 * D_FF), d ** -0.5),
        'ff_w_down': nrm(ks[16], (N_DENSE_LAYERS, D_FF, d), D_FF ** -0.5),
        'moe_router': nrm(ks[17], (N_MOE_LAYERS, d, N_EXPERTS), d ** -0.5),
        'moe_w_gate_up': nrm(ks[18], (N_MOE_LAYERS, N_EXPERTS, d, 2 * D_EXPERT), d ** -0.5),
        'moe_w_down': nrm(ks[19], (N_MOE_LAYERS, N_EXPERTS, D_EXPERT, d), D_EXPERT ** -0.5),
    }


def reference(x, c, ctx, c_ctx, w_mod, b_mod, hg_w_in, hg_lb_logits, hg_norm_w, hg_w_out,
              sw_w_qkv, sw_q_norm, sw_k_norm, sw_sink, sw_w_out,
              ff_w_gate_up, ff_w_down, moe_router, moe_w_gate_up, moe_w_down):
    p_lb = jax.nn.softmax(hg_lb_logits.astype(jnp.float32), axis=0)
    lower_bounds = jnp.cumsum(p_lb, axis=0) - p_lb[:1]
    silu_c = jax.nn.silu(c)[:, None, :]
    silu_cc = jax.nn.silu(c_ctx)
    n_ctx = ctx.shape[1]
    for i in range(DEPTH):
        ctx_live = i < DEPTH - 1
        sh_a, sc_a, g_a, sh_f, sc_f, g_f = jnp.split(silu_c @ w_mod[i] + b_mod[i], 6, axis=-1)
        csh_a, csc_a, cg_a, csh_f, csc_f, cg_f = jnp.split(silu_cc @ w_mod[i] + b_mod[i], 6, axis=-1)
        h_lat = modulate(x, sh_a, sc_a)
        h_ctx = modulate(ctx, csh_a, csc_a)
        j = i // N_MIXERS
        if i % N_MIXERS == 0:
            y_lat, y_ctx = hgrn2_mixer(h_lat, h_ctx, hg_w_in[j], lower_bounds[j], hg_norm_w[j], hg_w_out[j], ctx_live)
        else:
            y_lat, y_ctx = swa_mixer(h_lat, h_ctx, sw_w_qkv[j], sw_q_norm[j], sw_k_norm[j], sw_sink[j], sw_w_out[j], ctx_live)
        x = x + g_a * y_lat
        h_lat = modulate(x, sh_f, sc_f)
        if ctx_live:
            ctx = ctx + cg_a * y_ctx
            h_tok = jnp.concatenate([modulate(ctx, csh_f, csc_f), h_lat], axis=1)
        else:
            h_tok = h_lat
        m = i // 2
        if i % 2 == 0:
            y = swiglu(h_tok, ff_w_gate_up[m], ff_w_down[m])
        else:
            y = moe_swiglu(h_tok, moe_router[m], moe_w_gate_up[m], moe_w_down[m])
        if ctx_live:
            ctx = ctx + cg_f * y[:, :n_ctx]
            x = x + g_f * y[:, n_ctx:]
        else:
            x = x + g_f * y
    return x
```

```python
import functools

import numpy as np
import jax
import jax.numpy as jnp
from jax import lax
from jax.experimental import pallas as pl
from jax.experimental.pallas import tpu as pltpu

D = 1024
EPS = 1e-6
NEG_BIG = -1e30
GRID_W = 64
ROPE_THETA = 10000.0

HG_HEADS = 8
HG_DK = 128
HG_FDIM = HG_HEADS * HG_DK
HG_CHUNK = 128
HG_LEVELS = 7
HG_ROWS = 256

SW_HEADS = 16
SW_KV = 4
SW_GROUP = 4
SW_DH = 64
SW_BLOCK = 128

N_EXPERTS = 8
LANES = 128

VMEM_LIMIT = 56 * 1024 * 1024


def _cparams(sem):
    return pltpu.CompilerParams(dimension_semantics=sem, vmem_limit_bytes=VMEM_LIMIT)


def _sigmoid(x):
    return 1.0 / (1.0 + jnp.exp(-x))


def _mod_row(i, nct, lt, nb):
    return jnp.where(i < nct, nb, (i - nct) // lt)


def _mod_spec(k, nct, lt, nb, t0, ngrid):
    if ngrid == 2:
        return pl.BlockSpec((None, 1, D), lambda i, j: (_mod_row(i + t0, nct, lt, nb) * 6 + k, 0, 0))
    return pl.BlockSpec((None, 1, D), lambda i, e, f: (_mod_row(i + t0, nct, lt, nb) * 6 + k, 0, 0))


def _modulate(x, shift, scale):
    ms = jnp.mean(x * x, axis=-1, keepdims=True)
    return (x * lax.rsqrt(ms + EPS)) * (1.0 + scale) + shift


def _modvec_kernel(c_ref, w_ref, b_ref, o_ref):
    c = c_ref[...]
    s = c * _sigmoid(c)
    o_ref[...] = jnp.dot(s, w_ref[...], precision=lax.Precision.HIGHEST,
                         preferred_element_type=jnp.float32) + b_ref[...]


def _modvecs(cpad, w_mod, b_mod):
    depth = w_mod.shape[0]
    tn = 1024
    out = pl.pallas_call(
        _modvec_kernel,
        out_shape=jax.ShapeDtypeStruct((depth, 8, 6 * D), jnp.float32),
        grid=(depth, 6 * D // tn),
        in_specs=[pl.BlockSpec((8, D), lambda l, j: (0, 0)),
                  pl.BlockSpec((None, D, tn), lambda l, j: (l, 0, j)),
                  pl.BlockSpec((None, 1, tn), lambda l, j: (l, 0, j))],
        out_specs=pl.BlockSpec((None, 8, tn), lambda l, j: (l, 0, j)),
        compiler_params=_cparams(("arbitrary", "arbitrary")),
    )(cpad, w_mod, b_mod.reshape(depth, 1, 6 * D))
    return out.reshape(depth, 8 * 6, 1, D)


def _modmm_kernel(x_ref, sh_ref, sc_ref, w_ref, o_ref, h_ref):
    @pl.when(pl.program_id(1) == 0)
    def _():
        h_ref[...] = _modulate(x_ref[...], sh_ref[...], sc_ref[...]).astype(jnp.bfloat16)

    o_ref[...] = jnp.dot(h_ref[...], w_ref[...], preferred_element_type=jnp.float32).astype(o_ref.dtype)


def _mod_matmul(x, mods, w, *, k_shift, geom, tm, tn, out_dtype):
    nb, nct_rows, seq = geom
    t, n = x.shape[0], w.shape[1]
    nct, lt = nct_rows // tm, seq // tm
    return pl.pallas_call(
        _modmm_kernel,
        out_shape=jax.ShapeDtypeStruct((t, n), out_dtype),
        grid=(t // tm, n // tn),
        in_specs=[pl.BlockSpec((tm, D), lambda i, j: (i, 0)),
                  _mod_spec(k_shift, nct, lt, nb, 0, 2),
                  _mod_spec(k_shift + 1, nct, lt, nb, 0, 2),
                  pl.BlockSpec((D, tn), lambda i, j: (0, j))],
        out_specs=pl.BlockSpec((tm, tn), lambda i, j: (i, j)),
        scratch_shapes=[pltpu.VMEM((tm, D), jnp.bfloat16)],
        compiler_params=_cparams(("arbitrary", "arbitrary")),
    )(x, mods, mods, w)


def _hg_tables(fwd):
    c = HG_CHUNK
    g = np.zeros((HG_LEVELS + 2, c, c), np.float32)
    u = np.arange(c)
    for l in range(HG_LEVELS):
        w = 1 << l
        for t in range(c):
            mid = (t // (2 * w)) * 2 * w + w
            upper = t >= mid
            if fwd:
                sel = (u >= mid) & (u <= t) if upper else (u > t) & (u < mid)
            else:
                sel = (u >= mid) & (u < t) if upper else (u >= t) & (u < mid)
            g[l, t, sel] = 1.0
    for t in range(c):
        if fwd:
            g[HG_LEVELS, t, u <= t] = 1.0
            g[HG_LEVELS + 1, t, u > t] = 1.0
        else:
            g[HG_LEVELS, t, u >= t] = 1.0
            g[HG_LEVELS + 1, t, u < t] = 1.0
    g = g.reshape((HG_LEVELS + 2) * c, c)
    return np.concatenate([g, g], axis=1)


def _hg_chunk(q, v, z, lb, g_ref, e_ref, st_ref, fwd):
    c = HG_CHUNK
    sig = _sigmoid(z)
    f = lb + (1.0 - lb) * sig
    lf = jnp.log(jnp.maximum(f, 1e-30))
    k = (1.0 - lb) * (1.0 - sig)
    lf_hi = lf.astype(jnp.bfloat16)
    lf_lo = (lf - lf_hi.astype(jnp.float32)).astype(jnp.bfloat16)
    e_ref[...] = jnp.exp(jnp.dot(g_ref[...], jnp.concatenate([lf_hi, lf_lo], axis=0),
                                 preferred_element_type=jnp.float32))

    row = lax.broadcasted_iota(jnp.int32, (c, c), 0)
    col = lax.broadcasted_iota(jnp.int32, (c, c), 1)
    nt = (((1,), (1,)), ((), ()))
    qb = q.astype(jnp.bfloat16)
    kb = k.astype(jnp.bfloat16)
    scores = jnp.where(row == col, lax.dot_general(qb, kb, nt, preferred_element_type=jnp.float32), 0.0)
    for l in range(HG_LEVELS):
        e = e_ref[l * c:(l + 1) * c, :]
        upper = ((lax.broadcasted_iota(jnp.int32, (c, HG_DK), 0) >> l) & 1) == 1
        q_side = upper if fwd else jnp.logical_not(upper)
        qm = jnp.where(q_side, q * e, 0.0).astype(jnp.bfloat16)
        km = jnp.where(q_side, 0.0, k * e).astype(jnp.bfloat16)
        sc = lax.dot_general(qm, km, nt, preferred_element_type=jnp.float32)
        scores = scores + jnp.where((row >> (l + 1)) == (col >> (l + 1)), sc, 0.0)

    e_in = e_ref[HG_LEVELS * c:(HG_LEVELS + 1) * c, :]
    e_out = e_ref[(HG_LEVELS + 1) * c:(HG_LEVELS + 2) * c, :]
    tot = e_in[c - 1:c, :] if fwd else e_in[0:1, :]
    vb = v.astype(jnp.bfloat16)
    st = st_ref[...]
    o = (jnp.dot(scores.astype(jnp.bfloat16), vb, preferred_element_type=jnp.float32)
         + lax.dot_general((q * e_in).astype(jnp.bfloat16), st.astype(jnp.bfloat16), nt,
                           preferred_element_type=jnp.float32))
    st_ref[...] = tot * st + jnp.dot(v.T.astype(jnp.bfloat16), (k * e_out).astype(jnp.bfloat16),
                                     preferred_element_type=jnp.float32)
    return o


def _hg_scan_kernel(qf_ref, vf_ref, zf_ref, qb_ref, vb_ref, zb_ref, lbf_ref, lbb_ref, gf_ref, gb_ref,
                    of_ref, ob_ref, sf_ref, sb_ref, e_ref):
    @pl.when(pl.program_id(2) == 0)
    def _():
        sf_ref[...] = jnp.zeros_like(sf_ref)
        sb_ref[...] = jnp.zeros_like(sb_ref)

    c = HG_CHUNK
    nch = HG_ROWS // c
    for ci in range(nch):
        r = slice(ci * c, (ci + 1) * c)
        q = qf_ref[r, :]
        o = _hg_chunk(q * _sigmoid(q), vf_ref[r, :], zf_ref[r, :], lbf_ref[...], gf_ref, e_ref, sf_ref, True)
        of_ref[r, :] = o.astype(of_ref.dtype)
    for ci in reversed(range(nch)):
        r = slice(ci * c, (ci + 1) * c)
        q = qb_ref[r, :]
        o = _hg_chunk(q * _sigmoid(q), vb_ref[r, :], zb_ref[r, :], lbb_ref[...], gb_ref, e_ref, sb_ref, False)
        ob_ref[r, :] = o.astype(ob_ref.dtype)


def _hg_scan(p, lbf, lbb, geom):
    nb, nct_rows, seq = geom
    t = p.shape[0]
    r = HG_ROWS
    cb, lb_ = (nct_rows // nb) // r, seq // r
    nsteps = cb + lb_
    lat0 = nct_rows // r

    def fblk(b, s):
        return jnp.where(s < cb, b * cb + s, lat0 + b * lb_ + (s - cb))

    def bblk(b, s):
        return jnp.where(s < cb, b * cb + (cb - 1 - s), lat0 + b * lb_ + (lb_ - 1 - (s - cb)))

    def spec(blk, colblock):
        return pl.BlockSpec((r, HG_DK), lambda b, h, s: (blk(b, s), colblock * HG_HEADS + h))

    lbspec = pl.BlockSpec((1, HG_DK), lambda b, h, s: (0, h))
    gspec = pl.BlockSpec(((HG_LEVELS + 2) * HG_CHUNK, 2 * HG_CHUNK), lambda b, h, s: (0, 0))
    gf = jnp.asarray(_hg_tables(True), jnp.bfloat16)
    gb = jnp.asarray(_hg_tables(False), jnp.bfloat16)
    return pl.pallas_call(
        _hg_scan_kernel,
        out_shape=(jax.ShapeDtypeStruct((t, D), jnp.bfloat16), jax.ShapeDtypeStruct((t, D), jnp.bfloat16)),
        grid=(nb, HG_HEADS, nsteps),
        in_specs=[spec(fblk, 0), spec(fblk, 1), spec(fblk, 2),
                  spec(bblk, 0), spec(bblk, 1), spec(bblk, 3),
                  lbspec, lbspec, gspec, gspec],
        out_specs=(pl.BlockSpec((r, HG_DK), lambda b, h, s: (fblk(b, s), h)),
                   pl.BlockSpec((r, HG_DK), lambda b, h, s: (bblk(b, s), h))),
        scratch_shapes=[pltpu.VMEM((HG_DK, HG_DK), jnp.float32),
                        pltpu.VMEM((HG_DK, HG_DK), jnp.float32),
                        pltpu.VMEM(((HG_LEVELS + 2) * HG_CHUNK, HG_DK), jnp.float32)],
        compiler_params=_cparams(("arbitrary", "arbitrary", "arbitrary")),
    )(p, p, p, p, p, p, lbf, lbb, gf, gb)


def _hg_out_kernel(of_ref, ob_ref, g_ref, nw_ref, w_ref, x_ref, gate_ref, o_ref, h_ref):
    @pl.when(pl.program_id(1) == 0)
    def _():
        nw = nw_ref[...]
        for h in range(HG_HEADS):
            cs = slice(h * HG_DK, (h + 1) * HG_DK)
            o = of_ref[:, cs].astype(jnp.float32) + ob_ref[:, cs].astype(jnp.float32)
            y = o * lax.rsqrt(jnp.mean(o * o, axis=-1, keepdims=True) + EPS) * nw
            g = g_ref[:, cs]
            h_ref[:, cs] = (y * (g * _sigmoid(g))).astype(jnp.bfloat16)

    acc = jnp.dot(h_ref[...], w_ref[...], preferred_element_type=jnp.float32)
    o_ref[...] = x_ref[...] + gate_ref[...] * acc


def _hg_out(of, ob, p, norm_w, w, x, mods, *, geom, tm, tn):
    nb, nct_rows, seq = geom
    t = x.shape[0]
    nct, lt = nct_rows // tm, seq // tm
    gate = pl.BlockSpec((None, 1, tn), lambda i, j: (_mod_row(i, nct, lt, nb) * 6 + 2, 0, j))
    return pl.pallas_call(
        _hg_out_kernel,
        out_shape=jax.ShapeDtypeStruct((t, D), jnp.float32),
        grid=(t // tm, D // tn),
        in_specs=[pl.BlockSpec((tm, D), lambda i, j: (i, 0)),
                  pl.BlockSpec((tm, D), lambda i, j: (i, 0)),
                  pl.BlockSpec((tm, D), lambda i, j: (i, 4)),
                  pl.BlockSpec((1, HG_DK), lambda i, j: (0, 0)),
                  pl.BlockSpec((D, tn), lambda i, j: (0, j)),
                  pl.BlockSpec((tm, tn), lambda i, j: (i, j)),
                  gate],
        out_specs=pl.BlockSpec((tm, tn), lambda i, j: (i, j)),
        scratch_shapes=[pltpu.VMEM((tm, D), jnp.bfloat16)],
        compiler_params=_cparams(("arbitrary", "arbitrary")),
    )(of, ob, p, norm_w, w, x, mods)


def _resmm_kernel(a_ref, w_ref, x_ref, gate_ref, o_ref):
    acc = jnp.dot(a_ref[...], w_ref[...], preferred_element_type=jnp.float32)
    o_ref[...] = x_ref[...] + gate_ref[...] * acc


def _res_matmul(a, w, x, mods, *, geom, tm, tn, t0):
    nb, nct_rows, seq = geom
    t = x.shape[0]
    nct, lt = nct_rows // tm, seq // tm
    gate = pl.BlockSpec((None, 1, tn), lambda i, j: (_mod_row(i + t0, nct, lt, nb) * 6 + 2, 0, j))
    return pl.pallas_call(
        _resmm_kernel,
        out_shape=jax.ShapeDtypeStruct((t, D), jnp.float32),
        grid=(t // tm - t0, D // tn),
        in_specs=[pl.BlockSpec((tm, a.shape[1]), lambda i, j: (i + t0, 0)),
                  pl.BlockSpec((a.shape[1], tn), lambda i, j: (0, j)),
                  pl.BlockSpec((tm, tn), lambda i, j: (i + t0, j)),
                  gate],
        out_specs=pl.BlockSpec((tm, tn), lambda i, j: (i + t0, j)),
        compiler_params=_cparams(("arbitrary", "arbitrary")),
    )(a, w, x, mods)


def _rope_tables(seq, tm):
    rows = seq // GRID_W
    row = np.repeat(np.arange(rows, dtype=np.float32), GRID_W)
    col = np.tile(np.arange(GRID_W, dtype=np.float32), rows)
    nf = SW_DH // 4
    inv = (ROPE_THETA ** (-np.arange(nf, dtype=np.float32) / nf)).astype(np.float32)
    ang_r = row[:, None] * inv
    ang_c = col[:, None] * inv
    cos = np.concatenate([np.cos(ang_r), np.cos(ang_r), np.cos(ang_c), np.cos(ang_c)], axis=1)
    sin = np.concatenate([-np.sin(ang_r), np.sin(ang_r), -np.sin(ang_c), np.sin(ang_c)], axis=1)
    cos = np.concatenate([np.tile(cos, (1, 2)), np.ones((tm, LANES), np.float32)], axis=0)
    sin = np.concatenate([np.tile(sin, (1, 2)), np.zeros((tm, LANES), np.float32)], axis=0)
    return jnp.asarray(cos, jnp.float32), jnp.asarray(sin, jnp.float32)


def _qkv_kernel(x_ref, sh_ref, sc_ref, w_ref, nw_ref, ones_ref, cos_ref, sin_ref, o_ref, h_ref):
    j = pl.program_id(1)

    @pl.when(j == 0)
    def _():
        h_ref[...] = _modulate(x_ref[...], sh_ref[...], sc_ref[...]).astype(jnp.bfloat16)

    acc = jnp.dot(h_ref[...], w_ref[...], preferred_element_type=jnp.float32)

    @pl.when(j == 2)
    def _():
        o_ref[...] = acc.astype(o_ref.dtype)

    @pl.when(j < 2)
    def _():
        ss = jnp.dot((acc * acc).astype(jnp.bfloat16), ones_ref[...], preferred_element_type=jnp.float32)
        y = acc * lax.rsqrt(ss * (1.0 / SW_DH) + EPS) * nw_ref[...]
        cos = cos_ref[...]
        sin = sin_ref[...]
        lane = lax.broadcasted_iota(jnp.int32, (y.shape[0], LANES), 1)
        first = (lane & 16) == 0
        scale = jnp.where(j == 0, SW_DH ** -0.5, 1.0)
        for g in range(D // LANES):
            cs = slice(g * LANES, (g + 1) * LANES)
            yg = y[:, cs]
            partner = jnp.where(first, pltpu.roll(yg, LANES - 16, 1), pltpu.roll(yg, 16, 1))
            o_ref[:, cs] = ((yg * cos + partner * sin) * scale).astype(o_ref.dtype)


def _qkv_proj(x, mods, w3, nw2, *, geom, tm):
    nb, nct_rows, seq = geom
    t = x.shape[0]
    nct, lt = nct_rows // tm, seq // tm
    cos, sin = _rope_tables(seq, tm)
    blk = np.kron(np.eye(D // SW_DH, dtype=np.float32), np.ones((SW_DH, SW_DH), np.float32))
    ones = jnp.asarray(blk, jnp.bfloat16)
    tab = lambda i, j: (jnp.where(i < nct, lt, (i - nct) % lt), 0)
    return pl.pallas_call(
        _qkv_kernel,
        out_shape=jax.ShapeDtypeStruct((t, 3 * D), jnp.bfloat16),
        grid=(t // tm, 3),
        in_specs=[pl.BlockSpec((tm, D), lambda i, j: (i, 0)),
                  _mod_spec(0, nct, lt, nb, 0, 2),
                  _mod_spec(1, nct, lt, nb, 0, 2),
                  pl.BlockSpec((D, D), lambda i, j: (0, j)),
                  pl.BlockSpec((None, 1, D), lambda i, j: (jnp.minimum(j, 1), 0, 0)),
                  pl.BlockSpec((D, D), lambda i, j: (0, 0)),
                  pl.BlockSpec((tm, LANES), tab),
                  pl.BlockSpec((tm, LANES), tab)],
        out_specs=pl.BlockSpec((tm, D), lambda i, j: (i, j)),
        scratch_shapes=[pltpu.VMEM((tm, D), jnp.bfloat16)],
        compiler_params=_cparams(("arbitrary", "arbitrary")),
    )(x, mods, mods, w3, nw2, ones, cos, sin)


def _attn_body(q, kcat, vcat, valid, sink_ref, kv, o_ref):
    nt = (((1,), (1,)), ((), ()))
    grp = lax.broadcasted_iota(jnp.int32, q.shape, 1) // SW_DH
    out = jnp.zeros(q.shape, jnp.float32)
    for g in range(SW_GROUP):
        sink = sink_ref[kv * SW_GROUP + g]
        s = lax.dot_general(jnp.where(grp == g, q, jnp.zeros_like(q)), kcat, nt,
                            preferred_element_type=jnp.float32)
        if valid is not None:
            s = jnp.where(valid, s, NEG_BIG)
        m = jnp.maximum(jnp.max(s, axis=-1, keepdims=True), sink)
        p = jnp.exp(s - m)
        denom = jnp.sum(p, axis=-1, keepdims=True) + jnp.exp(sink - m)
        og = jnp.dot(p.astype(jnp.bfloat16), vcat, preferred_element_type=jnp.float32) / denom
        out = jnp.where(grp == g, og, out)
    o_ref[...] = out.astype(o_ref.dtype)


def _attn_lat_kernel(sink_ref, q_ref, kp_ref, kc_ref, kn_ref, kx_ref, vp_ref, vc_ref, vn_ref, vx_ref, o_ref,
                     *, nblk):
    kv, j = pl.program_id(1), pl.program_id(2)
    kcat = jnp.concatenate([kp_ref[...], kc_ref[...], kn_ref[...], kx_ref[...]], axis=0)
    vcat = jnp.concatenate([vp_ref[...], vc_ref[...], vn_ref[...], vx_ref[...]], axis=0)
    ns = kcat.shape[0]
    t = lax.broadcasted_iota(jnp.int32, (SW_BLOCK, ns), 0)
    s = lax.broadcasted_iota(jnp.int32, (SW_BLOCK, ns), 1)
    lo = jnp.maximum(t, jnp.where(j > 0, 0, SW_BLOCK))
    hi = jnp.minimum(t + 2 * SW_BLOCK, jnp.where(j < nblk - 1, 3 * SW_BLOCK - 1, 2 * SW_BLOCK - 1))
    valid = ((s >= lo) & (s <= hi)) | (s >= 3 * SW_BLOCK)
    _attn_body(q_ref[...], kcat, vcat, valid, sink_ref, kv, o_ref)


def _attn_ctx_kernel(sink_ref, q_ref, kx_ref, vx_ref, prev_ref, o_ref):
    del prev_ref
    _attn_body(q_ref[...], kx_ref[...], vx_ref[...], None, sink_ref, pl.program_id(1), o_ref)


def _attention(qkv, sink, geom, ctx_out):
    nb, nct_rows, seq = geom
    t = qkv.shape[0]
    ctx_len = nct_rows // nb
    nblk = seq // SW_BLOCK
    lat0 = nct_rows // SW_BLOCK
    w = SW_GROUP * SW_DH
    smem = pl.BlockSpec(memory_space=pltpu.SMEM)

    def lat(off, colblock):
        return pl.BlockSpec((SW_BLOCK, w), lambda b, kv, j: (
            lat0 + b * nblk + jnp.clip(j + off, 0, nblk - 1), colblock * SW_KV + kv))

    def ctxkv(colblock):
        return pl.BlockSpec((ctx_len, w), lambda b, kv, j: (b, colblock * SW_KV + kv))

    out_lat = pl.pallas_call(
        functools.partial(_attn_lat_kernel, nblk=nblk),
        out_shape=jax.ShapeDtypeStruct((t, D), jnp.bfloat16),
        grid=(nb, SW_KV, nblk),
        in_specs=[smem, lat(0, 0), lat(-1, 1), lat(0, 1), lat(1, 1), ctxkv(1),
                  lat(-1, 2), lat(0, 2), lat(1, 2), ctxkv(2)],
        out_specs=pl.BlockSpec((SW_BLOCK, w), lambda b, kv, j: (lat0 + b * nblk + j, kv)),
        compiler_params=_cparams(("arbitrary", "arbitrary", "arbitrary")),
    )(sink, qkv, qkv, qkv, qkv, qkv, qkv, qkv, qkv, qkv)
    if not ctx_out:
        return out_lat
    ncb = ctx_len // SW_BLOCK
    return pl.pallas_call(
        _attn_ctx_kernel,
        out_shape=jax.ShapeDtypeStruct((t, D), jnp.bfloat16),
        grid=(nb, SW_KV, ncb),
        in_specs=[smem,
                  pl.BlockSpec((SW_BLOCK, w), lambda b, kv, j: (b * ncb + j, kv)),
                  ctxkv(1), ctxkv(2),
                  pl.BlockSpec(memory_space=pl.ANY)],
        out_specs=pl.BlockSpec((SW_BLOCK, w), lambda b, kv, j: (b * ncb + j, kv)),
        input_output_aliases={4: 0},
        compiler_params=_cparams(("arbitrary", "arbitrary", "arbitrary")),
    )(sink, qkv, qkv, qkv, out_lat)


def _router_kernel(x_ref, sh_ref, sc_ref, r_ref, g_ref):
    h = _modulate(x_ref[...], sh_ref[...], sc_ref[...])
    logits = jnp.dot(h, r_ref[...], precision=lax.Precision.HIGHEST, preferred_element_type=jnp.float32)
    lane = lax.broadcasted_iota(jnp.int32, logits.shape, 1).astype(jnp.float32)
    logits = jnp.where(lane < N_EXPERTS, logits, -jnp.inf)
    l1 = jnp.max(logits, axis=-1, keepdims=True)
    i1 = jnp.min(jnp.where(logits == l1, lane, float(LANES)), axis=-1, keepdims=True)
    rest = jnp.where(lane == i1, -jnp.inf, logits)
    l2 = jnp.max(rest, axis=-1, keepdims=True)
    i2 = jnp.min(jnp.where(rest == l2, lane, float(LANES)), axis=-1, keepdims=True)
    w1 = 1.0 / (1.0 + jnp.exp(l2 - l1))
    g_ref[...] = jnp.where(lane == i1, w1, 0.0) + jnp.where(lane == i2, 1.0 - w1, 0.0)


def _router(x, mods, router_pad, *, geom, tm, t0):
    nb, nct_rows, seq = geom
    t = x.shape[0]
    nct, lt = nct_rows // tm, seq // tm
    row = lambda k: pl.BlockSpec((None, 1, D), lambda i: (_mod_row(i + t0, nct, lt, nb) * 6 + k, 0, 0))
    return pl.pallas_call(
        _router_kernel,
        out_shape=jax.ShapeDtypeStruct((t, LANES), jnp.float32),
        grid=(t // tm - t0,),
        in_specs=[pl.BlockSpec((tm, D), lambda i: (i + t0, 0)), row(3), row(4),
                  pl.BlockSpec((D, LANES), lambda i: (0, 0))],
        out_specs=pl.BlockSpec((tm, LANES), lambda i: (i + t0, 0)),
        compiler_params=_cparams(("arbitrary",)),
    )(x, mods, mods, router_pad)


def _ffn_kernel(x_ref, sh_ref, sc_ref, gate_ref, rg_ref, wg_ref, wu_ref, wd_ref, o_ref, h_ref, acc_ref,
                *, routed):
    e, f = pl.program_id(1), pl.program_id(2)

    @pl.when((e == 0) & (f == 0))
    def _():
        h_ref[...] = _modulate(x_ref[...], sh_ref[...], sc_ref[...]).astype(jnp.bfloat16)
        acc_ref[...] = jnp.zeros_like(acc_ref)

    h = h_ref[...]
    g = jnp.dot(h, wg_ref[...], preferred_element_type=jnp.float32)
    u = jnp.dot(h, wu_ref[...], preferred_element_type=jnp.float32)
    a = g * _sigmoid(g) * u
    if routed:
        rg = rg_ref[...]
        lane = lax.broadcasted_iota(jnp.int32, rg.shape, 1)
        a = a * jnp.sum(jnp.where(lane == e, rg, 0.0), axis=-1, keepdims=True)
    acc_ref[...] += jnp.dot(a.astype(jnp.bfloat16), wd_ref[...], preferred_element_type=jnp.float32)

    @pl.when((e == pl.num_programs(1) - 1) & (f == pl.num_programs(2) - 1))
    def _():
        o_ref[...] = x_ref[...] + gate_ref[...] * acc_ref[...]


def _ffn(x, mods, rgates, wgu, wd, *, geom, tm, tf, t0):
    nb, nct_rows, seq = geom
    t = x.shape[0]
    ne, fdim = wd.shape[0], wd.shape[1]
    nf = fdim // tf
    nct, lt = nct_rows // tm, seq // tm
    routed = rgates is not None
    if not routed:
        rgates = jnp.zeros((8, LANES), jnp.float32)
        rg_spec = pl.BlockSpec((8, LANES), lambda i, e, f: (0, 0))
    else:
        rg_spec = pl.BlockSpec((tm, LANES), lambda i, e, f: (i + t0, 0))
    return pl.pallas_call(
        functools.partial(_ffn_kernel, routed=routed),
        out_shape=jax.ShapeDtypeStruct((t, D), jnp.float32),
        grid=(t // tm - t0, ne, nf),
        in_specs=[pl.BlockSpec((tm, D), lambda i, e, f: (i + t0, 0)),
                  _mod_spec(3, nct, lt, nb, t0, 3),
                  _mod_spec(4, nct, lt, nb, t0, 3),
                  _mod_spec(5, nct, lt, nb, t0, 3),
                  rg_spec,
                  pl.BlockSpec((None, D, tf), lambda i, e, f: (e, 0, f)),
                  pl.BlockSpec((None, D, tf), lambda i, e, f: (e, 0, nf + f)),
                  pl.BlockSpec((None, tf, D), lambda i, e, f: (e, f, 0))],
        out_specs=pl.BlockSpec((tm, D), lambda i, e, f: (i + t0, 0)),
        scratch_shapes=[pltpu.VMEM((tm, D), jnp.bfloat16), pltpu.VMEM((tm, D), jnp.float32)],
        compiler_params=_cparams(("arbitrary", "arbitrary", "arbitrary")),
    )(x, mods, mods, mods, rgates, wgu, wgu, wd)


def kernel(x, c, ctx, c_ctx, w_mod, b_mod, hg_w_in, hg_lb_logits, hg_norm_w, hg_w_out, sw_w_qkv, sw_q_norm,
           sw_k_norm, sw_sink, sw_w_out, ff_w_gate_up, ff_w_down, moe_router, moe_w_gate_up, moe_w_down):
    nb, seq, _ = x.shape
    ctx_len = ctx.shape[1]
    depth = w_mod.shape[0]
    nct_rows = nb * ctx_len
    geom = (nb, nct_rows, seq)
    tm = 512
    bf = jnp.bfloat16

    xs = jnp.concatenate([ctx.reshape(nct_rows, D), x.reshape(nb * seq, D)], axis=0)
    cpad = jnp.concatenate([c, c_ctx[None, :], jnp.zeros((8 - nb - 1, D), jnp.float32)], axis=0)
    mods_all = _modvecs(cpad, w_mod, b_mod)

    p_lb = jax.nn.softmax(hg_lb_logits.astype(jnp.float32), axis=0)
    lower_bounds = jnp.cumsum(p_lb, axis=0) - p_lb[:1]

    for i in range(depth):
        ctx_live = i < depth - 1
        t0 = 0 if ctx_live else nct_rows // tm
        mods = mods_all[i]
        j = i // 2
        if i % 2 == 0:
            p = _mod_matmul(xs, mods, hg_w_in[j].astype(bf), k_shift=0, geom=geom, tm=tm, tn=512,
                            out_dtype=jnp.float32)
            of, ob = _hg_scan(p, lower_bounds[j, 0:1], lower_bounds[j, 1:2], geom)
            xs = _hg_out(of, ob, p, hg_norm_w[j][None, :], hg_w_out[j].astype(bf), xs, mods,
                         geom=geom, tm=tm, tn=512)
        else:
            wq, wk, wv = jnp.split(sw_w_qkv[j], [D, D + SW_KV * SW_DH], axis=1)
            rep = lambda w: jnp.repeat(w.reshape(D, SW_KV, 1, SW_DH), SW_GROUP, axis=2).reshape(D, D)
            w3 = jnp.concatenate([wq, rep(wk), rep(wv)], axis=1).astype(bf)
            nw2 = jnp.stack([jnp.tile(sw_q_norm[j], SW_HEADS), jnp.tile(sw_k_norm[j], SW_HEADS)])[:, None, :]
            qkv = _qkv_proj(xs, mods, w3, nw2, geom=geom, tm=tm)
            o = _attention(qkv, sw_sink[j], geom, ctx_live)
            xs = _res_matmul(o, sw_w_out[j].astype(bf), xs, mods, geom=geom, tm=tm, tn=512, t0=t0)
        if i % 2 == 0:
            xs = _ffn(xs, mods, None, ff_w_gate_up[j][None].astype(bf), ff_w_down[j][None].astype(bf),
                      geom=geom, tm=tm, tf=256, t0=t0)
        else:
            rpad = jnp.pad(moe_router[j], ((0, 0), (0, LANES - N_EXPERTS)))
            rg = _router(xs, mods, rpad, geom=geom, tm=tm, t0=t0)
            xs = _ffn(xs, mods, rg, moe_w_gate_up[j].astype(bf), moe_w_down[j].astype(bf),
                      geom=geom, tm=tm, tf=512, t0=t0)
    return xs[nct_rows:].reshape(nb, seq, D)
```

```python
import functools

import numpy as np
import jax
import jax.numpy as jnp
from jax import lax
from jax.experimental import pallas as pl
from jax.experimental.pallas import tpu as pltpu

D = 1024
EPS = 1e-6
NEG_BIG = -1e30
GRID_W = 64
ROPE_THETA = 10000.0

HG_HEADS = 8
HG_DK = 128
HG_FDIM = HG_HEADS * HG_DK
HG_CHUNK = 128
HG_LEVELS = 7
HG_ROWS = 256

SW_HEADS = 16
SW_KV = 4
SW_GROUP = 4
SW_DH = 64
SW_BLOCK = 128

N_EXPERTS = 8
LANES = 128

VMEM_LIMIT = 56 * 1024 * 1024


def _cparams(sem):
    return pltpu.CompilerParams(dimension_semantics=sem, vmem_limit_bytes=VMEM_LIMIT)


def _sigmoid(x):
    return 1.0 / (1.0 + jnp.exp(-x))


def _mod_row(i, nct, lt, nb):
    return jnp.where(i < nct, nb, (i - nct) // lt)


def _mod_spec(k, nct, lt, nb):
    return pl.BlockSpec((None, 1, D), lambda i, *_: (_mod_row(i, nct, lt, nb) * 6 + k, 0, 0))


def _modulate(x, shift, scale):
    ms = jnp.mean(x * x, axis=-1, keepdims=True)
    return (x * lax.rsqrt(ms + EPS)) * (1.0 + scale) + shift


def _modvec_kernel(c_ref, w_ref, b_ref, o_ref):
    c = c_ref[...]
    s = c * _sigmoid(c)
    o_ref[...] = jnp.dot(s, w_ref[...], precision=lax.Precision.HIGHEST,
                         preferred_element_type=jnp.float32) + b_ref[...]


def _modvecs(cpad, w_mod, b_mod):
    depth = w_mod.shape[0]
    tn = 1024
    out = pl.pallas_call(
        _modvec_kernel,
        out_shape=jax.ShapeDtypeStruct((depth, 8, 6 * D), jnp.float32),
        grid=(depth, 6 * D // tn),
        in_specs=[pl.BlockSpec((8, D), lambda l, j: (0, 0)),
                  pl.BlockSpec((None, D, tn), lambda l, j: (l, 0, j)),
                  pl.BlockSpec((None, 1, tn), lambda l, j: (l, 0, j))],
        out_specs=pl.BlockSpec((None, 8, tn), lambda l, j: (l, 0, j)),
        name="adaln_vectors",
        compiler_params=_cparams(("arbitrary", "arbitrary")),
    )(cpad, w_mod, b_mod.reshape(depth, 1, 6 * D))
    return out.reshape(depth, 8 * 6, 1, D)


def _modmm_kernel(x_ref, sh_ref, sc_ref, w_ref, o_ref, h_ref):
    @pl.when(pl.program_id(1) == 0)
    def _():
        h_ref[...] = _modulate(x_ref[...], sh_ref[...], sc_ref[...]).astype(jnp.bfloat16)

    o_ref[...] = jnp.dot(h_ref[...], w_ref[...], preferred_element_type=jnp.float32).astype(o_ref.dtype)


def _mod_matmul(x, mods, w, *, k_shift, geom, tm, tn, out_dtype):
    nb, nct_rows, seq = geom
    t, n = x.shape[0], w.shape[1]
    nct, lt = nct_rows // tm, seq // tm
    return pl.pallas_call(
        _modmm_kernel,
        out_shape=jax.ShapeDtypeStruct((t, n), out_dtype),
        grid=(t // tm, n // tn),
        in_specs=[pl.BlockSpec((tm, D), lambda i, j: (i, 0)),
                  _mod_spec(k_shift, nct, lt, nb),
                  _mod_spec(k_shift + 1, nct, lt, nb),
                  pl.BlockSpec((D, tn), lambda i, j: (0, j))],
        out_specs=pl.BlockSpec((tm, tn), lambda i, j: (i, j)),
        scratch_shapes=[pltpu.VMEM((tm, D), jnp.bfloat16)],
        name="hg_in_proj",
        compiler_params=_cparams(("arbitrary", "arbitrary")),
    )(x, mods, mods, w)


def _hg_tables(fwd):
    c = HG_CHUNK
    g = np.zeros((HG_LEVELS + 2, c, c), np.float32)
    u = np.arange(c)
    for l in range(HG_LEVELS):
        w = 1 << l
        for t in range(c):
            mid = (t // (2 * w)) * 2 * w + w
            upper = t >= mid
            if fwd:
                sel = (u >= mid) & (u <= t) if upper else (u > t) & (u < mid)
            else:
                sel = (u >= mid) & (u < t) if upper else (u >= t) & (u < mid)
            g[l, t, sel] = 1.0
    for t in range(c):
        if fwd:
            g[HG_LEVELS, t, u <= t] = 1.0
            g[HG_LEVELS + 1, t, u > t] = 1.0
        else:
            g[HG_LEVELS, t, u >= t] = 1.0
            g[HG_LEVELS + 1, t, u < t] = 1.0
    g = g.reshape((HG_LEVELS + 2) * c, c)
    return np.concatenate([g, g], axis=1)


def _hg_chunk(q, v, z, lb, g_ref, e_ref, st_ref, fwd):
    c = HG_CHUNK
    sig = _sigmoid(z)
    f = lb + (1.0 - lb) * sig
    lf = jnp.log(jnp.maximum(f, 1e-30))
    k = (1.0 - lb) * (1.0 - sig)
    lf_hi = lf.astype(jnp.bfloat16)
    lf_lo = (lf - lf_hi.astype(jnp.float32)).astype(jnp.bfloat16)
    e_ref[...] = jnp.exp(jnp.dot(g_ref[...], jnp.concatenate([lf_hi, lf_lo], axis=0),
                                 preferred_element_type=jnp.float32))

    row = lax.broadcasted_iota(jnp.int32, (c, c), 0)
    col = lax.broadcasted_iota(jnp.int32, (c, c), 1)
    nt = (((1,), (1,)), ((), ()))
    qb = q.astype(jnp.bfloat16)
    kb = k.astype(jnp.bfloat16)
    scores = jnp.where(row == col, lax.dot_general(qb, kb, nt, preferred_element_type=jnp.float32), 0.0)
    for l in range(HG_LEVELS):
        e = e_ref[l * c:(l + 1) * c, :]
        upper = ((lax.broadcasted_iota(jnp.int32, (c, HG_DK), 0) >> l) & 1) == 1
        q_side = upper if fwd else jnp.logical_not(upper)
        qm = jnp.where(q_side, q * e, 0.0).astype(jnp.bfloat16)
        km = jnp.where(q_side, 0.0, k * e).astype(jnp.bfloat16)
        sc = lax.dot_general(qm, km, nt, preferred_element_type=jnp.float32)
        scores = scores + jnp.where((row >> (l + 1)) == (col >> (l + 1)), sc, 0.0)

    e_in = e_ref[HG_LEVELS * c:(HG_LEVELS + 1) * c, :]
    e_out = e_ref[(HG_LEVELS + 1) * c:(HG_LEVELS + 2) * c, :]
    tot = e_in[c - 1:c, :] if fwd else e_in[0:1, :]
    vb = v.astype(jnp.bfloat16)
    st = st_ref[...]
    o = (jnp.dot(scores.astype(jnp.bfloat16), vb, preferred_element_type=jnp.float32)
         + lax.dot_general((q * e_in).astype(jnp.bfloat16), st.astype(jnp.bfloat16), nt,
                           preferred_element_type=jnp.float32))
    st_ref[...] = tot * st + jnp.dot(v.T.astype(jnp.bfloat16), (k * e_out).astype(jnp.bfloat16),
                                     preferred_element_type=jnp.float32)
    return o


def _hg_scan_kernel(qf_ref, vf_ref, zf_ref, qb_ref, vb_ref, zb_ref, lbf_ref, lbb_ref, gf_ref, gb_ref,
                    of_ref, ob_ref, sf_ref, sb_ref, e_ref):
    @pl.when(pl.program_id(2) == 0)
    def _():
        sf_ref[...] = jnp.zeros_like(sf_ref)
        sb_ref[...] = jnp.zeros_like(sb_ref)

    c = HG_CHUNK
    nch = HG_ROWS // c
    for ci in range(nch):
        r = slice(ci * c, (ci + 1) * c)
        q = qf_ref[r, :]
        o = _hg_chunk(q * _sigmoid(q), vf_ref[r, :], zf_ref[r, :], lbf_ref[...], gf_ref, e_ref, sf_ref, True)
        of_ref[r, :] = o.astype(of_ref.dtype)
    for ci in reversed(range(nch)):
        r = slice(ci * c, (ci + 1) * c)
        q = qb_ref[r, :]
        o = _hg_chunk(q * _sigmoid(q), vb_ref[r, :], zb_ref[r, :], lbb_ref[...], gb_ref, e_ref, sb_ref, False)
        ob_ref[r, :] = o.astype(ob_ref.dtype)


def _hg_scan(p, lbf, lbb, geom):
    nb, nct_rows, seq = geom
    t = p.shape[0]
    r = HG_ROWS
    cb, lb_ = (nct_rows // nb) // r, seq // r
    nsteps = cb + lb_
    lat0 = nct_rows // r

    def fblk(b, s):
        return jnp.where(s < cb, b * cb + s, lat0 + b * lb_ + (s - cb))

    def bblk(b, s):
        return jnp.where(s < cb, b * cb + (cb - 1 - s), lat0 + b * lb_ + (lb_ - 1 - (s - cb)))

    def spec(blk, colblock):
        return pl.BlockSpec((r, HG_DK), lambda b, h, s: (blk(b, s), colblock * HG_HEADS + h))

    lbspec = pl.BlockSpec((1, HG_DK), lambda b, h, s: (0, h))
    gspec = pl.BlockSpec(((HG_LEVELS + 2) * HG_CHUNK, 2 * HG_CHUNK), lambda b, h, s: (0, 0))
    gf = jnp.asarray(_hg_tables(True), jnp.bfloat16)
    gb = jnp.asarray(_hg_tables(False), jnp.bfloat16)
    return pl.pallas_call(
        _hg_scan_kernel,
        out_shape=(jax.ShapeDtypeStruct((t, D), jnp.bfloat16), jax.ShapeDtypeStruct((t, D), jnp.bfloat16)),
        grid=(nb, HG_HEADS, nsteps),
        in_specs=[spec(fblk, 0), spec(fblk, 1), spec(fblk, 2),
                  spec(bblk, 0), spec(bblk, 1), spec(bblk, 3),
                  lbspec, lbspec, gspec, gspec],
        out_specs=(pl.BlockSpec((r, HG_DK), lambda b, h, s: (fblk(b, s), h)),
                   pl.BlockSpec((r, HG_DK), lambda b, h, s: (bblk(b, s), h))),
        scratch_shapes=[pltpu.VMEM((HG_DK, HG_DK), jnp.float32),
                        pltpu.VMEM((HG_DK, HG_DK), jnp.float32),
                        pltpu.VMEM(((HG_LEVELS + 2) * HG_CHUNK, HG_DK), jnp.float32)],
        name="hg_scan",
        compiler_params=_cparams(("arbitrary", "arbitrary", "arbitrary")),
    )(p, p, p, p, p, p, lbf, lbb, gf, gb)


def _hg_out_kernel(of_ref, ob_ref, g_ref, nw_ref, w_ref, x_ref, gate_ref, o_ref, h_ref):
    @pl.when(pl.program_id(1) == 0)
    def _():
        nw = nw_ref[...]
        for h in range(HG_HEADS):
            cs = slice(h * HG_DK, (h + 1) * HG_DK)
            o = of_ref[:, cs].astype(jnp.float32) + ob_ref[:, cs].astype(jnp.float32)
            y = o * lax.rsqrt(jnp.mean(o * o, axis=-1, keepdims=True) + EPS) * nw
            g = g_ref[:, cs]
            h_ref[:, cs] = (y * (g * _sigmoid(g))).astype(jnp.bfloat16)

    acc = jnp.dot(h_ref[...], w_ref[...], preferred_element_type=jnp.float32)
    o_ref[...] = x_ref[...] + gate_ref[...] * acc


def _hg_out(of, ob, p, norm_w, w, x, mods, *, geom, tm, tn):
    nb, nct_rows, seq = geom
    t = x.shape[0]
    nct, lt = nct_rows // tm, seq // tm
    gate = pl.BlockSpec((None, 1, tn), lambda i, j: (_mod_row(i, nct, lt, nb) * 6 + 2, 0, j))
    return pl.pallas_call(
        _hg_out_kernel,
        out_shape=jax.ShapeDtypeStruct((t, D), jnp.float32),
        grid=(t // tm, D // tn),
        in_specs=[pl.BlockSpec((tm, D), lambda i, j: (i, 0)),
                  pl.BlockSpec((tm, D), lambda i, j: (i, 0)),
                  pl.BlockSpec((tm, D), lambda i, j: (i, 4)),
                  pl.BlockSpec((1, HG_DK), lambda i, j: (0, 0)),
                  pl.BlockSpec((D, tn), lambda i, j: (0, j)),
                  pl.BlockSpec((tm, tn), lambda i, j: (i, j)),
                  gate],
        out_specs=pl.BlockSpec((tm, tn), lambda i, j: (i, j)),
        scratch_shapes=[pltpu.VMEM((tm, D), jnp.bfloat16)],
        name="hg_out_proj",
        compiler_params=_cparams(("arbitrary", "arbitrary")),
    )(of, ob, p, norm_w, w, x, mods)


def _resmm_kernel(a_ref, w_ref, x_ref, gate_ref, o_ref):
    acc = jnp.dot(a_ref[...], w_ref[...], preferred_element_type=jnp.float32)
    o_ref[...] = x_ref[...] + gate_ref[...] * acc


def _res_matmul(a, w, x, mods, *, geom, tm, tn):
    nb, nct_rows, seq = geom
    rows = a.shape[0]
    t0 = (x.shape[0] - rows) // tm
    nct, lt = nct_rows // tm, seq // tm
    gate = pl.BlockSpec((None, 1, tn), lambda i, j: (_mod_row(i + t0, nct, lt, nb) * 6 + 2, 0, j))
    return pl.pallas_call(
        _resmm_kernel,
        out_shape=jax.ShapeDtypeStruct((rows, D), jnp.float32),
        grid=(rows // tm, D // tn),
        in_specs=[pl.BlockSpec((tm, a.shape[1]), lambda i, j: (i, 0)),
                  pl.BlockSpec((a.shape[1], tn), lambda i, j: (0, j)),
                  pl.BlockSpec((tm, tn), lambda i, j: (i + t0, j)),
                  gate],
        out_specs=pl.BlockSpec((tm, tn), lambda i, j: (i, j)),
        name="attn_out_proj",
        compiler_params=_cparams(("arbitrary", "arbitrary")),
    )(a, w, x, mods)


def _rope_tables(seq, tm):
    rows = seq // GRID_W
    row = np.repeat(np.arange(rows, dtype=np.float32), GRID_W)
    col = np.tile(np.arange(GRID_W, dtype=np.float32), rows)
    nf = SW_DH // 4
    inv = (ROPE_THETA ** (-np.arange(nf, dtype=np.float32) / nf)).astype(np.float32)
    ang_r = row[:, None] * inv
    ang_c = col[:, None] * inv
    cos = np.concatenate([np.cos(ang_r), np.cos(ang_r), np.cos(ang_c), np.cos(ang_c)], axis=1)
    sin = np.concatenate([-np.sin(ang_r), np.sin(ang_r), -np.sin(ang_c), np.sin(ang_c)], axis=1)
    cos = np.concatenate([np.tile(cos, (1, 2)), np.ones((tm, LANES), np.float32)], axis=0)
    sin = np.concatenate([np.tile(sin, (1, 2)), np.zeros((tm, LANES), np.float32)], axis=0)
    return jnp.asarray(cos, jnp.float32), jnp.asarray(sin, jnp.float32)


def _qkv_kernel(x_ref, sh_ref, sc_ref, w_ref, nw_ref, ones_ref, cos_ref, sin_ref, o_ref, h_ref):
    j = pl.program_id(1)

    @pl.when(j == 0)
    def _():
        h_ref[...] = _modulate(x_ref[...], sh_ref[...], sc_ref[...]).astype(jnp.bfloat16)

    acc = jnp.dot(h_ref[...], w_ref[...], preferred_element_type=jnp.float32)

    @pl.when(j == 2)
    def _():
        o_ref[...] = acc.astype(o_ref.dtype)

    @pl.when(j < 2)
    def _():
        ss = jnp.dot((acc * acc).astype(jnp.bfloat16), ones_ref[...], preferred_element_type=jnp.float32)
        y = acc * lax.rsqrt(ss * (1.0 / SW_DH) + EPS) * nw_ref[...]
        cos = cos_ref[...]
        sin = sin_ref[...]
        lane = lax.broadcasted_iota(jnp.int32, (y.shape[0], LANES), 1)
        first = (lane & 16) == 0
        scale = jnp.where(j == 0, SW_DH ** -0.5, 1.0)
        for g in range(D // LANES):
            cs = slice(g * LANES, (g + 1) * LANES)
            yg = y[:, cs]
            partner = jnp.where(first, pltpu.roll(yg, LANES - 16, 1), pltpu.roll(yg, 16, 1))
            o_ref[:, cs] = ((yg * cos + partner * sin) * scale).astype(o_ref.dtype)


def _qkv_proj(x, mods, w3, nw2, *, geom, tm):
    nb, nct_rows, seq = geom
    t = x.shape[0]
    nct, lt = nct_rows // tm, seq // tm
    cos, sin = _rope_tables(seq, tm)
    blk = np.kron(np.eye(D // SW_DH, dtype=np.float32), np.ones((SW_DH, SW_DH), np.float32))
    ones = jnp.asarray(blk, jnp.bfloat16)
    tab = lambda i, j: (jnp.where(i < nct, lt, (i - nct) % lt), 0)
    return pl.pallas_call(
        _qkv_kernel,
        out_shape=jax.ShapeDtypeStruct((t, 3 * D), jnp.bfloat16),
        grid=(t // tm, 3),
        in_specs=[pl.BlockSpec((tm, D), lambda i, j: (i, 0)),
                  _mod_spec(0, nct, lt, nb),
                  _mod_spec(1, nct, lt, nb),
                  pl.BlockSpec((D, D), lambda i, j: (0, j)),
                  pl.BlockSpec((None, 1, D), lambda i, j: (jnp.minimum(j, 1), 0, 0)),
                  pl.BlockSpec((D, D), lambda i, j: (0, 0)),
                  pl.BlockSpec((tm, LANES), tab),
                  pl.BlockSpec((tm, LANES), tab)],
        out_specs=pl.BlockSpec((tm, D), lambda i, j: (i, j)),
        scratch_shapes=[pltpu.VMEM((tm, D), jnp.bfloat16)],
        name="qkv_proj",
        compiler_params=_cparams(("arbitrary", "arbitrary")),
    )(x, mods, mods, w3, nw2, ones, cos, sin)


def _attn_body(q, kcat, vcat, valid, sink_ref, kv, o_ref):
    nt = (((1,), (1,)), ((), ()))
    grp = lax.broadcasted_iota(jnp.int32, q.shape, 1) // SW_DH
    out = jnp.zeros(q.shape, jnp.float32)
    for g in range(SW_GROUP):
        sink = sink_ref[kv * SW_GROUP + g]
        s = lax.dot_general(jnp.where(grp == g, q, jnp.zeros_like(q)), kcat, nt,
                            preferred_element_type=jnp.float32)
        if valid is not None:
            s = jnp.where(valid, s, NEG_BIG)
        m = jnp.maximum(jnp.max(s, axis=-1, keepdims=True), sink)
        p = jnp.exp(s - m)
        denom = jnp.sum(p, axis=-1, keepdims=True) + jnp.exp(sink - m)
        og = jnp.dot(p.astype(jnp.bfloat16), vcat, preferred_element_type=jnp.float32) / denom
        out = jnp.where(grp == g, og, out)
    o_ref[...] = out.astype(o_ref.dtype)


def _attn_kernel(sink_ref, q_ref, kp_ref, kc_ref, kn_ref, kx_ref, vp_ref, vc_ref, vn_ref, vx_ref, o_ref,
                 *, nblk, ctx_out):
    kv, j = pl.program_id(1), pl.program_id(2)

    def latent():
        kcat = jnp.concatenate([kp_ref[...], kc_ref[...], kn_ref[...], kx_ref[...]], axis=0)
        vcat = jnp.concatenate([vp_ref[...], vc_ref[...], vn_ref[...], vx_ref[...]], axis=0)
        ns = kcat.shape[0]
        t = lax.broadcasted_iota(jnp.int32, (SW_BLOCK, ns), 0)
        s = lax.broadcasted_iota(jnp.int32, (SW_BLOCK, ns), 1)
        lo = jnp.maximum(t, jnp.where(j > 0, 0, SW_BLOCK))
        hi = jnp.minimum(t + 2 * SW_BLOCK, jnp.where(j < nblk - 1, 3 * SW_BLOCK - 1, 2 * SW_BLOCK - 1))
        valid = ((s >= lo) & (s <= hi)) | (s >= 3 * SW_BLOCK)
        _attn_body(q_ref[...], kcat, vcat, valid, sink_ref, kv, o_ref)

    if not ctx_out:
        latent()
        return
    pl.when(j < nblk)(latent)

    @pl.when(j >= nblk)
    def _():
        _attn_body(q_ref[...], kx_ref[...], vx_ref[...], None, sink_ref, kv, o_ref)


def _attention(qkv, sink, geom, ctx_out):
    nb, nct_rows, seq = geom
    ctx_len = nct_rows // nb
    nblk = seq // SW_BLOCK
    ncb = ctx_len // SW_BLOCK
    lat0 = nct_rows // SW_BLOCK
    w = SW_GROUP * SW_DH
    smem = pl.BlockSpec(memory_space=pltpu.SMEM)

    def lat(off, colblock):
        return pl.BlockSpec((SW_BLOCK, w), lambda b, kv, j: (
            lat0 + b * nblk + jnp.clip(j + off, 0, nblk - 1), colblock * SW_KV + kv))

    def ctxkv(colblock):
        return pl.BlockSpec((ctx_len, w), lambda b, kv, j: (b, colblock * SW_KV + kv))

    def qrow(b, j):
        return jnp.where(j < nblk, lat0 + b * nblk + j, b * ncb + (j - nblk))

    if ctx_out:
        steps, out_rows = nblk + ncb, qkv.shape[0]
        out_spec = pl.BlockSpec((SW_BLOCK, w), lambda b, kv, j: (qrow(b, j), kv))
    else:
        steps, out_rows = nblk, nb * seq
        out_spec = pl.BlockSpec((SW_BLOCK, w), lambda b, kv, j: (b * nblk + j, kv))
    return pl.pallas_call(
        functools.partial(_attn_kernel, nblk=nblk, ctx_out=ctx_out),
        out_shape=jax.ShapeDtypeStruct((out_rows, D), jnp.bfloat16),
        grid=(nb, SW_KV, steps),
        in_specs=[smem, pl.BlockSpec((SW_BLOCK, w), lambda b, kv, j: (qrow(b, j), kv)),
                  lat(-1, 1), lat(0, 1), lat(1, 1), ctxkv(1),
                  lat(-1, 2), lat(0, 2), lat(1, 2), ctxkv(2)],
        out_specs=out_spec,
        name="attention",
        compiler_params=_cparams(("arbitrary", "arbitrary", "arbitrary")),
    )(sink, qkv, qkv, qkv, qkv, qkv, qkv, qkv, qkv, qkv)


def _router_kernel(x_ref, sh_ref, sc_ref, r_ref, h_ref, info_ref):
    h = _modulate(x_ref[...], sh_ref[...], sc_ref[...])
    h_ref[...] = h
    logits = jnp.dot(h, r_ref[...], precision=lax.Precision.HIGHEST, preferred_element_type=jnp.float32)
    lane = lax.broadcasted_iota(jnp.int32, logits.shape, 1).astype(jnp.float32)
    logits = jnp.where(lane < N_EXPERTS, logits, -jnp.inf)
    l1 = jnp.max(logits, axis=-1, keepdims=True)
    i1 = jnp.min(jnp.where(logits == l1, lane, float(LANES)), axis=-1, keepdims=True)
    rest = jnp.where(lane == i1, -jnp.inf, logits)
    l2 = jnp.max(rest, axis=-1, keepdims=True)
    i2 = jnp.min(jnp.where(rest == l2, lane, float(LANES)), axis=-1, keepdims=True)
    w1 = 1.0 / (1.0 + jnp.exp(l2 - l1))
    info_ref[...] = jnp.where(lane == 0.0, i1, jnp.where(lane == 1.0, i2, jnp.where(lane == 2.0, w1, 1.0 - w1)))


def _router(x, mods, router_pad, *, geom, tm):
    nb, nct_rows, seq = geom
    t = x.shape[0]
    nct, lt = nct_rows // tm, seq // tm
    return pl.pallas_call(
        _router_kernel,
        out_shape=(jax.ShapeDtypeStruct((t, D), jnp.float32), jax.ShapeDtypeStruct((t, LANES), jnp.float32)),
        grid=(t // tm,),
        in_specs=[pl.BlockSpec((tm, D), lambda i: (i, 0)), _mod_spec(3, nct, lt, nb), _mod_spec(4, nct, lt, nb),
                  pl.BlockSpec((D, LANES), lambda i: (0, 0))],
        out_specs=(pl.BlockSpec((tm, D), lambda i: (i, 0)), pl.BlockSpec((tm, LANES), lambda i: (i, 0))),
        name="moe_router",
        compiler_params=_cparams(("arbitrary",)),
    )(x, mods, mods, router_pad)


def _swiglu_step(h, wg_ref, wu_ref, wd_ref, row_scale):
    g = jnp.dot(h, wg_ref[...], preferred_element_type=jnp.float32)
    u = jnp.dot(h, wu_ref[...], preferred_element_type=jnp.float32)
    a = g * _sigmoid(g) * u
    if row_scale is not None:
        a = a * row_scale
    return jnp.dot(a.astype(jnp.bfloat16), wd_ref[...], preferred_element_type=jnp.float32)


def _ffn_kernel(x_ref, sh_ref, sc_ref, gate_ref, wg_ref, wu_ref, wd_ref, o_ref, h_ref, acc_ref):
    f = pl.program_id(1)

    @pl.when(f == 0)
    def _():
        h_ref[...] = _modulate(x_ref[...], sh_ref[...], sc_ref[...]).astype(jnp.bfloat16)
        acc_ref[...] = jnp.zeros_like(acc_ref)

    acc_ref[...] += _swiglu_step(h_ref[...], wg_ref, wu_ref, wd_ref, None)

    @pl.when(f == pl.num_programs(1) - 1)
    def _():
        o_ref[...] = x_ref[...] + gate_ref[...] * acc_ref[...]


def _ffn(x, mods, wgu, wd, *, geom, tm, tf):
    nb, nct_rows, seq = geom
    t = x.shape[0]
    nf = wd.shape[0] // tf
    nct, lt = nct_rows // tm, seq // tm
    return pl.pallas_call(
        _ffn_kernel,
        out_shape=jax.ShapeDtypeStruct((t, D), jnp.float32),
        grid=(t // tm, nf),
        in_specs=[pl.BlockSpec((tm, D), lambda i, f: (i, 0)),
                  _mod_spec(3, nct, lt, nb),
                  _mod_spec(4, nct, lt, nb),
                  _mod_spec(5, nct, lt, nb),
                  pl.BlockSpec((D, tf), lambda i, f: (0, f)),
                  pl.BlockSpec((D, tf), lambda i, f: (0, nf + f)),
                  pl.BlockSpec((tf, D), lambda i, f: (f, 0))],
        out_specs=pl.BlockSpec((tm, D), lambda i, f: (i, 0)),
        scratch_shapes=[pltpu.VMEM((tm, D), jnp.bfloat16), pltpu.VMEM((tm, D), jnp.float32)],
        name="dense_ffn",
        compiler_params=_cparams(("arbitrary", "arbitrary")),
    )(x, mods, mods, mods, wgu, wgu, wd)


def _row_gather_start(src_hbm, rows_ref, buf, sem, first, count):
    for r in range(count):
        pltpu.make_async_copy(src_hbm.at[pl.ds(rows_ref[0, first + r], 1), :],
                              buf.at[pl.ds(first + r, 1), :], sem).start()


def _moe_kernel(te_ref, nt_ref, rows0_ref, rows1_ref, ws_ref, h_hbm, wg_ref, wu_ref, wd_ref, o_ref,
                hbuf, hb_ref, acc_ref, sem, *, tm, chunk):
    del te_ref
    i, f = pl.program_id(0), pl.program_id(1)
    slot = i % 2

    @pl.when((i == 0) & (f == 0))
    def _():
        _row_gather_start(h_hbm, rows0_ref, hbuf.at[0], sem.at[0], 0, tm)

    @pl.when(f == 0)
    def _():
        pltpu.make_async_copy(hbuf.at[slot], hbuf.at[slot], sem.at[slot]).wait()
        hb_ref[...] = hbuf[slot].astype(jnp.bfloat16)
        acc_ref[...] = jnp.zeros_like(acc_ref)

    @pl.when(i + 1 < pl.num_programs(0))
    def _():
        _row_gather_start(h_hbm, rows1_ref, hbuf.at[1 - slot], sem.at[1 - slot], f * chunk, chunk)

    @pl.when(i < nt_ref[0])
    def _():
        acc_ref[...] += _swiglu_step(hb_ref[...], wg_ref, wu_ref, wd_ref, ws_ref[...])

    @pl.when(f == pl.num_programs(1) - 1)
    def _():
        o_ref[...] = acc_ref[...]


def _moe_experts(h, tile_expert, n_tiles, rows, ws, wgu, wd, *, tm, tf):
    nt = rows.shape[0]
    nf = wd.shape[1] // tf
    chunk = tm // nf
    assert chunk * nf == tm
    grid_spec = pltpu.PrefetchScalarGridSpec(
        num_scalar_prefetch=2,
        grid=(nt, nf),
        in_specs=[pl.BlockSpec((None, 1, tm), lambda i, f, te, n: (i, 0, 0), memory_space=pltpu.SMEM),
                  pl.BlockSpec((None, 1, tm), lambda i, f, te, n: (jnp.minimum(i + 1, nt - 1), 0, 0),
                               memory_space=pltpu.SMEM),
                  pl.BlockSpec((tm, 1), lambda i, f, te, n: (i, 0)),
                  pl.BlockSpec(memory_space=pl.ANY),
                  pl.BlockSpec((None, D, tf), lambda i, f, te, n: (te[i], 0, f)),
                  pl.BlockSpec((None, D, tf), lambda i, f, te, n: (te[i], 0, nf + f)),
                  pl.BlockSpec((None, tf, D), lambda i, f, te, n: (te[i], f, 0))],
        out_specs=pl.BlockSpec((tm, D), lambda i, f, te, n: (i, 0)),
        scratch_shapes=[pltpu.VMEM((2, tm, D), jnp.float32), pltpu.VMEM((tm, D), jnp.bfloat16),
                        pltpu.VMEM((tm, D), jnp.float32), pltpu.SemaphoreType.DMA((2,))])
    return pl.pallas_call(
        functools.partial(_moe_kernel, tm=tm, chunk=chunk),
        out_shape=jax.ShapeDtypeStruct((nt * tm, D), jnp.float32),
        grid_spec=grid_spec,
        name="moe_experts",
        compiler_params=_cparams(("arbitrary", "arbitrary")),
    )(tile_expert, n_tiles, rows, rows, ws, h, wgu, wgu, wd)


def _combine_kernel(rows0_ref, rows1_ref, ys_hbm, x_ref, gate_ref, o_ref, buf, sem, *, tmc):
    i = pl.program_id(0)
    slot = i % 2
    n = 2 * tmc

    def start(rows_ref, s):
        def body(r, carry):
            pltpu.make_async_copy(ys_hbm.at[pl.ds(rows_ref[0, r], 1), :], buf.at[s, pl.ds(r, 1), :],
                                  sem.at[s]).start()
            return carry
        lax.fori_loop(0, n, body, 0, unroll=8)

    @pl.when(i == 0)
    def _():
        start(rows0_ref, 0)

    @pl.when(i + 1 < pl.num_programs(0))
    def _():
        start(rows1_ref, 1 - slot)

    pltpu.make_async_copy(buf.at[slot], buf.at[slot], sem.at[slot]).wait()
    y = buf[slot, 0:tmc, :] + buf[slot, tmc:n, :]
    o_ref[...] = x_ref[...] + gate_ref[...] * y


def _moe_combine(x, ys, rows, mods, *, geom, tmc):
    nb, nct_rows, seq = geom
    t = x.shape[0]
    nct, lt = nct_rows // tmc, seq // tmc
    ntile = rows.shape[0]
    return pl.pallas_call(
        functools.partial(_combine_kernel, tmc=tmc),
        out_shape=jax.ShapeDtypeStruct((t, D), jnp.float32),
        grid=(ntile,),
        in_specs=[pl.BlockSpec((None, 1, 2 * tmc), lambda i: (i, 0, 0), memory_space=pltpu.SMEM),
                  pl.BlockSpec((None, 1, 2 * tmc), lambda i: (jnp.minimum(i + 1, ntile - 1), 0, 0),
                               memory_space=pltpu.SMEM),
                  pl.BlockSpec(memory_space=pl.ANY),
                  pl.BlockSpec((tmc, D), lambda i: (i, 0)),
                  _mod_spec(5, nct, lt, nb)],
        out_specs=pl.BlockSpec((tmc, D), lambda i: (i, 0)),
        scratch_shapes=[pltpu.VMEM((2, 2 * tmc, D), jnp.float32), pltpu.SemaphoreType.DMA((2,))],
        name="moe_combine",
        compiler_params=_cparams(("arbitrary",)),
    )(rows, rows, ys, x, mods)


def _moe(x, mods, router_pad, wgu, wd, *, geom, tm, tf, tmc):
    tr = x.shape[0]
    h, info = _router(x, mods, router_pad, geom=geom, tm=tm)
    e = info[:, 0:2].astype(jnp.int32).reshape(-1)
    w = info[:, 2:4].reshape(-1)
    onehot = (e[:, None] == jnp.arange(N_EXPERTS, dtype=jnp.int32)[None, :]).astype(jnp.int32)
    csum = jnp.cumsum(onehot, axis=0)
    rank = jnp.sum(onehot * (csum - 1), axis=1)
    counts = csum[-1]
    padded = ((counts + tm - 1) // tm) * tm
    ends = jnp.cumsum(padded)
    dest = (ends - padded)[e] + rank
    nt = 2 * tr // tm + N_EXPERTS
    tile_expert = jnp.minimum(jnp.sum(jnp.arange(nt, dtype=jnp.int32)[:, None] * tm >= ends[None, :], axis=1),
                              N_EXPERTS - 1).astype(jnp.int32)
    n_tiles = (ends[-1:] // tm).astype(jnp.int32)
    token_row = jnp.arange(2 * tr, dtype=jnp.int32) // 2
    src = jnp.zeros((nt * tm,), jnp.int32).at[dest].set(token_row)
    ws = jnp.zeros((nt * tm,), jnp.float32).at[dest].set(w)
    ys = _moe_experts(h, tile_expert, n_tiles, src.reshape(nt, 1, tm), ws.reshape(nt * tm, 1), wgu, wd,
                      tm=tm, tf=tf)
    crow = dest.reshape(tr // tmc, tmc, 2).transpose(0, 2, 1).reshape(tr // tmc, 1, 2 * tmc)
    return _moe_combine(x, ys, crow, mods, geom=geom, tmc=tmc)


def kernel(x, c, ctx, c_ctx, w_mod, b_mod, hg_w_in, hg_lb_logits, hg_norm_w, hg_w_out, sw_w_qkv, sw_q_norm,
           sw_k_norm, sw_sink, sw_w_out, ff_w_gate_up, ff_w_down, moe_router, moe_w_gate_up, moe_w_down):
    nb, seq, _ = x.shape
    ctx_len = ctx.shape[1]
    depth = w_mod.shape[0]
    nct_rows = nb * ctx_len
    geom = (nb, nct_rows, seq)
    tm = 512
    bf = jnp.bfloat16

    xs = jnp.concatenate([ctx.reshape(nct_rows, D), x.reshape(nb * seq, D)], axis=0)
    cpad = jnp.concatenate([c, c_ctx[None, :], jnp.zeros((8 - nb - 1, D), jnp.float32)], axis=0)
    mods_all = _modvecs(cpad, w_mod, b_mod)

    p_lb = jax.nn.softmax(hg_lb_logits.astype(jnp.float32), axis=0)
    lower_bounds = jnp.cumsum(p_lb, axis=0) - p_lb[:1]

    for i in range(depth):
        ctx_live = i < depth - 1
        mods = mods_all[i]
        j = i // 2
        if i % 2 == 0:
            p = _mod_matmul(xs, mods, hg_w_in[j].astype(bf), k_shift=0, geom=geom, tm=tm, tn=512,
                            out_dtype=jnp.float32)
            of, ob = _hg_scan(p, lower_bounds[j, 0:1], lower_bounds[j, 1:2], geom)
            xs = _hg_out(of, ob, p, hg_norm_w[j][None, :], hg_w_out[j].astype(bf), xs, mods,
                         geom=geom, tm=tm, tn=512)
        else:
            wq, wk, wv = jnp.split(sw_w_qkv[j], [D, D + SW_KV * SW_DH], axis=1)
            rep = lambda w: jnp.repeat(w.reshape(D, SW_KV, 1, SW_DH), SW_GROUP, axis=2).reshape(D, D)
            w3 = jnp.concatenate([wq, rep(wk), rep(wv)], axis=1).astype(bf)
            nw2 = jnp.stack([jnp.tile(sw_q_norm[j], SW_HEADS), jnp.tile(sw_k_norm[j], SW_HEADS)])[:, None, :]
            qkv = _qkv_proj(xs, mods, w3, nw2, geom=geom, tm=tm)
            o = _attention(qkv, sw_sink[j], geom, ctx_live)
            xs = _res_matmul(o, sw_w_out[j].astype(bf), xs, mods, geom=geom, tm=tm, tn=512)
            if not ctx_live:
                geom = (nb, 0, seq)
        if i % 2 == 0:
            xs = _ffn(xs, mods, ff_w_gate_up[j].astype(bf), ff_w_down[j].astype(bf), geom=geom, tm=tm, tf=256)
        else:
            rpad = jnp.pad(moe_router[j], ((0, 0), (0, LANES - N_EXPERTS)))
            xs = _moe(xs, mods, rpad, moe_w_gate_up[j].astype(bf), moe_w_down[j].astype(bf),
                      geom=geom, tm=tm, tf=896, tmc=256)
    return xs[xs.shape[0] - nb * seq:].reshape(nb, seq, D)
```

```python
import functools

import numpy as np
import jax
import jax.numpy as jnp
from jax import lax
from jax.experimental import pallas as pl
from jax.experimental.pallas import tpu as pltpu

D = 1024
EPS = 1e-6
NEG_BIG = -1e30
GRID_W = 64
ROPE_THETA = 10000.0

HG_HEADS = 8
HG_DK = 128
HG_FDIM = HG_HEADS * HG_DK
HG_CHUNK = 128
HG_LEVELS = 7
HG_ROWS = 256

SW_HEADS = 16
SW_KV = 4
SW_GROUP = 4
SW_DH = 64
SW_BLOCK = 128

N_EXPERTS = 8
LANES = 128

VMEM_LIMIT = 56 * 1024 * 1024


def _cparams(sem):
    return pltpu.CompilerParams(dimension_semantics=sem, vmem_limit_bytes=VMEM_LIMIT)


def _sigmoid(x):
    return 1.0 / (1.0 + jnp.exp(-x))


def _mod_row(i, nct, lt, nb):
    return jnp.where(i < nct, nb, (i - nct) // lt)


def _mod_spec(k, nct, lt, nb):
    return pl.BlockSpec((None, 1, D), lambda i, *_: (_mod_row(i, nct, lt, nb) * 6 + k, 0, 0))


def _modulate(x, shift, scale):
    ms = jnp.mean(x * x, axis=-1, keepdims=True)
    return (x * lax.rsqrt(ms + EPS)) * (1.0 + scale) + shift


def _modvec_kernel(c_ref, w_ref, b_ref, o_ref):
    c = c_ref[...]
    s = c * _sigmoid(c)
    o_ref[...] = jnp.dot(s, w_ref[...], precision=lax.Precision.HIGHEST,
                         preferred_element_type=jnp.float32) + b_ref[...]


def _modvecs(cpad, w_mod, b_mod):
    depth = w_mod.shape[0]
    tn = 1024
    out = pl.pallas_call(
        _modvec_kernel,
        out_shape=jax.ShapeDtypeStruct((depth, 8, 6 * D), jnp.float32),
        grid=(depth, 6 * D // tn),
        in_specs=[pl.BlockSpec((8, D), lambda l, j: (0, 0)),
                  pl.BlockSpec((None, D, tn), lambda l, j: (l, 0, j)),
                  pl.BlockSpec((None, 1, tn), lambda l, j: (l, 0, j))],
        out_specs=pl.BlockSpec((None, 8, tn), lambda l, j: (l, 0, j)),
        name="adaln_vectors",
        compiler_params=_cparams(("arbitrary", "arbitrary")),
    )(cpad, w_mod, b_mod.reshape(depth, 1, 6 * D))
    return out.reshape(depth, 8 * 6, 1, D)


def _modmm_kernel(x_ref, sh_ref, sc_ref, w_ref, o_ref, h_ref):
    @pl.when(pl.program_id(1) == 0)
    def _():
        h_ref[...] = _modulate(x_ref[...], sh_ref[...], sc_ref[...]).astype(jnp.bfloat16)

    o_ref[...] = jnp.dot(h_ref[...], w_ref[...], preferred_element_type=jnp.float32).astype(o_ref.dtype)


def _mod_matmul(x, mods, w, *, k_shift, geom, tm, tn, out_dtype):
    nb, nct_rows, seq = geom
    t, n = x.shape[0], w.shape[1]
    nct, lt = nct_rows // tm, seq // tm
    return pl.pallas_call(
        _modmm_kernel,
        out_shape=jax.ShapeDtypeStruct((t, n), out_dtype),
        grid=(t // tm, n // tn),
        in_specs=[pl.BlockSpec((tm, D), lambda i, j: (i, 0)),
                  _mod_spec(k_shift, nct, lt, nb),
                  _mod_spec(k_shift + 1, nct, lt, nb),
                  pl.BlockSpec((D, tn), lambda i, j: (0, j))],
        out_specs=pl.BlockSpec((tm, tn), lambda i, j: (i, j)),
        scratch_shapes=[pltpu.VMEM((tm, D), jnp.bfloat16)],
        name="hg_in_proj",
        compiler_params=_cparams(("arbitrary", "arbitrary")),
    )(x, mods, mods, w)


def _hg_tables(fwd):
    c = HG_CHUNK
    g = np.zeros((HG_LEVELS + 2, c, c), np.float32)
    u = np.arange(c)
    for l in range(HG_LEVELS):
        w = 1 << l
        for t in range(c):
            mid = (t // (2 * w)) * 2 * w + w
            upper = t >= mid
            if fwd:
                sel = (u >= mid) & (u <= t) if upper else (u > t) & (u < mid)
            else:
                sel = (u >= mid) & (u < t) if upper else (u >= t) & (u < mid)
            g[l, t, sel] = 1.0
    for t in range(c):
        if fwd:
            g[HG_LEVELS, t, u <= t] = 1.0
            g[HG_LEVELS + 1, t, u > t] = 1.0
        else:
            g[HG_LEVELS, t, u >= t] = 1.0
            g[HG_LEVELS + 1, t, u < t] = 1.0
    g = g.reshape((HG_LEVELS + 2) * c, c)
    return np.concatenate([g, g], axis=1)


def _hg_chunk(q, v, z, lb, g_ref, e_ref, st_ref, fwd):
    c = HG_CHUNK
    sig = _sigmoid(z)
    f = lb + (1.0 - lb) * sig
    lf = jnp.log(jnp.maximum(f, 1e-30))
    k = (1.0 - lb) * (1.0 - sig)
    lf_hi = lf.astype(jnp.bfloat16)
    lf_lo = (lf - lf_hi.astype(jnp.float32)).astype(jnp.bfloat16)
    e_ref[...] = jnp.exp(jnp.dot(g_ref[...], jnp.concatenate([lf_hi, lf_lo], axis=0),
                                 preferred_element_type=jnp.float32))

    row = lax.broadcasted_iota(jnp.int32, (c, c), 0)
    col = lax.broadcasted_iota(jnp.int32, (c, c), 1)
    nt = (((1,), (1,)), ((), ()))
    qb = q.astype(jnp.bfloat16)
    kb = k.astype(jnp.bfloat16)
    scores = jnp.where(row == col, lax.dot_general(qb, kb, nt, preferred_element_type=jnp.float32), 0.0)
    for l in range(HG_LEVELS):
        e = e_ref[l * c:(l + 1) * c, :]
        upper = ((lax.broadcasted_iota(jnp.int32, (c, HG_DK), 0) >> l) & 1) == 1
        q_side = upper if fwd else jnp.logical_not(upper)
        qm = jnp.where(q_side, q * e, 0.0).astype(jnp.bfloat16)
        km = jnp.where(q_side, 0.0, k * e).astype(jnp.bfloat16)
        sc = lax.dot_general(qm, km, nt, preferred_element_type=jnp.float32)
        scores = scores + jnp.where((row >> (l + 1)) == (col >> (l + 1)), sc, 0.0)

    e_in = e_ref[HG_LEVELS * c:(HG_LEVELS + 1) * c, :]
    e_out = e_ref[(HG_LEVELS + 1) * c:(HG_LEVELS + 2) * c, :]
    tot = e_in[c - 1:c, :] if fwd else e_in[0:1, :]
    vb = v.astype(jnp.bfloat16)
    st = st_ref[...]
    o = (jnp.dot(scores.astype(jnp.bfloat16), vb, preferred_element_type=jnp.float32)
         + lax.dot_general((q * e_in).astype(jnp.bfloat16), st.astype(jnp.bfloat16), nt,
                           preferred_element_type=jnp.float32))
    st_ref[...] = tot * st + jnp.dot(v.T.astype(jnp.bfloat16), (k * e_out).astype(jnp.bfloat16),
                                     preferred_element_type=jnp.float32)
    return o


def _hg_scan_kernel(qf_ref, vf_ref, zf_ref, qb_ref, vb_ref, zb_ref, lbf_ref, lbb_ref, gf_ref, gb_ref,
                    of_ref, ob_ref, sf_ref, sb_ref, e_ref):
    @pl.when(pl.program_id(2) == 0)
    def _():
        sf_ref[...] = jnp.zeros_like(sf_ref)
        sb_ref[...] = jnp.zeros_like(sb_ref)

    c = HG_CHUNK
    nch = HG_ROWS // c
    for ci in range(nch):
        r = slice(ci * c, (ci + 1) * c)
        q = qf_ref[r, :]
        o = _hg_chunk(q * _sigmoid(q), vf_ref[r, :], zf_ref[r, :], lbf_ref[...], gf_ref, e_ref, sf_ref, True)
        of_ref[r, :] = o.astype(of_ref.dtype)
    for ci in reversed(range(nch)):
        r = slice(ci * c, (ci + 1) * c)
        q = qb_ref[r, :]
        o = _hg_chunk(q * _sigmoid(q), vb_ref[r, :], zb_ref[r, :], lbb_ref[...], gb_ref, e_ref, sb_ref, False)
        ob_ref[r, :] = o.astype(ob_ref.dtype)


def _hg_scan(p, lbf, lbb, geom):
    nb, nct_rows, seq = geom
    t = p.shape[0]
    r = HG_ROWS
    cb, lb_ = (nct_rows // nb) // r, seq // r
    nsteps = cb + lb_
    lat0 = nct_rows // r

    def fblk(b, s):
        return jnp.where(s < cb, b * cb + s, lat0 + b * lb_ + (s - cb))

    def bblk(b, s):
        return jnp.where(s < cb, b * cb + (cb - 1 - s), lat0 + b * lb_ + (lb_ - 1 - (s - cb)))

    def spec(blk, colblock):
        return pl.BlockSpec((r, HG_DK), lambda b, h, s: (blk(b, s), colblock * HG_HEADS + h))

    lbspec = pl.BlockSpec((1, HG_DK), lambda b, h, s: (0, h))
    gspec = pl.BlockSpec(((HG_LEVELS + 2) * HG_CHUNK, 2 * HG_CHUNK), lambda b, h, s: (0, 0))
    gf = jnp.asarray(_hg_tables(True), jnp.bfloat16)
    gb = jnp.asarray(_hg_tables(False), jnp.bfloat16)
    return pl.pallas_call(
        _hg_scan_kernel,
        out_shape=(jax.ShapeDtypeStruct((t, D), jnp.bfloat16), jax.ShapeDtypeStruct((t, D), jnp.bfloat16)),
        grid=(nb, HG_HEADS, nsteps),
        in_specs=[spec(fblk, 0), spec(fblk, 1), spec(fblk, 2),
                  spec(bblk, 0), spec(bblk, 1), spec(bblk, 3),
                  lbspec, lbspec, gspec, gspec],
        out_specs=(pl.BlockSpec((r, HG_DK), lambda b, h, s: (fblk(b, s), h)),
                   pl.BlockSpec((r, HG_DK), lambda b, h, s: (bblk(b, s), h))),
        scratch_shapes=[pltpu.VMEM((HG_DK, HG_DK), jnp.float32),
                        pltpu.VMEM((HG_DK, HG_DK), jnp.float32),
                        pltpu.VMEM(((HG_LEVELS + 2) * HG_CHUNK, HG_DK), jnp.float32)],
        name="hg_scan",
        compiler_params=_cparams(("arbitrary", "arbitrary", "arbitrary")),
    )(p, p, p, p, p, p, lbf, lbb, gf, gb)


def _hg_out_kernel(of_ref, ob_ref, g_ref, nw_ref, w_ref, x_ref, gate_ref, o_ref, h_ref):
    @pl.when(pl.program_id(1) == 0)
    def _():
        nw = nw_ref[...]
        for h in range(HG_HEADS):
            cs = slice(h * HG_DK, (h + 1) * HG_DK)
            o = of_ref[:, cs].astype(jnp.float32) + ob_ref[:, cs].astype(jnp.float32)
            y = o * lax.rsqrt(jnp.mean(o * o, axis=-1, keepdims=True) + EPS) * nw
            g = g_ref[:, cs]
            h_ref[:, cs] = (y * (g * _sigmoid(g))).astype(jnp.bfloat16)

    acc = jnp.dot(h_ref[...], w_ref[...], preferred_element_type=jnp.float32)
    o_ref[...] = x_ref[...] + gate_ref[...] * acc


def _hg_out(of, ob, p, norm_w, w, x, mods, *, geom, tm, tn):
    nb, nct_rows, seq = geom
    t = x.shape[0]
    nct, lt = nct_rows // tm, seq // tm
    gate = pl.BlockSpec((None, 1, tn), lambda i, j: (_mod_row(i, nct, lt, nb) * 6 + 2, 0, j))
    return pl.pallas_call(
        _hg_out_kernel,
        out_shape=jax.ShapeDtypeStruct((t, D), jnp.float32),
        grid=(t // tm, D // tn),
        in_specs=[pl.BlockSpec((tm, D), lambda i, j: (i, 0)),
                  pl.BlockSpec((tm, D), lambda i, j: (i, 0)),
                  pl.BlockSpec((tm, D), lambda i, j: (i, 4)),
                  pl.BlockSpec((1, HG_DK), lambda i, j: (0, 0)),
                  pl.BlockSpec((D, tn), lambda i, j: (0, j)),
                  pl.BlockSpec((tm, tn), lambda i, j: (i, j)),
                  gate],
        out_specs=pl.BlockSpec((tm, tn), lambda i, j: (i, j)),
        scratch_shapes=[pltpu.VMEM((tm, D), jnp.bfloat16)],
        name="hg_out_proj",
        compiler_params=_cparams(("arbitrary", "arbitrary")),
    )(of, ob, p, norm_w, w, x, mods)


def _resmm_kernel(a_ref, w_ref, x_ref, gate_ref, o_ref):
    acc = jnp.dot(a_ref[...], w_ref[...], preferred_element_type=jnp.float32)
    o_ref[...] = x_ref[...] + gate_ref[...] * acc


def _res_matmul(a, w, x, mods, *, geom, tm, tn):
    nb, nct_rows, seq = geom
    rows = a.shape[0]
    t0 = (x.shape[0] - rows) // tm
    nct, lt = nct_rows // tm, seq // tm
    gate = pl.BlockSpec((None, 1, tn), lambda i, j: (_mod_row(i + t0, nct, lt, nb) * 6 + 2, 0, j))
    return pl.pallas_call(
        _resmm_kernel,
        out_shape=jax.ShapeDtypeStruct((rows, D), jnp.float32),
        grid=(rows // tm, D // tn),
        in_specs=[pl.BlockSpec((tm, a.shape[1]), lambda i, j: (i, 0)),
                  pl.BlockSpec((a.shape[1], tn), lambda i, j: (0, j)),
                  pl.BlockSpec((tm, tn), lambda i, j: (i + t0, j)),
                  gate],
        out_specs=pl.BlockSpec((tm, tn), lambda i, j: (i, j)),
        name="attn_out_proj",
        compiler_params=_cparams(("arbitrary", "arbitrary")),
    )(a, w, x, mods)


def _rope_tables(seq, tm):
    rows = seq // GRID_W
    row = np.repeat(np.arange(rows, dtype=np.float32), GRID_W)
    col = np.tile(np.arange(GRID_W, dtype=np.float32), rows)
    nf = SW_DH // 4
    inv = (ROPE_THETA ** (-np.arange(nf, dtype=np.float32) / nf)).astype(np.float32)
    ang_r = row[:, None] * inv
    ang_c = col[:, None] * inv
    cos = np.concatenate([np.cos(ang_r), np.cos(ang_r), np.cos(ang_c), np.cos(ang_c)], axis=1)
    sin = np.concatenate([-np.sin(ang_r), np.sin(ang_r), -np.sin(ang_c), np.sin(ang_c)], axis=1)
    cos = np.concatenate([np.tile(cos, (1, 2)), np.ones((tm, LANES), np.float32)], axis=0)
    sin = np.concatenate([np.tile(sin, (1, 2)), np.zeros((tm, LANES), np.float32)], axis=0)
    return jnp.asarray(cos, jnp.float32), jnp.asarray(sin, jnp.float32)


def _qkv_kernel(x_ref, sh_ref, sc_ref, w_ref, nw_ref, ones_ref, cos_ref, sin_ref, o_ref, h_ref):
    j = pl.program_id(1)

    @pl.when(j == 0)
    def _():
        h_ref[...] = _modulate(x_ref[...], sh_ref[...], sc_ref[...]).astype(jnp.bfloat16)

    acc = jnp.dot(h_ref[...], w_ref[...], preferred_element_type=jnp.float32)

    @pl.when(j == 2)
    def _():
        o_ref[...] = acc.astype(o_ref.dtype)

    @pl.when(j < 2)
    def _():
        ss = jnp.dot((acc * acc).astype(jnp.bfloat16), ones_ref[...], preferred_element_type=jnp.float32)
        y = acc * lax.rsqrt(ss * (1.0 / SW_DH) + EPS) * nw_ref[...]
        cos = cos_ref[...]
        sin = sin_ref[...]
        lane = lax.broadcasted_iota(jnp.int32, (y.shape[0], LANES), 1)
        first = (lane & 16) == 0
        scale = jnp.where(j == 0, SW_DH ** -0.5, 1.0)
        for g in range(D // LANES):
            cs = slice(g * LANES, (g + 1) * LANES)
            yg = y[:, cs]
            partner = jnp.where(first, pltpu.roll(yg, LANES - 16, 1), pltpu.roll(yg, 16, 1))
            o_ref[:, cs] = ((yg * cos + partner * sin) * scale).astype(o_ref.dtype)


def _qkv_proj(x, mods, w3, nw2, *, geom, tm):
    nb, nct_rows, seq = geom
    t = x.shape[0]
    nct, lt = nct_rows // tm, seq // tm
    cos, sin = _rope_tables(seq, tm)
    blk = np.kron(np.eye(D // SW_DH, dtype=np.float32), np.ones((SW_DH, SW_DH), np.float32))
    ones = jnp.asarray(blk, jnp.bfloat16)
    tab = lambda i, j: (jnp.where(i < nct, lt, (i - nct) % lt), 0)
    return pl.pallas_call(
        _qkv_kernel,
        out_shape=jax.ShapeDtypeStruct((t, 3 * D), jnp.bfloat16),
        grid=(t // tm, 3),
        in_specs=[pl.BlockSpec((tm, D), lambda i, j: (i, 0)),
                  _mod_spec(0, nct, lt, nb),
                  _mod_spec(1, nct, lt, nb),
                  pl.BlockSpec((D, D), lambda i, j: (0, j)),
                  pl.BlockSpec((None, 1, D), lambda i, j: (jnp.minimum(j, 1), 0, 0)),
                  pl.BlockSpec((D, D), lambda i, j: (0, 0)),
                  pl.BlockSpec((tm, LANES), tab),
                  pl.BlockSpec((tm, LANES), tab)],
        out_specs=pl.BlockSpec((tm, D), lambda i, j: (i, j)),
        scratch_shapes=[pltpu.VMEM((tm, D), jnp.bfloat16)],
        name="qkv_proj",
        compiler_params=_cparams(("arbitrary", "arbitrary")),
    )(x, mods, mods, w3, nw2, ones, cos, sin)


def _attn_body(q, kcat, vcat, valid, sink_ref, kv, o_ref):
    nt = (((1,), (1,)), ((), ()))
    grp = lax.broadcasted_iota(jnp.int32, q.shape, 1) // SW_DH
    out = jnp.zeros(q.shape, jnp.float32)
    for g in range(SW_GROUP):
        sink = sink_ref[kv * SW_GROUP + g]
        s = lax.dot_general(jnp.where(grp == g, q, jnp.zeros_like(q)), kcat, nt,
                            preferred_element_type=jnp.float32)
        if valid is not None:
            s = jnp.where(valid, s, NEG_BIG)
        m = jnp.maximum(jnp.max(s, axis=-1, keepdims=True), sink)
        p = jnp.exp(s - m)
        denom = jnp.sum(p, axis=-1, keepdims=True) + jnp.exp(sink - m)
        og = jnp.dot(p.astype(jnp.bfloat16), vcat, preferred_element_type=jnp.float32) / denom
        out = jnp.where(grp == g, og, out)
    o_ref[...] = out.astype(o_ref.dtype)


def _attn_kernel(sink_ref, q_ref, kp_ref, kc_ref, kn_ref, kx_ref, vp_ref, vc_ref, vn_ref, vx_ref, o_ref,
                 *, nblk, ctx_out):
    kv, j = pl.program_id(1), pl.program_id(2)

    def latent():
        kcat = jnp.concatenate([kp_ref[...], kc_ref[...], kn_ref[...], kx_ref[...]], axis=0)
        vcat = jnp.concatenate([vp_ref[...], vc_ref[...], vn_ref[...], vx_ref[...]], axis=0)
        ns = kcat.shape[0]
        t = lax.broadcasted_iota(jnp.int32, (SW_BLOCK, ns), 0)
        s = lax.broadcasted_iota(jnp.int32, (SW_BLOCK, ns), 1)
        lo = jnp.maximum(t, jnp.where(j > 0, 0, SW_BLOCK))
        hi = jnp.minimum(t + 2 * SW_BLOCK, jnp.where(j < nblk - 1, 3 * SW_BLOCK - 1, 2 * SW_BLOCK - 1))
        valid = ((s >= lo) & (s <= hi)) | (s >= 3 * SW_BLOCK)
        _attn_body(q_ref[...], kcat, vcat, valid, sink_ref, kv, o_ref)

    if not ctx_out:
        latent()
        return
    pl.when(j < nblk)(latent)

    @pl.when(j >= nblk)
    def _():
        _attn_body(q_ref[...], kx_ref[...], vx_ref[...], None, sink_ref, kv, o_ref)


def _attention(qkv, sink, geom, ctx_out):
    nb, nct_rows, seq = geom
    ctx_len = nct_rows // nb
    nblk = seq // SW_BLOCK
    ncb = ctx_len // SW_BLOCK
    lat0 = nct_rows // SW_BLOCK
    w = SW_GROUP * SW_DH
    smem = pl.BlockSpec(memory_space=pltpu.SMEM)

    def lat(off, colblock):
        return pl.BlockSpec((SW_BLOCK, w), lambda b, kv, j: (
            lat0 + b * nblk + jnp.clip(j + off, 0, nblk - 1), colblock * SW_KV + kv))

    def ctxkv(colblock):
        return pl.BlockSpec((ctx_len, w), lambda b, kv, j: (b, colblock * SW_KV + kv))

    def qrow(b, j):
        return jnp.where(j < nblk, lat0 + b * nblk + j, b * ncb + (j - nblk))

    if ctx_out:
        steps, out_rows = nblk + ncb, qkv.shape[0]
        out_spec = pl.BlockSpec((SW_BLOCK, w), lambda b, kv, j: (qrow(b, j), kv))
    else:
        steps, out_rows = nblk, nb * seq
        out_spec = pl.BlockSpec((SW_BLOCK, w), lambda b, kv, j: (b * nblk + j, kv))
    return pl.pallas_call(
        functools.partial(_attn_kernel, nblk=nblk, ctx_out=ctx_out),
        out_shape=jax.ShapeDtypeStruct((out_rows, D), jnp.bfloat16),
        grid=(nb, SW_KV, steps),
        in_specs=[smem, pl.BlockSpec((SW_BLOCK, w), lambda b, kv, j: (qrow(b, j), kv)),
                  lat(-1, 1), lat(0, 1), lat(1, 1), ctxkv(1),
                  lat(-1, 2), lat(0, 2), lat(1, 2), ctxkv(2)],
        out_specs=out_spec,
        name="attention",
        compiler_params=_cparams(("arbitrary", "arbitrary", "arbitrary")),
    )(sink, qkv, qkv, qkv, qkv, qkv, qkv, qkv, qkv, qkv)


def _router_kernel(x_ref, sh_ref, sc_ref, r_ref, h_ref, info_ref):
    h = _modulate(x_ref[...], sh_ref[...], sc_ref[...])
    h_ref[...] = h
    logits = jnp.dot(h, r_ref[...], precision=lax.Precision.HIGHEST, preferred_element_type=jnp.float32)
    lane = lax.broadcasted_iota(jnp.int32, logits.shape, 1).astype(jnp.float32)
    logits = jnp.where(lane < N_EXPERTS, logits, -jnp.inf)
    l1 = jnp.max(logits, axis=-1, keepdims=True)
    i1 = jnp.min(jnp.where(logits == l1, lane, float(LANES)), axis=-1, keepdims=True)
    rest = jnp.where(lane == i1, -jnp.inf, logits)
    l2 = jnp.max(rest, axis=-1, keepdims=True)
    i2 = jnp.min(jnp.where(rest == l2, lane, float(LANES)), axis=-1, keepdims=True)
    w1 = 1.0 / (1.0 + jnp.exp(l2 - l1))
    info_ref[...] = jnp.where(lane == 0.0, i1, jnp.where(lane == 1.0, i2, jnp.where(lane == 2.0, w1, 1.0 - w1)))


def _router(x, mods, router_pad, *, geom, tm):
    nb, nct_rows, seq = geom
    t = x.shape[0]
    nct, lt = nct_rows // tm, seq // tm
    return pl.pallas_call(
        _router_kernel,
        out_shape=(jax.ShapeDtypeStruct((t, D), jnp.float32), jax.ShapeDtypeStruct((t, LANES), jnp.float32)),
        grid=(t // tm,),
        in_specs=[pl.BlockSpec((tm, D), lambda i: (i, 0)), _mod_spec(3, nct, lt, nb), _mod_spec(4, nct, lt, nb),
                  pl.BlockSpec((D, LANES), lambda i: (0, 0))],
        out_specs=(pl.BlockSpec((tm, D), lambda i: (i, 0)), pl.BlockSpec((tm, LANES), lambda i: (i, 0))),
        name="moe_router",
        compiler_params=_cparams(("arbitrary",)),
    )(x, mods, mods, router_pad)


def _swiglu_step(h, wg_ref, wu_ref, wd_ref, row_scale):
    g = jnp.dot(h, wg_ref[...], preferred_element_type=jnp.float32)
    u = jnp.dot(h, wu_ref[...], preferred_element_type=jnp.float32)
    a = g * _sigmoid(g) * u
    if row_scale is not None:
        a = a * row_scale
    return jnp.dot(a.astype(jnp.bfloat16), wd_ref[...], preferred_element_type=jnp.float32)


def _ffn_kernel(x_ref, sh_ref, sc_ref, gate_ref, wg_ref, wu_ref, wd_ref, o_ref, h_ref, acc_ref):
    f = pl.program_id(1)

    @pl.when(f == 0)
    def _():
        h_ref[...] = _modulate(x_ref[...], sh_ref[...], sc_ref[...]).astype(jnp.bfloat16)
        acc_ref[...] = jnp.zeros_like(acc_ref)

    acc_ref[...] += _swiglu_step(h_ref[...], wg_ref, wu_ref, wd_ref, None)

    @pl.when(f == pl.num_programs(1) - 1)
    def _():
        o_ref[...] = x_ref[...] + gate_ref[...] * acc_ref[...]


def _ffn(x, mods, wgu, wd, *, geom, tm, tf):
    nb, nct_rows, seq = geom
    t = x.shape[0]
    nf = wd.shape[0] // tf
    nct, lt = nct_rows // tm, seq // tm
    return pl.pallas_call(
        _ffn_kernel,
        out_shape=jax.ShapeDtypeStruct((t, D), jnp.float32),
        grid=(t // tm, nf),
        in_specs=[pl.BlockSpec((tm, D), lambda i, f: (i, 0)),
                  _mod_spec(3, nct, lt, nb),
                  _mod_spec(4, nct, lt, nb),
                  _mod_spec(5, nct, lt, nb),
                  pl.BlockSpec((D, tf), lambda i, f: (0, f)),
                  pl.BlockSpec((D, tf), lambda i, f: (0, nf + f)),
                  pl.BlockSpec((tf, D), lambda i, f: (f, 0))],
        out_specs=pl.BlockSpec((tm, D), lambda i, f: (i, 0)),
        scratch_shapes=[pltpu.VMEM((tm, D), jnp.bfloat16), pltpu.VMEM((tm, D), jnp.float32)],
        name="dense_ffn",
        compiler_params=_cparams(("arbitrary", "arbitrary")),
    )(x, mods, mods, mods, wgu, wgu, wd)


def _row_gather_start(src_hbm, rows_ref, buf, sem, first, count):
    for r in range(count):
        pltpu.make_async_copy(src_hbm.at[pl.ds(rows_ref[0, first + r], 1), :],
                              buf.at[pl.ds(first + r, 1), :], sem).start()


def _moe_kernel(te_ref, nt_ref, rows0_ref, rows1_ref, h_hbm, wg_ref, wu_ref, wd_ref, o_ref,
                hbuf, hb_ref, acc_ref, sem, *, tm, chunk):
    del te_ref
    i, f = pl.program_id(0), pl.program_id(1)
    last_i, last_f = pl.num_programs(0) - 1, pl.num_programs(1) - 1
    slot = i % 2
    nt = nt_ref[0]
    wait_all = lambda s: pltpu.make_async_copy(hbuf.at[s], hbuf.at[s], sem.at[s]).wait()

    @pl.when((i == 0) & (f == 0))
    def _():
        _row_gather_start(h_hbm, rows0_ref, hbuf.at[0], sem.at[0], 0, tm)

    @pl.when((f == 0) & (i <= nt))
    def _():
        wait_all(slot)
        hb_ref[...] = hbuf[slot].astype(jnp.bfloat16)
        acc_ref[...] = jnp.zeros_like(acc_ref)

    @pl.when(i < nt)
    def _():
        _row_gather_start(h_hbm, rows1_ref, hbuf.at[1 - slot], sem.at[1 - slot], f * chunk, chunk)
        acc_ref[...] += _swiglu_step(hb_ref[...], wg_ref, wu_ref, wd_ref, None)

    @pl.when(f == last_f)
    def _():
        o_ref[...] = acc_ref[...]

    @pl.when((i == last_i) & (f == last_f) & (i < nt))
    def _():
        wait_all(1 - slot)


def _moe_experts(h, tile_expert, n_tiles, rows, wgu, wd, *, tm, tf):
    nt = rows.shape[0]
    nf = wd.shape[1] // tf
    chunk = tm // nf
    assert chunk * nf == tm
    grid_spec = pltpu.PrefetchScalarGridSpec(
        num_scalar_prefetch=2,
        grid=(nt, nf),
        in_specs=[pl.BlockSpec((None, 1, tm), lambda i, f, te, n: (i, 0, 0), memory_space=pltpu.SMEM),
                  pl.BlockSpec((None, 1, tm), lambda i, f, te, n: (jnp.minimum(i + 1, nt - 1), 0, 0),
                               memory_space=pltpu.SMEM),
                  pl.BlockSpec(memory_space=pl.ANY),
                  pl.BlockSpec((None, D, tf), lambda i, f, te, n: (te[i], 0, f)),
                  pl.BlockSpec((None, D, tf), lambda i, f, te, n: (te[i], 0, nf + f)),
                  pl.BlockSpec((None, tf, D), lambda i, f, te, n: (te[i], f, 0))],
        out_specs=pl.BlockSpec((tm, D), lambda i, f, te, n: (i, 0)),
        scratch_shapes=[pltpu.VMEM((2, tm, D), jnp.float32), pltpu.VMEM((tm, D), jnp.bfloat16),
                        pltpu.VMEM((tm, D), jnp.float32), pltpu.SemaphoreType.DMA((2,))])
    return pl.pallas_call(
        functools.partial(_moe_kernel, tm=tm, chunk=chunk),
        out_shape=jax.ShapeDtypeStruct((nt * tm, D), jnp.float32),
        grid_spec=grid_spec,
        name="moe_experts",
        compiler_params=_cparams(("arbitrary", "arbitrary")),
    )(tile_expert, n_tiles, rows, rows, h, wgu, wgu, wd)


def _combine_kernel(rows0_ref, rows1_ref, ys_hbm, x_ref, gate_ref, info_ref, o_ref, buf, sem, *, tmc):
    i = pl.program_id(0)
    slot = i % 2
    n = 2 * tmc

    def start(rows_ref, s):
        def body(r, carry):
            pltpu.make_async_copy(ys_hbm.at[pl.ds(rows_ref[0, r], 1), :], buf.at[s, pl.ds(r, 1), :],
                                  sem.at[s]).start()
            return carry
        lax.fori_loop(0, n, body, 0, unroll=8)

    @pl.when(i == 0)
    def _():
        start(rows0_ref, 0)

    @pl.when(i + 1 < pl.num_programs(0))
    def _():
        start(rows1_ref, 1 - slot)

    pltpu.make_async_copy(buf.at[slot], buf.at[slot], sem.at[slot]).wait()
    info = info_ref[...]
    y = info[:, 2:3] * buf[slot, 0:tmc, :] + info[:, 3:4] * buf[slot, tmc:n, :]
    o_ref[...] = x_ref[...] + gate_ref[...] * y


def _moe_combine(x, ys, rows, info, mods, *, geom, tmc):
    nb, nct_rows, seq = geom
    t = x.shape[0]
    nct, lt = nct_rows // tmc, seq // tmc
    ntile = rows.shape[0]
    return pl.pallas_call(
        functools.partial(_combine_kernel, tmc=tmc),
        out_shape=jax.ShapeDtypeStruct((t, D), jnp.float32),
        grid=(ntile,),
        in_specs=[pl.BlockSpec((None, 1, 2 * tmc), lambda i: (i, 0, 0), memory_space=pltpu.SMEM),
                  pl.BlockSpec((None, 1, 2 * tmc), lambda i: (jnp.minimum(i + 1, ntile - 1), 0, 0),
                               memory_space=pltpu.SMEM),
                  pl.BlockSpec(memory_space=pl.ANY),
                  pl.BlockSpec((tmc, D), lambda i: (i, 0)),
                  _mod_spec(5, nct, lt, nb),
                  pl.BlockSpec((tmc, LANES), lambda i: (i, 0))],
        out_specs=pl.BlockSpec((tmc, D), lambda i: (i, 0)),
        scratch_shapes=[pltpu.VMEM((2, 2 * tmc, D), jnp.float32), pltpu.SemaphoreType.DMA((2,))],
        name="moe_combine",
        compiler_params=_cparams(("arbitrary",)),
    )(rows, rows, ys, x, mods, info)


def _moe(x, mods, router_pad, wgu, wd, *, geom, tm, tf, tmc):
    tr = x.shape[0]
    h, info = _router(x, mods, router_pad, geom=geom, tm=tm)
    e = info[:, 0:2].astype(jnp.int32).reshape(-1)
    onehot = (e[:, None] == jnp.arange(N_EXPERTS, dtype=jnp.int32)[None, :]).astype(jnp.int32)
    csum = jnp.cumsum(onehot, axis=0)
    rank = jnp.sum(onehot * (csum - 1), axis=1)
    counts = csum[-1]
    padded = ((counts + tm - 1) // tm) * tm
    ends = jnp.cumsum(padded)
    dest = (ends - padded)[e] + rank
    nt = 2 * tr // tm + N_EXPERTS
    tile_expert = jnp.minimum(jnp.sum(jnp.arange(nt, dtype=jnp.int32)[:, None] * tm >= ends[None, :], axis=1),
                              N_EXPERTS - 1).astype(jnp.int32)
    n_tiles = (ends[-1:] // tm).astype(jnp.int32)
    token_row = jnp.arange(2 * tr, dtype=jnp.int32) // 2
    src = jnp.zeros((nt * tm,), jnp.int32).at[dest].set(token_row)
    ys = _moe_experts(h, tile_expert, n_tiles, src.reshape(nt, 1, tm), wgu, wd, tm=tm, tf=tf)
    crow = dest.reshape(tr // tmc, tmc, 2).transpose(0, 2, 1).reshape(tr // tmc, 1, 2 * tmc)
    return _moe_combine(x, ys, crow, info, mods, geom=geom, tmc=tmc)


def kernel(x, c, ctx, c_ctx, w_mod, b_mod, hg_w_in, hg_lb_logits, hg_norm_w, hg_w_out, sw_w_qkv, sw_q_norm,
           sw_k_norm, sw_sink, sw_w_out, ff_w_gate_up, ff_w_down, moe_router, moe_w_gate_up, moe_w_down):
    nb, seq, _ = x.shape
    ctx_len = ctx.shape[1]
    depth = w_mod.shape[0]
    nct_rows = nb * ctx_len
    geom = (nb, nct_rows, seq)
    tm = 512
    tmb = 1024 if nct_rows % 1024 == 0 and seq % 1024 == 0 else tm
    bf = jnp.bfloat16

    xs = jnp.concatenate([ctx.reshape(nct_rows, D), x.reshape(nb * seq, D)], axis=0)
    cpad = jnp.concatenate([c, c_ctx[None, :], jnp.zeros((8 - nb - 1, D), jnp.float32)], axis=0)
    mods_all = _modvecs(cpad, w_mod, b_mod)

    p_lb = jax.nn.softmax(hg_lb_logits.astype(jnp.float32), axis=0)
    lower_bounds = jnp.cumsum(p_lb, axis=0) - p_lb[:1]

    for i in range(depth):
        ctx_live = i < depth - 1
        mods = mods_all[i]
        j = i // 2
        if i % 2 == 0:
            p = _mod_matmul(xs, mods, hg_w_in[j].astype(bf), k_shift=0, geom=geom, tm=tmb, tn=1024,
                            out_dtype=jnp.float32)
            of, ob = _hg_scan(p, lower_bounds[j, 0:1], lower_bounds[j, 1:2], geom)
            xs = _hg_out(of, ob, p, hg_norm_w[j][None, :], hg_w_out[j].astype(bf), xs, mods,
                         geom=geom, tm=tmb, tn=1024)
        else:
            wq, wk, wv = jnp.split(sw_w_qkv[j], [D, D + SW_KV * SW_DH], axis=1)
            rep = lambda w: jnp.repeat(w.reshape(D, SW_KV, 1, SW_DH), SW_GROUP, axis=2).reshape(D, D)
            w3 = jnp.concatenate([wq, rep(wk), rep(wv)], axis=1).astype(bf)
            nw2 = jnp.stack([jnp.tile(sw_q_norm[j], SW_HEADS), jnp.tile(sw_k_norm[j], SW_HEADS)])[:, None, :]
            qkv = _qkv_proj(xs, mods, w3, nw2, geom=geom, tm=tmb)
            o = _attention(qkv, sw_sink[j], geom, ctx_live)
            xs = _res_matmul(o, sw_w_out[j].astype(bf), xs, mods, geom=geom, tm=tmb, tn=1024)
            if not ctx_live:
                geom = (nb, 0, seq)
        if i % 2 == 0:
            xs = _ffn(xs, mods, ff_w_gate_up[j].astype(bf), ff_w_down[j].astype(bf), geom=geom, tm=tm, tf=1408)
        else:
            rpad = jnp.pad(moe_router[j], ((0, 0), (0, LANES - N_EXPERTS)))
            xs = _moe(xs, mods, rpad, moe_w_gate_up[j].astype(bf), moe_w_down[j].astype(bf),
                      geom=geom, tm=tm, tf=896, tmc=256)
    return xs[xs.shape[0] - nb * seq:].reshape(nb, seq, D)
```

```python
import functools

import numpy as np
import jax
import jax.numpy as jnp
from jax import lax
from jax.experimental import pallas as pl
from jax.experimental.pallas import tpu as pltpu

D = 1024
EPS = 1e-6
NEG_BIG = -1e30
LOG2E = 1.4426950408889634
GRID_W = 64
ROPE_THETA = 10000.0

HG_HEADS = 8
HG_DK = 128
HG_FDIM = HG_HEADS * HG_DK
HG_CHUNK = 128
HG_LEVELS = 7
HG_ROWS = 256
HG_HB = 2

SW_HEADS = 16
SW_KV = 4
SW_GROUP = 4
SW_DH = 64
SW_BLOCK = 128

N_EXPERTS = 8
LANES = 128

VMEM_LIMIT = 56 * 1024 * 1024


def _cparams(sem):
    return pltpu.CompilerParams(dimension_semantics=sem, vmem_limit_bytes=VMEM_LIMIT)


def _sigmoid(x):
    return 1.0 / (1.0 + jnp.exp2(x * -LOG2E))


def _mod_row(i, nct, lt, nb):
    return jnp.where(i < nct, nb, (i - nct) // lt)


def _mod_spec(k, nct, lt, nb):
    return pl.BlockSpec((None, 1, D), lambda i, *_: (_mod_row(i, nct, lt, nb) * 6 + k, 0, 0))


def _modulate(x, shift, scale):
    ms = jnp.mean(x * x, axis=-1, keepdims=True)
    return (x * lax.rsqrt(ms + EPS)) * (1.0 + scale) + shift


def _modvec_kernel(c_ref, w_ref, b_ref, o_ref):
    c = c_ref[...]
    s = c * _sigmoid(c)
    o_ref[...] = jnp.dot(s, w_ref[...], precision=lax.Precision.HIGHEST,
                         preferred_element_type=jnp.float32) + b_ref[...]


def _modvecs(cpad, w_mod, b_mod):
    depth = w_mod.shape[0]
    tn = 1024
    out = pl.pallas_call(
        _modvec_kernel,
        out_shape=jax.ShapeDtypeStruct((depth, 8, 6 * D), jnp.float32),
        grid=(depth, 6 * D // tn),
        in_specs=[pl.BlockSpec((8, D), lambda l, j: (0, 0)),
                  pl.BlockSpec((None, D, tn), lambda l, j: (l, 0, j)),
                  pl.BlockSpec((None, 1, tn), lambda l, j: (l, 0, j))],
        out_specs=pl.BlockSpec((None, 8, tn), lambda l, j: (l, 0, j)),
        name="adaln_vectors",
        compiler_params=_cparams(("arbitrary", "arbitrary")),
    )(cpad, w_mod, b_mod.reshape(depth, 1, 6 * D))
    return out.reshape(depth, 8 * 6, 1, D)


def _modmm_kernel(x_ref, sh_ref, sc_ref, w_ref, o_ref, h_ref):
    @pl.when(pl.program_id(1) == 0)
    def _():
        h_ref[...] = _modulate(x_ref[...], sh_ref[...], sc_ref[...]).astype(jnp.bfloat16)

    o_ref[...] = jnp.dot(h_ref[...], w_ref[...], preferred_element_type=jnp.float32).astype(o_ref.dtype)


def _mod_matmul(x, mods, w, *, k_shift, geom, tm, tn, out_dtype):
    nb, nct_rows, seq = geom
    t, n = x.shape[0], w.shape[1]
    nct, lt = nct_rows // tm, seq // tm
    return pl.pallas_call(
        _modmm_kernel,
        out_shape=jax.ShapeDtypeStruct((t, n), out_dtype),
        grid=(t // tm, n // tn),
        in_specs=[pl.BlockSpec((tm, D), lambda i, j: (i, 0)),
                  _mod_spec(k_shift, nct, lt, nb),
                  _mod_spec(k_shift + 1, nct, lt, nb),
                  pl.BlockSpec((D, tn), lambda i, j: (0, j))],
        out_specs=pl.BlockSpec((tm, tn), lambda i, j: (i, j)),
        scratch_shapes=[pltpu.VMEM((tm, D), jnp.bfloat16)],
        name="hg_in_proj",
        compiler_params=_cparams(("arbitrary", "arbitrary")),
    )(x, mods, mods, w)


def _hg_tables(fwd):
    c = HG_CHUNK
    t = np.arange(c)[:, None]
    u = np.arange(c)[None, :]
    reach = (u <= t) if fwd else (u >= t)
    x = t ^ u
    lvl = np.where(x > 0, np.floor(np.log2(np.maximum(x, 1))), HG_LEVELS)
    lvl = np.where(reach, lvl, -1).astype(np.int32)
    tri = np.concatenate([reach.astype(np.float32)] * 3, axis=1)
    return tri, lvl


def _hg_gates(q_raw, z, lb, tri_ref, a_ref):
    bf = jnp.bfloat16
    q = q_raw * _sigmoid(q_raw)
    sig = _sigmoid(z)
    fc = jnp.maximum(lb + (1.0 - lb) * sig, 1e-30)
    lf = jnp.log(fc) * LOG2E
    k = (1.0 - lb) * (1.0 - sig)
    hi = lf.astype(bf)
    rest = lf - hi.astype(jnp.float32)
    mid = rest.astype(bf)
    lo = (rest - mid.astype(jnp.float32)).astype(bf)
    a = jnp.dot(tri_ref[...], jnp.concatenate([hi, mid, lo], axis=0), preferred_element_type=jnp.float32)
    a_ref[...] = a
    return q, k, fc, a


def _hg_scores(q, k, fc, a, a_ref, lvl, fwd):
    c = HG_CHUNK
    bf = jnp.bfloat16

    def rows_of(row, n):
        return jnp.broadcast_to(a_ref[row:row + 1, :], (n, HG_DK))

    rowi = lax.broadcasted_iota(jnp.int32, (c, HG_DK), 0)
    nt = (((1,), (1,)), ((), ()))
    zeros8 = jnp.zeros((8, HG_DK), jnp.float32)

    def level_operands(l):
        w = 1 << l
        near = w - 1 if fwd else w
        if w < 8:
            upper = ((rowi >> l) & 1) == 1
            q_side = upper if fwd else jnp.logical_not(upper)
            if l == 0:
                qe, ke = q * fc, k
            else:
                if l == 1:
                    first = jnp.concatenate([rows_of(8 * g + near, 8) for g in range(c // 8)], axis=0)
                    second = jnp.concatenate([rows_of(8 * g + 4 + near, 8) for g in range(c // 8)], axis=0)
                    bnd = jnp.where((rowi & 4) == 0, first, second)
                else:
                    bnd = jnp.concatenate([rows_of(8 * g + near, 8) for g in range(c // 8)], axis=0)
                e = jnp.exp2(-jnp.abs(a - bnd))
                qe, ke = q * e, k * e
            return jnp.where(q_side, qe, 0.0).astype(bf), jnp.where(q_side, 0.0, ke).astype(bf)
        qp, kp = [], []
        for j in range(c // w):
            rs = slice(j * w, (j + 1) * w)
            bnd = rows_of((j // 2) * 2 * w + near, w)
            zero = jnp.concatenate([zeros8] * (w // 8), axis=0)
            if (j % 2 == 1) == fwd:
                qp.append(q[rs] * jnp.exp2(a[rs] - bnd))
                kp.append(zero)
            else:
                qp.append(zero)
                kp.append(k[rs] * jnp.exp2(bnd - a[rs]))
        return jnp.concatenate(qp, axis=0).astype(bf), jnp.concatenate(kp, axis=0).astype(bf)

    scores = jnp.where(lvl == HG_LEVELS,
                       lax.dot_general(q.astype(bf), k.astype(bf), nt, preferred_element_type=jnp.float32), 0.0)
    for l in range(HG_LEVELS):
        qm, km = level_operands(l)
        sc = lax.dot_general(qm, km, nt, preferred_element_type=jnp.float32)
        scores = jnp.where(lvl == l, sc, scores)
    return scores.astype(bf)


def _hg_finish(q, k, v, a, scores, st_ref, fwd):
    c = HG_CHUNK
    bf = jnp.bfloat16
    nt = (((1,), (1,)), ((), ()))
    last = c - 1 if fwd else 0
    e_in = jnp.exp2(a)
    e_out = jnp.exp2(a[last:last + 1, :] - a)
    st = st_ref[...]
    o = (jnp.dot(scores, v.astype(bf), preferred_element_type=jnp.float32)
         + lax.dot_general((q * e_in).astype(bf), st.astype(bf), nt, preferred_element_type=jnp.float32))
    st_ref[...] = (e_in[last:last + 1, :] * st
                   + jnp.dot(v.T.astype(bf), (k * e_out).astype(bf), preferred_element_type=jnp.float32))
    return o


def _hg_scan_kernel(qf_ref, vf_ref, zf_ref, qb_ref, vb_ref, zb_ref, lbf_ref, lbb_ref, trif_ref, trib_ref,
                    lvlf_ref, lvlb_ref, of_ref, ob_ref, sf_ref, sb_ref, a_ref):
    @pl.when(pl.program_id(2) == 0)
    def _():
        sf_ref[...] = jnp.zeros_like(sf_ref)
        sb_ref[...] = jnp.zeros_like(sb_ref)

    c = HG_CHUNK
    nch = HG_ROWS // c
    lvlf, lvlb = lvlf_ref[...], lvlb_ref[...]
    units = []
    for hh in range(HG_HB):
        cs = slice(hh * HG_DK, (hh + 1) * HG_DK)
        for step, ci in enumerate(range(nch)):
            units.append(dict(step=step, rows=slice(ci * c, (ci + 1) * c), cols=cs, fwd=True, q=qf_ref, v=vf_ref,
                              z=zf_ref, lb=lbf_ref, tri=trif_ref, lvl=lvlf, st=sf_ref.at[hh], out=of_ref))
        for step, ci in enumerate(reversed(range(nch))):
            units.append(dict(step=step, rows=slice(ci * c, (ci + 1) * c), cols=cs, fwd=False, q=qb_ref, v=vb_ref,
                              z=zb_ref, lb=lbb_ref, tri=trib_ref, lvl=lvlb, st=sb_ref.at[hh], out=ob_ref))
    for n, u in enumerate(units):
        u["a_ref"] = a_ref.at[n]
        u["q"], u["k"], u["fc"], u["a"] = _hg_gates(u["q"][u["rows"], u["cols"]], u["z"][u["rows"], u["cols"]],
                                                    u["lb"][:, u["cols"]], u["tri"], u["a_ref"])
    for u in units:
        u["scores"] = _hg_scores(u["q"], u["k"], u["fc"], u["a"], u["a_ref"], u["lvl"], u["fwd"])
    for step in range(nch):
        for u in units:
            if u["step"] == step:
                o = _hg_finish(u["q"], u["k"], u["v"][u["rows"], u["cols"]], u["a"], u["scores"], u["st"], u["fwd"])
                u["out"][u["rows"], u["cols"]] = o.astype(u["out"].dtype)


def _hg_scan(p, lbf, lbb, geom):
    nb, nct_rows, seq = geom
    t = p.shape[0]
    r = HG_ROWS
    cb, lb_ = (nct_rows // nb) // r, seq // r
    nsteps = cb + lb_
    lat0 = nct_rows // r

    def fblk(b, s):
        return jnp.where(s < cb, b * cb + s, lat0 + b * lb_ + (s - cb))

    def bblk(b, s):
        return jnp.where(s < cb, b * cb + (cb - 1 - s), lat0 + b * lb_ + (lb_ - 1 - (s - cb)))

    wcol = HG_HB * HG_DK
    ngrp = HG_HEADS // HG_HB
    units = 2 * HG_HB * (r // HG_CHUNK)

    def spec(blk, colblock):
        return pl.BlockSpec((r, wcol), lambda b, h, s: (blk(b, s), colblock * ngrp + h))

    lbspec = pl.BlockSpec((1, wcol), lambda b, h, s: (0, h))
    const = lambda shape: pl.BlockSpec(shape, lambda b, h, s: (0, 0))
    trif, lvlf = _hg_tables(True)
    trib, lvlb = _hg_tables(False)
    return pl.pallas_call(
        _hg_scan_kernel,
        out_shape=(jax.ShapeDtypeStruct((t, D), jnp.bfloat16), jax.ShapeDtypeStruct((t, D), jnp.bfloat16)),
        grid=(nb, ngrp, nsteps),
        in_specs=[spec(fblk, 0), spec(fblk, 1), spec(fblk, 2),
                  spec(bblk, 0), spec(bblk, 1), spec(bblk, 3),
                  lbspec, lbspec, const(trif.shape), const(trib.shape), const(lvlf.shape), const(lvlb.shape)],
        out_specs=(pl.BlockSpec((r, wcol), lambda b, h, s: (fblk(b, s), h)),
                   pl.BlockSpec((r, wcol), lambda b, h, s: (bblk(b, s), h))),
        scratch_shapes=[pltpu.VMEM((HG_HB, HG_DK, HG_DK), jnp.float32),
                        pltpu.VMEM((HG_HB, HG_DK, HG_DK), jnp.float32),
                        pltpu.VMEM((units, HG_CHUNK, HG_DK), jnp.float32)],
        name="hg_scan",
        compiler_params=_cparams(("arbitrary", "arbitrary", "arbitrary")),
    )(p, p, p, p, p, p, lbf, lbb, jnp.asarray(trif, jnp.bfloat16), jnp.asarray(trib, jnp.bfloat16),
      jnp.asarray(lvlf), jnp.asarray(lvlb))


def _hg_out_kernel(of_ref, ob_ref, g_ref, nw_ref, w_ref, x_ref, gate_ref, o_ref, h_ref):
    @pl.when(pl.program_id(1) == 0)
    def _():
        nw = nw_ref[...]
        for h in range(HG_HEADS):
            cs = slice(h * HG_DK, (h + 1) * HG_DK)
            o = of_ref[:, cs].astype(jnp.float32) + ob_ref[:, cs].astype(jnp.float32)
            y = o * lax.rsqrt(jnp.mean(o * o, axis=-1, keepdims=True) + EPS) * nw
            g = g_ref[:, cs]
            h_ref[:, cs] = (y * (g * _sigmoid(g))).astype(jnp.bfloat16)

    acc = jnp.dot(h_ref[...], w_ref[...], preferred_element_type=jnp.float32)
    o_ref[...] = x_ref[...] + gate_ref[...] * acc


def _hg_out(of, ob, p, norm_w, w, x, mods, *, geom, tm, tn):
    nb, nct_rows, seq = geom
    t = x.shape[0]
    nct, lt = nct_rows // tm, seq // tm
    gate = pl.BlockSpec((None, 1, tn), lambda i, j: (_mod_row(i, nct, lt, nb) * 6 + 2, 0, j))
    return pl.pallas_call(
        _hg_out_kernel,
        out_shape=jax.ShapeDtypeStruct((t, D), jnp.float32),
        grid=(t // tm, D // tn),
        in_specs=[pl.BlockSpec((tm, D), lambda i, j: (i, 0)),
                  pl.BlockSpec((tm, D), lambda i, j: (i, 0)),
                  pl.BlockSpec((tm, D), lambda i, j: (i, 4)),
                  pl.BlockSpec((1, HG_DK), lambda i, j: (0, 0)),
                  pl.BlockSpec((D, tn), lambda i, j: (0, j)),
                  pl.BlockSpec((tm, tn), lambda i, j: (i, j)),
                  gate],
        out_specs=pl.BlockSpec((tm, tn), lambda i, j: (i, j)),
        scratch_shapes=[pltpu.VMEM((tm, D), jnp.bfloat16)],
        name="hg_out_proj",
        compiler_params=_cparams(("arbitrary", "arbitrary")),
    )(of, ob, p, norm_w, w, x, mods)


def _resmm_kernel(a_ref, w_ref, x_ref, gate_ref, o_ref):
    acc = jnp.dot(a_ref[...], w_ref[...], preferred_element_type=jnp.float32)
    o_ref[...] = x_ref[...] + gate_ref[...] * acc


def _res_matmul(a, w, x, mods, *, geom, tm, tn):
    nb, nct_rows, seq = geom
    rows = a.shape[0]
    t0 = (x.shape[0] - rows) // tm
    nct, lt = nct_rows // tm, seq // tm
    gate = pl.BlockSpec((None, 1, tn), lambda i, j: (_mod_row(i + t0, nct, lt, nb) * 6 + 2, 0, j))
    return pl.pallas_call(
        _resmm_kernel,
        out_shape=jax.ShapeDtypeStruct((rows, D), jnp.float32),
        grid=(rows // tm, D // tn),
        in_specs=[pl.BlockSpec((tm, a.shape[1]), lambda i, j: (i, 0)),
                  pl.BlockSpec((a.shape[1], tn), lambda i, j: (0, j)),
                  pl.BlockSpec((tm, tn), lambda i, j: (i + t0, j)),
                  gate],
        out_specs=pl.BlockSpec((tm, tn), lambda i, j: (i, j)),
        name="attn_out_proj",
        compiler_params=_cparams(("arbitrary", "arbitrary")),
    )(a, w, x, mods)


def _rope_tables(seq, tm):
    rows = seq // GRID_W
    row = np.repeat(np.arange(rows, dtype=np.float32), GRID_W)
    col = np.tile(np.arange(GRID_W, dtype=np.float32), rows)
    nf = SW_DH // 4
    inv = (ROPE_THETA ** (-np.arange(nf, dtype=np.float32) / nf)).astype(np.float32)
    ang_r = row[:, None] * inv
    ang_c = col[:, None] * inv
    cos = np.concatenate([np.cos(ang_r), np.cos(ang_r), np.cos(ang_c), np.cos(ang_c)], axis=1)
    sin = np.concatenate([-np.sin(ang_r), np.sin(ang_r), -np.sin(ang_c), np.sin(ang_c)], axis=1)
    cos = np.concatenate([np.tile(cos, (1, 2)), np.ones((tm, LANES), np.float32)], axis=0)
    sin = np.concatenate([np.tile(sin, (1, 2)), np.zeros((tm, LANES), np.float32)], axis=0)
    return jnp.asarray(cos, jnp.float32), jnp.asarray(sin, jnp.float32)


def _qkv_kernel(x_ref, sh_ref, sc_ref, w_ref, nw_ref, ones_ref, cos_ref, sin_ref, o_ref, h_ref):
    j = pl.program_id(1)

    @pl.when(j == 0)
    def _():
        h_ref[...] = _modulate(x_ref[...], sh_ref[...], sc_ref[...]).astype(jnp.bfloat16)

    acc = jnp.dot(h_ref[...], w_ref[...], preferred_element_type=jnp.float32)

    @pl.when(j == 2)
    def _():
        o_ref[...] = acc.astype(o_ref.dtype)

    @pl.when(j < 2)
    def _():
        ss = jnp.dot((acc * acc).astype(jnp.bfloat16), ones_ref[...], preferred_element_type=jnp.float32)
        y = acc * lax.rsqrt(ss * (1.0 / SW_DH) + EPS) * nw_ref[...]
        cos = cos_ref[...]
        sin = sin_ref[...]
        lane = lax.broadcasted_iota(jnp.int32, (y.shape[0], LANES), 1)
        first = (lane & 16) == 0
        scale = jnp.where(j == 0, SW_DH ** -0.5 * LOG2E, 1.0)
        for g in range(D // LANES):
            cs = slice(g * LANES, (g + 1) * LANES)
            yg = y[:, cs]
            partner = jnp.where(first, pltpu.roll(yg, LANES - 16, 1), pltpu.roll(yg, 16, 1))
            o_ref[:, cs] = ((yg * cos + partner * sin) * scale).astype(o_ref.dtype)


def _qkv_proj(x, mods, w3, nw2, *, geom, tm):
    nb, nct_rows, seq = geom
    t = x.shape[0]
    nct, lt = nct_rows // tm, seq // tm
    cos, sin = _rope_tables(seq, tm)
    blk = np.kron(np.eye(D // SW_DH, dtype=np.float32), np.ones((SW_DH, SW_DH), np.float32))
    ones = jnp.asarray(blk, jnp.bfloat16)
    tab = lambda i, j: (jnp.where(i < nct, lt, (i - nct) % lt), 0)
    return pl.pallas_call(
        _qkv_kernel,
        out_shape=jax.ShapeDtypeStruct((t, 3 * D), jnp.bfloat16),
        grid=(t // tm, 3),
        in_specs=[pl.BlockSpec((tm, D), lambda i, j: (i, 0)),
                  _mod_spec(0, nct, lt, nb),
                  _mod_spec(1, nct, lt, nb),
                  pl.BlockSpec((D, D), lambda i, j: (0, j)),
                  pl.BlockSpec((None, 1, D), lambda i, j: (jnp.minimum(j, 1), 0, 0)),
                  pl.BlockSpec((D, D), lambda i, j: (0, 0)),
                  pl.BlockSpec((tm, LANES), tab),
                  pl.BlockSpec((tm, LANES), tab)],
        out_specs=pl.BlockSpec((tm, D), lambda i, j: (i, j)),
        scratch_shapes=[pltpu.VMEM((tm, D), jnp.bfloat16)],
        name="qkv_proj",
        compiler_params=_cparams(("arbitrary", "arbitrary")),
    )(x, mods, mods, w3, nw2, ones, cos, sin)


def _attn_body(q, kcat, vcat, valid, sink_ref, kv, o_ref):
    nt = (((1,), (1,)), ((), ()))
    grp = lax.broadcasted_iota(jnp.int32, q.shape, 1) // SW_DH
    heads = range(SW_GROUP)
    scores = [lax.dot_general(jnp.where(grp == g, q, jnp.zeros_like(q)), kcat, nt,
                              preferred_element_type=jnp.float32) for g in heads]
    probs, denoms = [], []
    for g in heads:
        sink = sink_ref[kv * SW_GROUP + g] * LOG2E
        s = scores[g] if valid is None else jnp.where(valid, scores[g], NEG_BIG)
        m = jnp.maximum(jnp.max(s, axis=-1, keepdims=True), sink)
        p = jnp.exp2(s - m)
        denoms.append(jnp.sum(p, axis=-1, keepdims=True) + jnp.exp2(sink - m))
        probs.append(p.astype(jnp.bfloat16))
    out = jnp.zeros(q.shape, jnp.float32)
    for g in heads:
        og = jnp.dot(probs[g], vcat, preferred_element_type=jnp.float32) / denoms[g]
        out = jnp.where(grp == g, og, out)
    o_ref[...] = out.astype(o_ref.dtype)


def _attn_kernel(sink_ref, q_ref, kp_ref, kc_ref, kn_ref, kx_ref, vp_ref, vc_ref, vn_ref, vx_ref, o_ref,
                 *, nblk, ctx_out):
    kv, j = pl.program_id(1), pl.program_id(2)

    def latent():
        kcat = jnp.concatenate([kp_ref[...], kc_ref[...], kn_ref[...], kx_ref[...]], axis=0)
        vcat = jnp.concatenate([vp_ref[...], vc_ref[...], vn_ref[...], vx_ref[...]], axis=0)
        ns = kcat.shape[0]
        t = lax.broadcasted_iota(jnp.int32, (SW_BLOCK, ns), 0)
        s = lax.broadcasted_iota(jnp.int32, (SW_BLOCK, ns), 1)
        lo = jnp.maximum(t, jnp.where(j > 0, 0, SW_BLOCK))
        hi = jnp.minimum(t + 2 * SW_BLOCK, jnp.where(j < nblk - 1, 3 * SW_BLOCK - 1, 2 * SW_BLOCK - 1))
        valid = ((s >= lo) & (s <= hi)) | (s >= 3 * SW_BLOCK)
        _attn_body(q_ref[...], kcat, vcat, valid, sink_ref, kv, o_ref)

    if not ctx_out:
        latent()
        return
    pl.when(j < nblk)(latent)

    @pl.when(j >= nblk)
    def _():
        _attn_body(q_ref[...], kx_ref[...], vx_ref[...], None, sink_ref, kv, o_ref)


def _attention(qkv, sink, geom, ctx_out):
    nb, nct_rows, seq = geom
    ctx_len = nct_rows // nb
    nblk = seq // SW_BLOCK
    ncb = ctx_len // SW_BLOCK
    lat0 = nct_rows // SW_BLOCK
    w = SW_GROUP * SW_DH
    smem = pl.BlockSpec(memory_space=pltpu.SMEM)

    def lat(off, colblock):
        return pl.BlockSpec((SW_BLOCK, w), lambda b, kv, j: (
            lat0 + b * nblk + jnp.clip(j + off, 0, nblk - 1), colblock * SW_KV + kv))

    def ctxkv(colblock):
        return pl.BlockSpec((ctx_len, w), lambda b, kv, j: (b, colblock * SW_KV + kv))

    def qrow(b, j):
        return jnp.where(j < nblk, lat0 + b * nblk + j, b * ncb + (j - nblk))

    if ctx_out:
        steps, out_rows = nblk + ncb, qkv.shape[0]
        out_spec = pl.BlockSpec((SW_BLOCK, w), lambda b, kv, j: (qrow(b, j), kv))
    else:
        steps, out_rows = nblk, nb * seq
        out_spec = pl.BlockSpec((SW_BLOCK, w), lambda b, kv, j: (b * nblk + j, kv))
    return pl.pallas_call(
        functools.partial(_attn_kernel, nblk=nblk, ctx_out=ctx_out),
        out_shape=jax.ShapeDtypeStruct((out_rows, D), jnp.bfloat16),
        grid=(nb, SW_KV, steps),
        in_specs=[smem, pl.BlockSpec((SW_BLOCK, w), lambda b, kv, j: (qrow(b, j), kv)),
                  lat(-1, 1), lat(0, 1), lat(1, 1), ctxkv(1),
                  lat(-1, 2), lat(0, 2), lat(1, 2), ctxkv(2)],
        out_specs=out_spec,
        name="attention",
        compiler_params=_cparams(("arbitrary", "arbitrary", "arbitrary")),
    )(sink, qkv, qkv, qkv, qkv, qkv, qkv, qkv, qkv, qkv)


def _router_kernel(x_ref, sh_ref, sc_ref, r_ref, h_ref, info_ref):
    h = _modulate(x_ref[...], sh_ref[...], sc_ref[...])
    h_ref[...] = h
    logits = jnp.dot(h, r_ref[...], precision=lax.Precision.HIGHEST, preferred_element_type=jnp.float32)
    lane = lax.broadcasted_iota(jnp.int32, logits.shape, 1).astype(jnp.float32)
    logits = jnp.where(lane < N_EXPERTS, logits, -jnp.inf)
    l1 = jnp.max(logits, axis=-1, keepdims=True)
    i1 = jnp.min(jnp.where(logits == l1, lane, float(LANES)), axis=-1, keepdims=True)
    rest = jnp.where(lane == i1, -jnp.inf, logits)
    l2 = jnp.max(rest, axis=-1, keepdims=True)
    i2 = jnp.min(jnp.where(rest == l2, lane, float(LANES)), axis=-1, keepdims=True)
    w1 = 1.0 / (1.0 + jnp.exp(l2 - l1))
    info_ref[...] = jnp.where(lane == 0.0, i1, jnp.where(lane == 1.0, i2, jnp.where(lane == 2.0, w1, 1.0 - w1)))


def _router(x, mods, router_pad, *, geom, tm):
    nb, nct_rows, seq = geom
    t = x.shape[0]
    nct, lt = nct_rows // tm, seq // tm
    return pl.pallas_call(
        _router_kernel,
        out_shape=(jax.ShapeDtypeStruct((t, D), jnp.float32), jax.ShapeDtypeStruct((t, LANES), jnp.float32)),
        grid=(t // tm,),
        in_specs=[pl.BlockSpec((tm, D), lambda i: (i, 0)), _mod_spec(3, nct, lt, nb), _mod_spec(4, nct, lt, nb),
                  pl.BlockSpec((D, LANES), lambda i: (0, 0))],
        out_specs=(pl.BlockSpec((tm, D), lambda i: (i, 0)), pl.BlockSpec((tm, LANES), lambda i: (i, 0))),
        name="moe_router",
        compiler_params=_cparams(("arbitrary",)),
    )(x, mods, mods, router_pad)


def _swiglu_step(h, wg_ref, wu_ref, wd_ref, row_scale):
    g = jnp.dot(h, wg_ref[...], preferred_element_type=jnp.float32)
    u = jnp.dot(h, wu_ref[...], preferred_element_type=jnp.float32)
    a = g * _sigmoid(g) * u
    if row_scale is not None:
        a = a * row_scale
    return jnp.dot(a.astype(jnp.bfloat16), wd_ref[...], preferred_element_type=jnp.float32)


def _ffn_kernel(x_ref, sh_ref, sc_ref, gate_ref, wg_ref, wu_ref, wd_ref, o_ref, h_ref, acc_ref):
    f = pl.program_id(1)

    @pl.when(f == 0)
    def _():
        h_ref[...] = _modulate(x_ref[...], sh_ref[...], sc_ref[...]).astype(jnp.bfloat16)
        acc_ref[...] = jnp.zeros_like(acc_ref)

    acc_ref[...] += _swiglu_step(h_ref[...], wg_ref, wu_ref, wd_ref, None)

    @pl.when(f == pl.num_programs(1) - 1)
    def _():
        o_ref[...] = x_ref[...] + gate_ref[...] * acc_ref[...]


def _ffn(x, mods, wgu, wd, *, geom, tm, tf):
    nb, nct_rows, seq = geom
    t = x.shape[0]
    nf = wd.shape[0] // tf
    nct, lt = nct_rows // tm, seq // tm
    return pl.pallas_call(
        _ffn_kernel,
        out_shape=jax.ShapeDtypeStruct((t, D), jnp.float32),
        grid=(t // tm, nf),
        in_specs=[pl.BlockSpec((tm, D), lambda i, f: (i, 0)),
                  _mod_spec(3, nct, lt, nb),
                  _mod_spec(4, nct, lt, nb),
                  _mod_spec(5, nct, lt, nb),
                  pl.BlockSpec((D, tf), lambda i, f: (0, f)),
                  pl.BlockSpec((D, tf), lambda i, f: (0, nf + f)),
                  pl.BlockSpec((tf, D), lambda i, f: (f, 0))],
        out_specs=pl.BlockSpec((tm, D), lambda i, f: (i, 0)),
        scratch_shapes=[pltpu.VMEM((tm, D), jnp.bfloat16), pltpu.VMEM((tm, D), jnp.float32)],
        name="dense_ffn",
        compiler_params=_cparams(("arbitrary", "arbitrary")),
    )(x, mods, mods, mods, wgu, wgu, wd)


def _row_gather_start(src_hbm, rows_ref, buf, sem, first, count):
    for r in range(count):
        pltpu.make_async_copy(src_hbm.at[pl.ds(rows_ref[0, first + r], 1), :],
                              buf.at[pl.ds(first + r, 1), :], sem).start()


def _moe_kernel(te_ref, nt_ref, rows0_ref, rows1_ref, h_hbm, wg_ref, wu_ref, wd_ref, o_ref,
                hbuf, hb_ref, acc_ref, sem, *, tm, chunk):
    del te_ref
    i, f = pl.program_id(0), pl.program_id(1)
    last_i, last_f = pl.num_programs(0) - 1, pl.num_programs(1) - 1
    slot = i % 2
    nt = nt_ref[0]
    wait_all = lambda s: pltpu.make_async_copy(hbuf.at[s], hbuf.at[s], sem.at[s]).wait()

    @pl.when((i == 0) & (f == 0))
    def _():
        _row_gather_start(h_hbm, rows0_ref, hbuf.at[0], sem.at[0], 0, tm)

    @pl.when((f == 0) & (i <= nt))
    def _():
        wait_all(slot)
        hb_ref[...] = hbuf[slot].astype(jnp.bfloat16)
        acc_ref[...] = jnp.zeros_like(acc_ref)

    @pl.when(i < nt)
    def _():
        _row_gather_start(h_hbm, rows1_ref, hbuf.at[1 - slot], sem.at[1 - slot], f * chunk, chunk)
        acc_ref[...] += _swiglu_step(hb_ref[...], wg_ref, wu_ref, wd_ref, None)

    @pl.when(f == last_f)
    def _():
        o_ref[...] = acc_ref[...]

    @pl.when((i == last_i) & (f == last_f) & (i < nt))
    def _():
        wait_all(1 - slot)


def _moe_experts(h, tile_expert, n_tiles, rows, wgu, wd, *, tm, tf):
    nt = rows.shape[0]
    nf = wd.shape[1] // tf
    chunk = tm // nf
    assert chunk * nf == tm
    grid_spec = pltpu.PrefetchScalarGridSpec(
        num_scalar_prefetch=2,
        grid=(nt, nf),
        in_specs=[pl.BlockSpec((None, 1, tm), lambda i, f, te, n: (i, 0, 0), memory_space=pltpu.SMEM),
                  pl.BlockSpec((None, 1, tm), lambda i, f, te, n: (jnp.minimum(i + 1, nt - 1), 0, 0),
                               memory_space=pltpu.SMEM),
                  pl.BlockSpec(memory_space=pl.ANY),
                  pl.BlockSpec((None, D, tf), lambda i, f, te, n: (te[i], 0, f)),
                  pl.BlockSpec((None, D, tf), lambda i, f, te, n: (te[i], 0, nf + f)),
                  pl.BlockSpec((None, tf, D), lambda i, f, te, n: (te[i], f, 0))],
        out_specs=pl.BlockSpec((tm, D), lambda i, f, te, n: (i, 0)),
        scratch_shapes=[pltpu.VMEM((2, tm, D), jnp.float32), pltpu.VMEM((tm, D), jnp.bfloat16),
                        pltpu.VMEM((tm, D), jnp.float32), pltpu.SemaphoreType.DMA((2,))])
    return pl.pallas_call(
        functools.partial(_moe_kernel, tm=tm, chunk=chunk),
        out_shape=jax.ShapeDtypeStruct((nt * tm, D), jnp.float32),
        grid_spec=grid_spec,
        name="moe_experts",
        compiler_params=_cparams(("arbitrary", "arbitrary")),
    )(tile_expert, n_tiles, rows, rows, h, wgu, wgu, wd)


def _combine_kernel(rows0_ref, rows1_ref, ys_hbm, x_ref, gate_ref, info_ref, o_ref, buf, sem, *, tmc):
    i = pl.program_id(0)
    slot = i % 2
    n = 2 * tmc

    def start(rows_ref, s):
        def body(r, carry):
            pltpu.make_async_copy(ys_hbm.at[pl.ds(rows_ref[0, r], 1), :], buf.at[s, pl.ds(r, 1), :],
                                  sem.at[s]).start()
            return carry
        lax.fori_loop(0, n, body, 0, unroll=8)

    @pl.when(i == 0)
    def _():
        start(rows0_ref, 0)

    @pl.when(i + 1 < pl.num_programs(0))
    def _():
        start(rows1_ref, 1 - slot)

    pltpu.make_async_copy(buf.at[slot], buf.at[slot], sem.at[slot]).wait()
    info = info_ref[...]
    y = info[:, 2:3] * buf[slot, 0:tmc, :] + info[:, 3:4] * buf[slot, tmc:n, :]
    o_ref[...] = x_ref[...] + gate_ref[...] * y


def _moe_combine(x, ys, rows, info, mods, *, geom, tmc):
    nb, nct_rows, seq = geom
    t = x.shape[0]
    nct, lt = nct_rows // tmc, seq // tmc
    ntile = rows.shape[0]
    return pl.pallas_call(
        functools.partial(_combine_kernel, tmc=tmc),
        out_shape=jax.ShapeDtypeStruct((t, D), jnp.float32),
        grid=(ntile,),
        in_specs=[pl.BlockSpec((None, 1, 2 * tmc), lambda i: (i, 0, 0), memory_space=pltpu.SMEM),
                  pl.BlockSpec((None, 1, 2 * tmc), lambda i: (jnp.minimum(i + 1, ntile - 1), 0, 0),
                               memory_space=pltpu.SMEM),
                  pl.BlockSpec(memory_space=pl.ANY),
                  pl.BlockSpec((tmc, D), lambda i: (i, 0)),
                  _mod_spec(5, nct, lt, nb),
                  pl.BlockSpec((tmc, LANES), lambda i: (i, 0))],
        out_specs=pl.BlockSpec((tmc, D), lambda i: (i, 0)),
        scratch_shapes=[pltpu.VMEM((2, 2 * tmc, D), jnp.float32), pltpu.SemaphoreType.DMA((2,))],
        name="moe_combine",
        compiler_params=_cparams(("arbitrary",)),
    )(rows, rows, ys, x, mods, info)


def _moe(x, mods, router_pad, wgu, wd, *, geom, tm, tf, tmc):
    tr = x.shape[0]
    h, info = _router(x, mods, router_pad, geom=geom, tm=tm)
    e = info[:, 0:2].astype(jnp.int32).reshape(-1)
    onehot = (e[:, None] == jnp.arange(N_EXPERTS, dtype=jnp.int32)[None, :]).astype(jnp.int32)
    csum = jnp.cumsum(onehot, axis=0)
    rank = jnp.sum(onehot * (csum - 1), axis=1)
    counts = csum[-1]
    padded = ((counts + tm - 1) // tm) * tm
    ends = jnp.cumsum(padded)
    dest = (ends - padded)[e] + rank
    nt = 2 * tr // tm + N_EXPERTS
    tile_expert = jnp.minimum(jnp.sum(jnp.arange(nt, dtype=jnp.int32)[:, None] * tm >= ends[None, :], axis=1),
                              N_EXPERTS - 1).astype(jnp.int32)
    n_tiles = (ends[-1:] // tm).astype(jnp.int32)
    token_row = jnp.arange(2 * tr, dtype=jnp.int32) // 2
    src = jnp.zeros((nt * tm,), jnp.int32).at[dest].set(token_row)
    ys = _moe_experts(h, tile_expert, n_tiles, src.reshape(nt, 1, tm), wgu, wd, tm=tm, tf=tf)
    crow = dest.reshape(tr // tmc, tmc, 2).transpose(0, 2, 1).reshape(tr // tmc, 1, 2 * tmc)
    return _moe_combine(x, ys, crow, info, mods, geom=geom, tmc=tmc)


def kernel(x, c, ctx, c_ctx, w_mod, b_mod, hg_w_in, hg_lb_logits, hg_norm_w, hg_w_out, sw_w_qkv, sw_q_norm,
           sw_k_norm, sw_sink, sw_w_out, ff_w_gate_up, ff_w_down, moe_router, moe_w_gate_up, moe_w_down):
    nb, seq, _ = x.shape
    ctx_len = ctx.shape[1]
    depth = w_mod.shape[0]
    nct_rows = nb * ctx_len
    geom = (nb, nct_rows, seq)
    tm = 512
    tmb = 1024 if nct_rows % 1024 == 0 and seq % 1024 == 0 else tm
    bf = jnp.bfloat16

    xs = jnp.concatenate([ctx.reshape(nct_rows, D), x.reshape(nb * seq, D)], axis=0)
    cpad = jnp.concatenate([c, c_ctx[None, :], jnp.zeros((8 - nb - 1, D), jnp.float32)], axis=0)
    mods_all = _modvecs(cpad, w_mod, b_mod)

    p_lb = jax.nn.softmax(hg_lb_logits.astype(jnp.float32), axis=0)
    lower_bounds = jnp.cumsum(p_lb, axis=0) - p_lb[:1]

    for i in range(depth):
        ctx_live = i < depth - 1
        mods = mods_all[i]
        j = i // 2
        if i % 2 == 0:
            p = _mod_matmul(xs, mods, hg_w_in[j].astype(bf), k_shift=0, geom=geom, tm=tmb, tn=1024,
                            out_dtype=jnp.float32)
            of, ob = _hg_scan(p, lower_bounds[j, 0:1], lower_bounds[j, 1:2], geom)
            xs = _hg_out(of, ob, p, hg_norm_w[j][None, :], hg_w_out[j].astype(bf), xs, mods,
                         geom=geom, tm=tmb, tn=1024)
        else:
            wq, wk, wv = jnp.split(sw_w_qkv[j], [D, D + SW_KV * SW_DH], axis=1)
            rep = lambda w: jnp.repeat(w.reshape(D, SW_KV, 1, SW_DH), SW_GROUP, axis=2).reshape(D, D)
            w3 = jnp.concatenate([wq, rep(wk), rep(wv)], axis=1).astype(bf)
            nw2 = jnp.stack([jnp.tile(sw_q_norm[j], SW_HEADS), jnp.tile(sw_k_norm[j], SW_HEADS)])[:, None, :]
            qkv = _qkv_proj(xs, mods, w3, nw2, geom=geom, tm=tmb)
            o = _attention(qkv, sw_sink[j], geom, ctx_live)
            xs = _res_matmul(o, sw_w_out[j].astype(bf), xs, mods, geom=geom, tm=tmb, tn=1024)
            if not ctx_live:
                geom = (nb, 0, seq)
        if i % 2 == 0:
            xs = _ffn(xs, mods, ff_w_gate_up[j].astype(bf), ff_w_down[j].astype(bf), geom=geom, tm=tm, tf=1408)
        else:
            rpad = jnp.pad(moe_router[j], ((0, 0), (0, LANES - N_EXPERTS)))
            xs = _moe(xs, mods, rpad, moe_w_gate_up[j].astype(bf), moe_w_down[j].astype(bf),
                      geom=geom, tm=tm, tf=896, tmc=256)
    return xs[xs.shape[0] - nb * seq:].reshape(nb, seq, D)
```

```python
import functools

import numpy as np
import jax
import jax.numpy as jnp
from jax import lax
from jax.experimental import pallas as pl
from jax.experimental.pallas import tpu as pltpu

D = 1024
EPS = 1e-6
NEG_BIG = -1e30
LOG2E = 1.4426950408889634
GRID_W = 64
ROPE_THETA = 10000.0

HG_HEADS = 8
HG_DK = 128
HG_FDIM = HG_HEADS * HG_DK
HG_CHUNK = 128
HG_LEVELS = 7
HG_ROWS = 256
HG_HB = 2

SW_HEADS = 16
SW_KV = 4
SW_GROUP = 4
SW_DH = 64
SW_BLOCK = 128

N_EXPERTS = 8
LANES = 128

VMEM_LIMIT = 56 * 1024 * 1024


def _cparams(sem):
    return pltpu.CompilerParams(dimension_semantics=sem, vmem_limit_bytes=VMEM_LIMIT)


def _sigmoid(x):
    return 1.0 / (1.0 + jnp.exp2(x * -LOG2E))


def _mod_row(i, nct, lt, nb):
    return jnp.where(i < nct, nb, (i - nct) // lt)


def _mod_spec(k, nct, lt, nb):
    return pl.BlockSpec((None, 1, D), lambda i, *_: (_mod_row(i, nct, lt, nb) * 6 + k, 0, 0))


def _modulate(x, shift, scale):
    ms = jnp.mean(x * x, axis=-1, keepdims=True)
    return (x * lax.rsqrt(ms + EPS)) * (1.0 + scale) + shift


def _modvec_kernel(c_ref, w_ref, b_ref, o_ref):
    c = c_ref[...]
    s = c * _sigmoid(c)
    o_ref[...] = jnp.dot(s, w_ref[...], precision=lax.Precision.HIGHEST,
                         preferred_element_type=jnp.float32) + b_ref[...]


def _modvecs(cpad, w_mod, b_mod):
    depth = w_mod.shape[0]
    tn = 1024
    out = pl.pallas_call(
        _modvec_kernel,
        out_shape=jax.ShapeDtypeStruct((depth, 8, 6 * D), jnp.float32),
        grid=(depth, 6 * D // tn),
        in_specs=[pl.BlockSpec((8, D), lambda l, j: (0, 0)),
                  pl.BlockSpec((None, D, tn), lambda l, j: (l, 0, j)),
                  pl.BlockSpec((None, 1, tn), lambda l, j: (l, 0, j))],
        out_specs=pl.BlockSpec((None, 8, tn), lambda l, j: (l, 0, j)),
        name="adaln_vectors",
        compiler_params=_cparams(("arbitrary", "arbitrary")),
    )(cpad, w_mod, b_mod.reshape(depth, 1, 6 * D))
    return out.reshape(depth, 8 * 6, 1, D)


def _modmm_kernel(x_ref, sh_ref, sc_ref, w_ref, o_ref, h_ref):
    @pl.when(pl.program_id(1) == 0)
    def _():
        h_ref[...] = _modulate(x_ref[...], sh_ref[...], sc_ref[...]).astype(jnp.bfloat16)

    o_ref[...] = jnp.dot(h_ref[...], w_ref[...], preferred_element_type=jnp.float32).astype(o_ref.dtype)


def _mod_matmul(x, mods, w, *, k_shift, geom, tm, tn, out_dtype):
    nb, nct_rows, seq = geom
    t, n = x.shape[0], w.shape[1]
    nct, lt = nct_rows // tm, seq // tm
    return pl.pallas_call(
        _modmm_kernel,
        out_shape=jax.ShapeDtypeStruct((t, n), out_dtype),
        grid=(t // tm, n // tn),
        in_specs=[pl.BlockSpec((tm, D), lambda i, j: (i, 0)),
                  _mod_spec(k_shift, nct, lt, nb),
                  _mod_spec(k_shift + 1, nct, lt, nb),
                  pl.BlockSpec((D, tn), lambda i, j: (0, j))],
        out_specs=pl.BlockSpec((tm, tn), lambda i, j: (i, j)),
        scratch_shapes=[pltpu.VMEM((tm, D), jnp.bfloat16)],
        name="hg_in_proj",
        compiler_params=_cparams(("arbitrary", "arbitrary")),
    )(x, mods, mods, w)


def _hg_tables(fwd):
    c = HG_CHUNK
    t = np.arange(c)[:, None]
    u = np.arange(c)[None, :]
    reach = (u <= t) if fwd else (u >= t)
    x = t ^ u
    lvl = np.where(x > 0, np.floor(np.log2(np.maximum(x, 1))), HG_LEVELS)
    lvl = np.where(reach, lvl, -1).astype(np.int32)
    tri = np.concatenate([reach.astype(np.float32)] * 3, axis=1)
    return tri, lvl


def _hg_gates(q_raw, z, lb, tri_ref, a_ref):
    bf = jnp.bfloat16
    q = q_raw * _sigmoid(q_raw)
    sig = _sigmoid(z)
    fc = jnp.maximum(lb + (1.0 - lb) * sig, 1e-30)
    lf = jnp.log(fc) * LOG2E
    k = (1.0 - lb) * (1.0 - sig)
    hi = lf.astype(bf)
    rest = lf - hi.astype(jnp.float32)
    mid = rest.astype(bf)
    lo = (rest - mid.astype(jnp.float32)).astype(bf)
    a = jnp.dot(tri_ref[...], jnp.concatenate([hi, mid, lo], axis=0), preferred_element_type=jnp.float32)
    a_ref[...] = a
    return q, k, fc, a


def _hg_scores(q, k, fc, a, a_ref, lvl, fwd):
    c = HG_CHUNK
    bf = jnp.bfloat16

    def rows_of(row, n):
        return jnp.broadcast_to(a_ref[row:row + 1, :], (n, HG_DK))

    rowi = lax.broadcasted_iota(jnp.int32, (c, HG_DK), 0)
    nt = (((1,), (1,)), ((), ()))
    zeros8 = jnp.zeros((8, HG_DK), jnp.float32)

    def level_operands(l):
        w = 1 << l
        near = w - 1 if fwd else w
        if w < 8:
            upper = ((rowi >> l) & 1) == 1
            q_side = upper if fwd else jnp.logical_not(upper)
            if l == 0:
                qe, ke = q * fc, k
            else:
                if l == 1:
                    first = jnp.concatenate([rows_of(8 * g + near, 8) for g in range(c // 8)], axis=0)
                    second = jnp.concatenate([rows_of(8 * g + 4 + near, 8) for g in range(c // 8)], axis=0)
                    bnd = jnp.where((rowi & 4) == 0, first, second)
                else:
                    bnd = jnp.concatenate([rows_of(8 * g + near, 8) for g in range(c // 8)], axis=0)
                e = jnp.exp2(-jnp.abs(a - bnd))
                qe, ke = q * e, k * e
            return jnp.where(q_side, qe, 0.0).astype(bf), jnp.where(q_side, 0.0, ke).astype(bf)
        qp, kp = [], []
        for j in range(c // w):
            rs = slice(j * w, (j + 1) * w)
            bnd = rows_of((j // 2) * 2 * w + near, w)
            zero = jnp.concatenate([zeros8] * (w // 8), axis=0)
            if (j % 2 == 1) == fwd:
                qp.append(q[rs] * jnp.exp2(a[rs] - bnd))
                kp.append(zero)
            else:
                qp.append(zero)
                kp.append(k[rs] * jnp.exp2(bnd - a[rs]))
        return jnp.concatenate(qp, axis=0).astype(bf), jnp.concatenate(kp, axis=0).astype(bf)

    scores = jnp.where(lvl == HG_LEVELS,
                       lax.dot_general(q.astype(bf), k.astype(bf), nt, preferred_element_type=jnp.float32), 0.0)
    for l in range(HG_LEVELS):
        qm, km = level_operands(l)
        sc = lax.dot_general(qm, km, nt, preferred_element_type=jnp.float32)
        scores = jnp.where(lvl == l, sc, scores)
    return scores.astype(bf)


def _hg_finish(q, k, v, a, scores, st_ref, fwd):
    c = HG_CHUNK
    bf = jnp.bfloat16
    nt = (((1,), (1,)), ((), ()))
    last = c - 1 if fwd else 0
    e_in = jnp.exp2(a)
    e_out = jnp.exp2(a[last:last + 1, :] - a)
    st = st_ref[...]
    o = (jnp.dot(scores, v.astype(bf), preferred_element_type=jnp.float32)
         + lax.dot_general((q * e_in).astype(bf), st.astype(bf), nt, preferred_element_type=jnp.float32))
    st_ref[...] = (e_in[last:last + 1, :] * st
                   + jnp.dot(v.T.astype(bf), (k * e_out).astype(bf), preferred_element_type=jnp.float32))
    return o


def _hg_scan_kernel(qf_ref, vf_ref, zf_ref, qb_ref, vb_ref, zb_ref, lbf_ref, lbb_ref, trif_ref, trib_ref,
                    lvlf_ref, lvlb_ref, of_ref, ob_ref, sf_ref, sb_ref, a_ref):
    @pl.when(pl.program_id(2) == 0)
    def _():
        sf_ref[...] = jnp.zeros_like(sf_ref)
        sb_ref[...] = jnp.zeros_like(sb_ref)

    c = HG_CHUNK
    nch = HG_ROWS // c
    lvlf, lvlb = lvlf_ref[...], lvlb_ref[...]
    units = []
    for hh in range(HG_HB):
        cs = slice(hh * HG_DK, (hh + 1) * HG_DK)
        for step, ci in enumerate(range(nch)):
            units.append(dict(step=step, rows=slice(ci * c, (ci + 1) * c), cols=cs, fwd=True, q=qf_ref, v=vf_ref,
                              z=zf_ref, lb=lbf_ref, tri=trif_ref, lvl=lvlf, st=sf_ref.at[hh], out=of_ref))
        for step, ci in enumerate(reversed(range(nch))):
            units.append(dict(step=step, rows=slice(ci * c, (ci + 1) * c), cols=cs, fwd=False, q=qb_ref, v=vb_ref,
                              z=zb_ref, lb=lbb_ref, tri=trib_ref, lvl=lvlb, st=sb_ref.at[hh], out=ob_ref))
    for n, u in enumerate(units):
        u["a_ref"] = a_ref.at[n]
        u["q"], u["k"], u["fc"], u["a"] = _hg_gates(u["q"][u["rows"], u["cols"]], u["z"][u["rows"], u["cols"]],
                                                    u["lb"][:, u["cols"]], u["tri"], u["a_ref"])
    for u in units:
        u["scores"] = _hg_scores(u["q"], u["k"], u["fc"], u["a"], u["a_ref"], u["lvl"], u["fwd"])
    for step in range(nch):
        for u in units:
            if u["step"] == step:
                o = _hg_finish(u["q"], u["k"], u["v"][u["rows"], u["cols"]], u["a"], u["scores"], u["st"], u["fwd"])
                u["out"][u["rows"], u["cols"]] = o.astype(u["out"].dtype)


def _hg_scan(p, lbf, lbb, geom):
    nb, nct_rows, seq = geom
    t = p.shape[0]
    r = HG_ROWS
    cb, lb_ = (nct_rows // nb) // r, seq // r
    nsteps = cb + lb_
    lat0 = nct_rows // r

    def fblk(b, s):
        return jnp.where(s < cb, b * cb + s, lat0 + b * lb_ + (s - cb))

    def bblk(b, s):
        return jnp.where(s < cb, b * cb + (cb - 1 - s), lat0 + b * lb_ + (lb_ - 1 - (s - cb)))

    wcol = HG_HB * HG_DK
    ngrp = HG_HEADS // HG_HB
    units = 2 * HG_HB * (r // HG_CHUNK)

    def spec(blk, colblock):
        return pl.BlockSpec((r, wcol), lambda b, h, s: (blk(b, s), colblock * ngrp + h))

    lbspec = pl.BlockSpec((1, wcol), lambda b, h, s: (0, h))
    const = lambda shape: pl.BlockSpec(shape, lambda b, h, s: (0, 0))
    trif, lvlf = _hg_tables(True)
    trib, lvlb = _hg_tables(False)
    return pl.pallas_call(
        _hg_scan_kernel,
        out_shape=(jax.ShapeDtypeStruct((t, D), jnp.bfloat16), jax.ShapeDtypeStruct((t, D), jnp.bfloat16)),
        grid=(nb, ngrp, nsteps),
        in_specs=[spec(fblk, 0), spec(fblk, 1), spec(fblk, 2),
                  spec(bblk, 0), spec(bblk, 1), spec(bblk, 3),
                  lbspec, lbspec, const(trif.shape), const(trib.shape), const(lvlf.shape), const(lvlb.shape)],
        out_specs=(pl.BlockSpec((r, wcol), lambda b, h, s: (fblk(b, s), h)),
                   pl.BlockSpec((r, wcol), lambda b, h, s: (bblk(b, s), h))),
        scratch_shapes=[pltpu.VMEM((HG_HB, HG_DK, HG_DK), jnp.float32),
                        pltpu.VMEM((HG_HB, HG_DK, HG_DK), jnp.float32),
                        pltpu.VMEM((units, HG_CHUNK, HG_DK), jnp.float32)],
        name="hg_scan",
        compiler_params=_cparams(("arbitrary", "arbitrary", "arbitrary")),
    )(p, p, p, p, p, p, lbf, lbb, jnp.asarray(trif, jnp.bfloat16), jnp.asarray(trib, jnp.bfloat16),
      jnp.asarray(lvlf), jnp.asarray(lvlb))


def _hg_out_kernel(of_ref, ob_ref, g_ref, nw_ref, w_ref, x_ref, gate_ref, o_ref, h_ref):
    @pl.when(pl.program_id(1) == 0)
    def _():
        nw = nw_ref[...]
        for h in range(HG_HEADS):
            cs = slice(h * HG_DK, (h + 1) * HG_DK)
            o = of_ref[:, cs].astype(jnp.float32) + ob_ref[:, cs].astype(jnp.float32)
            y = o * lax.rsqrt(jnp.mean(o * o, axis=-1, keepdims=True) + EPS) * nw
            g = g_ref[:, cs]
            h_ref[:, cs] = (y * (g * _sigmoid(g))).astype(jnp.bfloat16)

    acc = jnp.dot(h_ref[...], w_ref[...], preferred_element_type=jnp.float32)
    o_ref[...] = x_ref[...] + gate_ref[...] * acc


def _hg_out(of, ob, p, norm_w, w, x, mods, *, geom, tm, tn):
    nb, nct_rows, seq = geom
    t = x.shape[0]
    nct, lt = nct_rows // tm, seq // tm
    gate = pl.BlockSpec((None, 1, tn), lambda i, j: (_mod_row(i, nct, lt, nb) * 6 + 2, 0, j))
    return pl.pallas_call(
        _hg_out_kernel,
        out_shape=jax.ShapeDtypeStruct((t, D), jnp.float32),
        grid=(t // tm, D // tn),
        in_specs=[pl.BlockSpec((tm, D), lambda i, j: (i, 0)),
                  pl.BlockSpec((tm, D), lambda i, j: (i, 0)),
                  pl.BlockSpec((tm, D), lambda i, j: (i, 4)),
                  pl.BlockSpec((1, HG_DK), lambda i, j: (0, 0)),
                  pl.BlockSpec((D, tn), lambda i, j: (0, j)),
                  pl.BlockSpec((tm, tn), lambda i, j: (i, j)),
                  gate],
        out_specs=pl.BlockSpec((tm, tn), lambda i, j: (i, j)),
        scratch_shapes=[pltpu.VMEM((tm, D), jnp.bfloat16)],
        name="hg_out_proj",
        compiler_params=_cparams(("arbitrary", "arbitrary")),
    )(of, ob, p, norm_w, w, x, mods)


def _resmm_kernel(a_ref, w_ref, x_ref, gate_ref, o_ref):
    acc = jnp.dot(a_ref[...], w_ref[...], preferred_element_type=jnp.float32)
    o_ref[...] = x_ref[...] + gate_ref[...] * acc


def _res_matmul(a, w, x, mods, *, geom, tm, tn):
    nb, nct_rows, seq = geom
    rows = a.shape[0]
    t0 = (x.shape[0] - rows) // tm
    nct, lt = nct_rows // tm, seq // tm
    gate = pl.BlockSpec((None, 1, tn), lambda i, j: (_mod_row(i + t0, nct, lt, nb) * 6 + 2, 0, j))
    return pl.pallas_call(
        _resmm_kernel,
        out_shape=jax.ShapeDtypeStruct((rows, D), jnp.float32),
        grid=(rows // tm, D // tn),
        in_specs=[pl.BlockSpec((tm, a.shape[1]), lambda i, j: (i, 0)),
                  pl.BlockSpec((a.shape[1], tn), lambda i, j: (0, j)),
                  pl.BlockSpec((tm, tn), lambda i, j: (i + t0, j)),
                  gate],
        out_specs=pl.BlockSpec((tm, tn), lambda i, j: (i, j)),
        name="attn_out_proj",
        compiler_params=_cparams(("arbitrary", "arbitrary")),
    )(a, w, x, mods)


def _rope_tables(seq, tm):
    rows = seq // GRID_W
    row = np.repeat(np.arange(rows, dtype=np.float32), GRID_W)
    col = np.tile(np.arange(GRID_W, dtype=np.float32), rows)
    nf = SW_DH // 4
    inv = (ROPE_THETA ** (-np.arange(nf, dtype=np.float32) / nf)).astype(np.float32)
    ang_r = row[:, None] * inv
    ang_c = col[:, None] * inv
    cos = np.concatenate([np.cos(ang_r), np.cos(ang_r), np.cos(ang_c), np.cos(ang_c)], axis=1)
    sin = np.concatenate([-np.sin(ang_r), np.sin(ang_r), -np.sin(ang_c), np.sin(ang_c)], axis=1)
    cos = np.concatenate([np.tile(cos, (1, 2)), np.ones((tm, LANES), np.float32)], axis=0)
    sin = np.concatenate([np.tile(sin, (1, 2)), np.zeros((tm, LANES), np.float32)], axis=0)
    return jnp.asarray(cos, jnp.float32), jnp.asarray(sin, jnp.float32)


def _qkv_kernel(x_ref, sh_ref, sc_ref, w_ref, nw_ref, ones_ref, cos_ref, sin_ref, o_ref, h_ref):
    j = pl.program_id(1)

    @pl.when(j == 0)
    def _():
        h_ref[...] = _modulate(x_ref[...], sh_ref[...], sc_ref[...]).astype(jnp.bfloat16)

    acc = jnp.dot(h_ref[...], w_ref[...], preferred_element_type=jnp.float32)

    @pl.when(j == 2)
    def _():
        o_ref[...] = acc.astype(o_ref.dtype)

    @pl.when(j < 2)
    def _():
        ss = jnp.dot((acc * acc).astype(jnp.bfloat16), ones_ref[...], preferred_element_type=jnp.float32)
        y = acc * lax.rsqrt(ss * (1.0 / SW_DH) + EPS) * nw_ref[...]
        cos = cos_ref[...]
        sin = sin_ref[...]
        lane = lax.broadcasted_iota(jnp.int32, (y.shape[0], LANES), 1)
        first = (lane & 16) == 0
        scale = jnp.where(j == 0, SW_DH ** -0.5 * LOG2E, 1.0)
        for g in range(D // LANES):
            cs = slice(g * LANES, (g + 1) * LANES)
            yg = y[:, cs]
            partner = jnp.where(first, pltpu.roll(yg, LANES - 16, 1), pltpu.roll(yg, 16, 1))
            o_ref[:, cs] = ((yg * cos + partner * sin) * scale).astype(o_ref.dtype)


def _qkv_proj(x, mods, w3, nw2, *, geom, tm):
    nb, nct_rows, seq = geom
    t = x.shape[0]
    nct, lt = nct_rows // tm, seq // tm
    cos, sin = _rope_tables(seq, tm)
    blk = np.kron(np.eye(D // SW_DH, dtype=np.float32), np.ones((SW_DH, SW_DH), np.float32))
    ones = jnp.asarray(blk, jnp.bfloat16)
    tab = lambda i, j: (jnp.where(i < nct, lt, (i - nct) % lt), 0)
    return pl.pallas_call(
        _qkv_kernel,
        out_shape=jax.ShapeDtypeStruct((t, 3 * D), jnp.bfloat16),
        grid=(t // tm, 3),
        in_specs=[pl.BlockSpec((tm, D), lambda i, j: (i, 0)),
                  _mod_spec(0, nct, lt, nb),
                  _mod_spec(1, nct, lt, nb),
                  pl.BlockSpec((D, D), lambda i, j: (0, j)),
                  pl.BlockSpec((None, 1, D), lambda i, j: (jnp.minimum(j, 1), 0, 0)),
                  pl.BlockSpec((D, D), lambda i, j: (0, 0)),
                  pl.BlockSpec((tm, LANES), tab),
                  pl.BlockSpec((tm, LANES), tab)],
        out_specs=pl.BlockSpec((tm, D), lambda i, j: (i, j)),
        scratch_shapes=[pltpu.VMEM((tm, D), jnp.bfloat16)],
        name="qkv_proj",
        compiler_params=_cparams(("arbitrary", "arbitrary")),
    )(x, mods, mods, w3, nw2, ones, cos, sin)


def _attn_items(items, sink_ref, kv, o_ref):
    nt = (((1,), (1,)), ((), ()))
    work = [(it, g) for it in items for g in range(SW_GROUP)]
    scores = []
    for (_, q, kcat, _, _), g in work:
        grp = lax.broadcasted_iota(jnp.int32, q.shape, 1) // SW_DH
        scores.append(lax.dot_general(jnp.where(grp == g, q, jnp.zeros_like(q)), kcat, nt,
                                      preferred_element_type=jnp.float32))
    probs, denoms = [], []
    for ((_, _, _, _, valid), g), s in zip(work, scores):
        sink = sink_ref[kv * SW_GROUP + g] * LOG2E
        if valid is not None:
            s = jnp.where(valid, s, NEG_BIG)
        m = jnp.maximum(jnp.max(s, axis=-1, keepdims=True), sink)
        p = jnp.exp2(s - m)
        denoms.append(jnp.sum(p, axis=-1, keepdims=True) + jnp.exp2(sink - m))
        probs.append(p.astype(jnp.bfloat16))
    outs = {}
    for n, ((rows, q, _, vcat, _), g) in enumerate(work):
        grp = lax.broadcasted_iota(jnp.int32, q.shape, 1) // SW_DH
        og = jnp.dot(probs[n], vcat, preferred_element_type=jnp.float32) / denoms[n]
        outs[rows] = og if g == 0 else jnp.where(grp == g, og, outs[rows])
    for rows, out in outs.items():
        o_ref[rows[0]:rows[1], :] = out.astype(o_ref.dtype)


def _attn_kernel(sink_ref, q_ref, kp_ref, kc_ref, kn_ref, kx_ref, vp_ref, vc_ref, vn_ref, vx_ref, o_ref,
                 *, nstep, ctx_out):
    kv, j = pl.program_id(1), pl.program_id(2)
    blk = SW_BLOCK

    def latent():
        kx, vx = kx_ref[...], vx_ref[...]
        k0, k1, v0, v1 = kc_ref[0:blk, :], kc_ref[blk:2 * blk, :], vc_ref[0:blk, :], vc_ref[blk:2 * blk, :]
        ns = 3 * blk + kx.shape[0]
        t = lax.broadcasted_iota(jnp.int32, (blk, ns), 0)
        s = lax.broadcasted_iota(jnp.int32, (blk, ns), 1)

        def window(has_prev, has_next):
            lo = jnp.maximum(t, jnp.where(has_prev, 0, blk))
            hi = jnp.minimum(t + 2 * blk, jnp.where(has_next, 3 * blk - 1, 2 * blk - 1))
            return ((s >= lo) & (s <= hi)) | (s >= 3 * blk)

        items = [((0, blk), q_ref[0:blk, :], jnp.concatenate([kp_ref[...], k0, k1, kx], axis=0),
                  jnp.concatenate([vp_ref[...], v0, v1, vx], axis=0), window(j > 0, True)),
                 ((blk, 2 * blk), q_ref[blk:2 * blk, :], jnp.concatenate([k0, k1, kn_ref[...], kx], axis=0),
                  jnp.concatenate([v0, v1, vn_ref[...], vx], axis=0), window(True, j < nstep - 1))]
        _attn_items(items, sink_ref, kv, o_ref)

    if not ctx_out:
        latent()
        return
    pl.when(j < nstep)(latent)

    @pl.when(j >= nstep)
    def _():
        _attn_items([((0, 2 * blk), q_ref[...], kx_ref[...], vx_ref[...], None)], sink_ref, kv, o_ref)


def _attention(qkv, sink, geom, ctx_out):
    nb, nct_rows, seq = geom
    ctx_len = nct_rows // nb
    pair = 2 * SW_BLOCK
    assert ctx_len == pair and seq % pair == 0
    nblk = seq // SW_BLOCK
    nstep = seq // pair
    lat0 = nct_rows // SW_BLOCK
    w = SW_GROUP * SW_DH
    smem = pl.BlockSpec(memory_space=pltpu.SMEM)

    def qrow(b, j):
        return jnp.where(j < nstep, nct_rows // pair + b * nstep + j, b)

    def edge(off, colblock):
        return pl.BlockSpec((SW_BLOCK, w), lambda b, kv, j: (
            lat0 + b * nblk + jnp.clip(2 * jnp.minimum(j, nstep - 1) + off, 0, nblk - 1), colblock * SW_KV + kv))

    def center(colblock):
        return pl.BlockSpec((pair, w), lambda b, kv, j: (qrow(b, jnp.minimum(j, nstep - 1)), colblock * SW_KV + kv))

    def ctxkv(colblock):
        return pl.BlockSpec((ctx_len, w), lambda b, kv, j: (b, colblock * SW_KV + kv))

    if ctx_out:
        steps, out_rows = nstep + 1, qkv.shape[0]
        out_spec = pl.BlockSpec((pair, w), lambda b, kv, j: (qrow(b, j), kv))
    else:
        steps, out_rows = nstep, nb * seq
        out_spec = pl.BlockSpec((pair, w), lambda b, kv, j: (b * nstep + j, kv))
    return pl.pallas_call(
        functools.partial(_attn_kernel, nstep=nstep, ctx_out=ctx_out),
        out_shape=jax.ShapeDtypeStruct((out_rows, D), jnp.bfloat16),
        grid=(nb, SW_KV, steps),
        in_specs=[smem, pl.BlockSpec((pair, w), lambda b, kv, j: (qrow(b, j), kv)),
                  edge(-1, 1), center(1), edge(2, 1), ctxkv(1),
                  edge(-1, 2), center(2), edge(2, 2), ctxkv(2)],
        out_specs=out_spec,
        name="attention",
        compiler_params=_cparams(("arbitrary", "arbitrary", "arbitrary")),
    )(sink, qkv, qkv, qkv, qkv, qkv, qkv, qkv, qkv, qkv)


def _router_kernel(x_ref, sh_ref, sc_ref, r_ref, h_ref, info_ref):
    h = _modulate(x_ref[...], sh_ref[...], sc_ref[...])
    h_ref[...] = h
    logits = jnp.dot(h, r_ref[...], precision=lax.Precision.HIGHEST, preferred_element_type=jnp.float32)
    lane = lax.broadcasted_iota(jnp.int32, logits.shape, 1).astype(jnp.float32)
    logits = jnp.where(lane < N_EXPERTS, logits, -jnp.inf)
    l1 = jnp.max(logits, axis=-1, keepdims=True)
    i1 = jnp.min(jnp.where(logits == l1, lane, float(LANES)), axis=-1, keepdims=True)
    rest = jnp.where(lane == i1, -jnp.inf, logits)
    l2 = jnp.max(rest, axis=-1, keepdims=True)
    i2 = jnp.min(jnp.where(rest == l2, lane, float(LANES)), axis=-1, keepdims=True)
    w1 = 1.0 / (1.0 + jnp.exp(l2 - l1))
    info_ref[...] = jnp.where(lane == 0.0, i1, jnp.where(lane == 1.0, i2, jnp.where(lane == 2.0, w1, 1.0 - w1)))


def _router(x, mods, router_pad, *, geom, tm):
    nb, nct_rows, seq = geom
    t = x.shape[0]
    nct, lt = nct_rows // tm, seq // tm
    return pl.pallas_call(
        _router_kernel,
        out_shape=(jax.ShapeDtypeStruct((t, D), jnp.float32), jax.ShapeDtypeStruct((t, LANES), jnp.float32)),
        grid=(t // tm,),
        in_specs=[pl.BlockSpec((tm, D), lambda i: (i, 0)), _mod_spec(3, nct, lt, nb), _mod_spec(4, nct, lt, nb),
                  pl.BlockSpec((D, LANES), lambda i: (0, 0))],
        out_specs=(pl.BlockSpec((tm, D), lambda i: (i, 0)), pl.BlockSpec((tm, LANES), lambda i: (i, 0))),
        name="moe_router",
        compiler_params=_cparams(("arbitrary",)),
    )(x, mods, mods, router_pad)


def _swiglu_step(h, wg_ref, wu_ref, wd_ref, row_scale):
    g = jnp.dot(h, wg_ref[...], preferred_element_type=jnp.float32)
    u = jnp.dot(h, wu_ref[...], preferred_element_type=jnp.float32)
    a = g * _sigmoid(g) * u
    if row_scale is not None:
        a = a * row_scale
    return jnp.dot(a.astype(jnp.bfloat16), wd_ref[...], preferred_element_type=jnp.float32)


def _ffn_kernel(x_ref, sh_ref, sc_ref, gate_ref, wg_ref, wu_ref, wd_ref, o_ref, h_ref, acc_ref):
    f = pl.program_id(1)

    @pl.when(f == 0)
    def _():
        h_ref[...] = _modulate(x_ref[...], sh_ref[...], sc_ref[...]).astype(jnp.bfloat16)
        acc_ref[...] = jnp.zeros_like(acc_ref)

    acc_ref[...] += _swiglu_step(h_ref[...], wg_ref, wu_ref, wd_ref, None)

    @pl.when(f == pl.num_programs(1) - 1)
    def _():
        o_ref[...] = x_ref[...] + gate_ref[...] * acc_ref[...]


def _ffn(x, mods, wgu, wd, *, geom, tm, tf):
    nb, nct_rows, seq = geom
    t = x.shape[0]
    nf = wd.shape[0] // tf
    nct, lt = nct_rows // tm, seq // tm
    return pl.pallas_call(
        _ffn_kernel,
        out_shape=jax.ShapeDtypeStruct((t, D), jnp.float32),
        grid=(t // tm, nf),
        in_specs=[pl.BlockSpec((tm, D), lambda i, f: (i, 0)),
                  _mod_spec(3, nct, lt, nb),
                  _mod_spec(4, nct, lt, nb),
                  _mod_spec(5, nct, lt, nb),
                  pl.BlockSpec((D, tf), lambda i, f: (0, f)),
                  pl.BlockSpec((D, tf), lambda i, f: (0, nf + f)),
                  pl.BlockSpec((tf, D), lambda i, f: (f, 0))],
        out_specs=pl.BlockSpec((tm, D), lambda i, f: (i, 0)),
        scratch_shapes=[pltpu.VMEM((tm, D), jnp.bfloat16), pltpu.VMEM((tm, D), jnp.float32)],
        name="dense_ffn",
        compiler_params=_cparams(("arbitrary", "arbitrary")),
    )(x, mods, mods, mods, wgu, wgu, wd)


def _row_gather_start(src_hbm, rows_ref, buf, sem, first, count):
    for r in range(count):
        pltpu.make_async_copy(src_hbm.at[pl.ds(rows_ref[0, first + r], 1), :],
                              buf.at[pl.ds(first + r, 1), :], sem).start()


def _moe_kernel(te_ref, nt_ref, rows0_ref, rows1_ref, h_hbm, wg_ref, wu_ref, wd_ref, o_ref,
                hbuf, hb_ref, acc_ref, sem, *, tm, chunk):
    del te_ref
    i, f = pl.program_id(0), pl.program_id(1)
    last_i, last_f = pl.num_programs(0) - 1, pl.num_programs(1) - 1
    slot = i % 2
    nt = nt_ref[0]
    wait_all = lambda s: pltpu.make_async_copy(hbuf.at[s], hbuf.at[s], sem.at[s]).wait()

    @pl.when((i == 0) & (f == 0))
    def _():
        _row_gather_start(h_hbm, rows0_ref, hbuf.at[0], sem.at[0], 0, tm)

    @pl.when((f == 0) & (i <= nt))
    def _():
        wait_all(slot)
        hb_ref[...] = hbuf[slot].astype(jnp.bfloat16)
        acc_ref[...] = jnp.zeros_like(acc_ref)

    @pl.when(i < nt)
    def _():
        _row_gather_start(h_hbm, rows1_ref, hbuf.at[1 - slot], sem.at[1 - slot], f * chunk, chunk)
        acc_ref[...] += _swiglu_step(hb_ref[...], wg_ref, wu_ref, wd_ref, None)

    @pl.when(f == last_f)
    def _():
        o_ref[...] = acc_ref[...]

    @pl.when((i == last_i) & (f == last_f) & (i < nt))
    def _():
        wait_all(1 - slot)


def _moe_experts(h, tile_expert, n_tiles, rows, wgu, wd, *, tm, tf):
    nt = rows.shape[0]
    nf = wd.shape[1] // tf
    chunk = tm // nf
    assert chunk * nf == tm
    grid_spec = pltpu.PrefetchScalarGridSpec(
        num_scalar_prefetch=2,
        grid=(nt, nf),
        in_specs=[pl.BlockSpec((None, 1, tm), lambda i, f, te, n: (i, 0, 0), memory_space=pltpu.SMEM),
                  pl.BlockSpec((None, 1, tm), lambda i, f, te, n: (jnp.minimum(i + 1, nt - 1), 0, 0),
                               memory_space=pltpu.SMEM),
                  pl.BlockSpec(memory_space=pl.ANY),
                  pl.BlockSpec((None, D, tf), lambda i, f, te, n: (te[i], 0, f)),
                  pl.BlockSpec((None, D, tf), lambda i, f, te, n: (te[i], 0, nf + f)),
                  pl.BlockSpec((None, tf, D), lambda i, f, te, n: (te[i], f, 0))],
        out_specs=pl.BlockSpec((tm, D), lambda i, f, te, n: (i, 0)),
        scratch_shapes=[pltpu.VMEM((2, tm, D), jnp.float32), pltpu.VMEM((tm, D), jnp.bfloat16),
                        pltpu.VMEM((tm, D), jnp.float32), pltpu.SemaphoreType.DMA((2,))])
    return pl.pallas_call(
        functools.partial(_moe_kernel, tm=tm, chunk=chunk),
        out_shape=jax.ShapeDtypeStruct((nt * tm, D), jnp.float32),
        grid_spec=grid_spec,
        name="moe_experts",
        compiler_params=_cparams(("arbitrary", "arbitrary")),
    )(tile_expert, n_tiles, rows, rows, h, wgu, wgu, wd)


def _combine_kernel(rows0_ref, rows1_ref, ys_hbm, x_ref, gate_ref, info_ref, o_ref, buf, sem, *, tmc):
    i = pl.program_id(0)
    slot = i % 2
    n = 2 * tmc

    def start(rows_ref, s):
        def body(r, carry):
            pltpu.make_async_copy(ys_hbm.at[pl.ds(rows_ref[0, r], 1), :], buf.at[s, pl.ds(r, 1), :],
                                  sem.at[s]).start()
            return carry
        lax.fori_loop(0, n, body, 0, unroll=8)

    @pl.when(i == 0)
    def _():
        start(rows0_ref, 0)

    @pl.when(i + 1 < pl.num_programs(0))
    def _():
        start(rows1_ref, 1 - slot)

    pltpu.make_async_copy(buf.at[slot], buf.at[slot], sem.at[slot]).wait()
    info = info_ref[...]
    y = info[:, 2:3] * buf[slot, 0:tmc, :] + info[:, 3:4] * buf[slot, tmc:n, :]
    o_ref[...] = x_ref[...] + gate_ref[...] * y


def _moe_combine(x, ys, rows, info, mods, *, geom, tmc):
    nb, nct_rows, seq = geom
    t = x.shape[0]
    nct, lt = nct_rows // tmc, seq // tmc
    ntile = rows.shape[0]
    return pl.pallas_call(
        functools.partial(_combine_kernel, tmc=tmc),
        out_shape=jax.ShapeDtypeStruct((t, D), jnp.float32),
        grid=(ntile,),
        in_specs=[pl.BlockSpec((None, 1, 2 * tmc), lambda i: (i, 0, 0), memory_space=pltpu.SMEM),
                  pl.BlockSpec((None, 1, 2 * tmc), lambda i: (jnp.minimum(i + 1, ntile - 1), 0, 0),
                               memory_space=pltpu.SMEM),
                  pl.BlockSpec(memory_space=pl.ANY),
                  pl.BlockSpec((tmc, D), lambda i: (i, 0)),
                  _mod_spec(5, nct, lt, nb),
                  pl.BlockSpec((tmc, LANES), lambda i: (i, 0))],
        out_specs=pl.BlockSpec((tmc, D), lambda i: (i, 0)),
        scratch_shapes=[pltpu.VMEM((2, 2 * tmc, D), jnp.float32), pltpu.SemaphoreType.DMA((2,))],
        name="moe_combine",
        compiler_params=_cparams(("arbitrary",)),
    )(rows, rows, ys, x, mods, info)


def _moe(x, mods, router_pad, wgu, wd, *, geom, tm, tf, tmc):
    tr = x.shape[0]
    h, info = _router(x, mods, router_pad, geom=geom, tm=tm)
    e = info[:, 0:2].astype(jnp.int32).reshape(-1)
    onehot = (e[:, None] == jnp.arange(N_EXPERTS, dtype=jnp.int32)[None, :]).astype(jnp.int32)
    csum = jnp.cumsum(onehot, axis=0)
    rank = jnp.sum(onehot * (csum - 1), axis=1)
    counts = csum[-1]
    padded = ((counts + tm - 1) // tm) * tm
    ends = jnp.cumsum(padded)
    dest = (ends - padded)[e] + rank
    nt = 2 * tr // tm + N_EXPERTS
    tile_expert = jnp.minimum(jnp.sum(jnp.arange(nt, dtype=jnp.int32)[:, None] * tm >= ends[None, :], axis=1),
                              N_EXPERTS - 1).astype(jnp.int32)
    n_tiles = (ends[-1:] // tm).astype(jnp.int32)
    token_row = jnp.arange(2 * tr, dtype=jnp.int32) // 2
    src = jnp.zeros((nt * tm,), jnp.int32).at[dest].set(token_row)
    ys = _moe_experts(h, tile_expert, n_tiles, src.reshape(nt, 1, tm), wgu, wd, tm=tm, tf=tf)
    crow = dest.reshape(tr // tmc, tmc, 2).transpose(0, 2, 1).reshape(tr // tmc, 1, 2 * tmc)
    return _moe_combine(x, ys, crow, info, mods, geom=geom, tmc=tmc)


def kernel(x, c, ctx, c_ctx, w_mod, b_mod, hg_w_in, hg_lb_logits, hg_norm_w, hg_w_out, sw_w_qkv, sw_q_norm,
           sw_k_norm, sw_sink, sw_w_out, ff_w_gate_up, ff_w_down, moe_router, moe_w_gate_up, moe_w_down):
    nb, seq, _ = x.shape
    ctx_len = ctx.shape[1]
    depth = w_mod.shape[0]
    nct_rows = nb * ctx_len
    geom = (nb, nct_rows, seq)
    tm = 512
    tmb = 1024 if nct_rows % 1024 == 0 and seq % 1024 == 0 else tm
    bf = jnp.bfloat16

    xs = jnp.concatenate([ctx.reshape(nct_rows, D), x.reshape(nb * seq, D)], axis=0)
    cpad = jnp.concatenate([c, c_ctx[None, :], jnp.zeros((8 - nb - 1, D), jnp.float32)], axis=0)
    mods_all = _modvecs(cpad, w_mod, b_mod)

    p_lb = jax.nn.softmax(hg_lb_logits.astype(jnp.float32), axis=0)
    lower_bounds = jnp.cumsum(p_lb, axis=0) - p_lb[:1]

    for i in range(depth):
        ctx_live = i < depth - 1
        mods = mods_all[i]
        j = i // 2
        if i % 2 == 0:
            p = _mod_matmul(xs, mods, hg_w_in[j].astype(bf), k_shift=0, geom=geom, tm=tmb, tn=1024,
                            out_dtype=jnp.float32)
            of, ob = _hg_scan(p, lower_bounds[j, 0:1], lower_bounds[j, 1:2], geom)
            xs = _hg_out(of, ob, p, hg_norm_w[j][None, :], hg_w_out[j].astype(bf), xs, mods,
                         geom=geom, tm=tmb, tn=1024)
        else:
            wq, wk, wv = jnp.split(sw_w_qkv[j], [D, D + SW_KV * SW_DH], axis=1)
            rep = lambda w: jnp.repeat(w.reshape(D, SW_KV, 1, SW_DH), SW_GROUP, axis=2).reshape(D, D)
            w3 = jnp.concatenate([wq, rep(wk), rep(wv)], axis=1).astype(bf)
            nw2 = jnp.stack([jnp.tile(sw_q_norm[j], SW_HEADS), jnp.tile(sw_k_norm[j], SW_HEADS)])[:, None, :]
            qkv = _qkv_proj(xs, mods, w3, nw2, geom=geom, tm=tmb)
            o = _attention(qkv, sw_sink[j], geom, ctx_live)
            xs = _res_matmul(o, sw_w_out[j].astype(bf), xs, mods, geom=geom, tm=tmb, tn=1024)
            if not ctx_live:
                geom = (nb, 0, seq)
        if i % 2 == 0:
            xs = _ffn(xs, mods, ff_w_gate_up[j].astype(bf), ff_w_down[j].astype(bf), geom=geom, tm=tm, tf=1408)
        else:
            rpad = jnp.pad(moe_router[j], ((0, 0), (0, LANES - N_EXPERTS)))
            xs = _moe(xs, mods, rpad, moe_w_gate_up[j].astype(bf), moe_w_down[j].astype(bf),
                      geom=geom, tm=tm, tf=1792, tmc=256)
    return xs[xs.shape[0] - nb * seq:].reshape(nb, seq, D)
```

```python
import functools

import numpy as np
import jax
import jax.numpy as jnp
from jax import lax
from jax.experimental import pallas as pl
from jax.experimental.pallas import tpu as pltpu

D = 1024
EPS = 1e-6
NEG_BIG = -1e30
LOG2E = 1.4426950408889634
GRID_W = 64
ROPE_THETA = 10000.0

HG_HEADS = 8
HG_DK = 128
HG_FDIM = HG_HEADS * HG_DK
HG_CHUNK = 128
HG_LEVELS = 7
HG_ROWS = 256
HG_HB = 2

SW_HEADS = 16
SW_KV = 4
SW_GROUP = 4
SW_DH = 64
SW_BLOCK = 128

N_EXPERTS = 8
LANES = 128

VMEM_LIMIT = 56 * 1024 * 1024


def _cparams(sem):
    return pltpu.CompilerParams(dimension_semantics=sem, vmem_limit_bytes=VMEM_LIMIT)


def _sigmoid(x):
    return 1.0 / (1.0 + jnp.exp2(x * -LOG2E))


def _mod_row(i, nct, lt, nb):
    return jnp.where(i < nct, nb, (i - nct) // lt)


def _mod_spec(k, nct, lt, nb):
    return pl.BlockSpec((None, 1, D), lambda i, *_: (_mod_row(i, nct, lt, nb) * 6 + k, 0, 0))


def _modulate(x, shift, scale):
    ms = jnp.mean(x * x, axis=-1, keepdims=True)
    return (x * lax.rsqrt(ms + EPS)) * (1.0 + scale) + shift


def _modvec_kernel(c_ref, w_ref, b_ref, o_ref):
    c = c_ref[...]
    s = c * _sigmoid(c)
    o_ref[...] = jnp.dot(s, w_ref[...], precision=lax.Precision.HIGHEST,
                         preferred_element_type=jnp.float32) + b_ref[...]


def _modvecs(cpad, w_mod, b_mod):
    depth = w_mod.shape[0]
    tn = 1024
    out = pl.pallas_call(
        _modvec_kernel,
        out_shape=jax.ShapeDtypeStruct((depth, 8, 6 * D), jnp.float32),
        grid=(depth, 6 * D // tn),
        in_specs=[pl.BlockSpec((8, D), lambda l, j: (0, 0)),
                  pl.BlockSpec((None, D, tn), lambda l, j: (l, 0, j)),
                  pl.BlockSpec((None, 1, tn), lambda l, j: (l, 0, j))],
        out_specs=pl.BlockSpec((None, 8, tn), lambda l, j: (l, 0, j)),
        name="adaln_vectors",
        compiler_params=_cparams(("arbitrary", "arbitrary")),
    )(cpad, w_mod, b_mod.reshape(depth, 1, 6 * D))
    return out.reshape(depth, 8 * 6, 1, D)


def _modmm_kernel(x_ref, sh_ref, sc_ref, w_ref, o_ref, h_ref):
    @pl.when(pl.program_id(1) == 0)
    def _():
        h_ref[...] = _modulate(x_ref[...], sh_ref[...], sc_ref[...]).astype(jnp.bfloat16)

    o_ref[...] = jnp.dot(h_ref[...], w_ref[...], preferred_element_type=jnp.float32).astype(o_ref.dtype)


def _mod_matmul(x, mods, w, *, k_shift, geom, tm, tn, out_dtype):
    nb, nct_rows, seq = geom
    t, n = x.shape[0], w.shape[1]
    nct, lt = nct_rows // tm, seq // tm
    return pl.pallas_call(
        _modmm_kernel,
        out_shape=jax.ShapeDtypeStruct((t, n), out_dtype),
        grid=(t // tm, n // tn),
        in_specs=[pl.BlockSpec((tm, D), lambda i, j: (i, 0)),
                  _mod_spec(k_shift, nct, lt, nb),
                  _mod_spec(k_shift + 1, nct, lt, nb),
                  pl.BlockSpec((D, tn), lambda i, j: (0, j))],
        out_specs=pl.BlockSpec((tm, tn), lambda i, j: (i, j)),
        scratch_shapes=[pltpu.VMEM((tm, D), jnp.bfloat16)],
        name="hg_in_proj",
        compiler_params=_cparams(("arbitrary", "arbitrary")),
    )(x, mods, mods, w)


def _hg_tables(fwd):
    c = HG_CHUNK
    t = np.arange(c)[:, None]
    u = np.arange(c)[None, :]
    reach = (u <= t) if fwd else (u >= t)
    x = t ^ u
    lvl = np.where(x > 0, np.floor(np.log2(np.maximum(x, 1))), HG_LEVELS)
    lvl = np.where(reach, lvl, -1).astype(np.int32)
    tri = np.concatenate([reach.astype(np.float32)] * 3, axis=1)
    return tri, lvl


def _hg_gates(q_raw, z, lb, tri_ref, a_ref):
    bf = jnp.bfloat16
    q = q_raw * _sigmoid(q_raw)
    sig = _sigmoid(z)
    fc = jnp.maximum(lb + (1.0 - lb) * sig, 1e-30)
    lf = jnp.log(fc) * LOG2E
    k = (1.0 - lb) * (1.0 - sig)
    hi = lf.astype(bf)
    rest = lf - hi.astype(jnp.float32)
    mid = rest.astype(bf)
    lo = (rest - mid.astype(jnp.float32)).astype(bf)
    a = jnp.dot(tri_ref[...], jnp.concatenate([hi, mid, lo], axis=0), preferred_element_type=jnp.float32)
    a_ref[...] = a
    return q, k, fc, a


def _hg_scores(q, k, fc, a, a_ref, lvl, fwd):
    c = HG_CHUNK
    bf = jnp.bfloat16

    def rows_of(row, n):
        return jnp.broadcast_to(a_ref[row:row + 1, :], (n, HG_DK))

    rowi = lax.broadcasted_iota(jnp.int32, (c, HG_DK), 0)
    nt = (((1,), (1,)), ((), ()))
    zeros8 = jnp.zeros((8, HG_DK), jnp.float32)

    def level_operands(l):
        w = 1 << l
        near = w - 1 if fwd else w
        if w < 8:
            upper = ((rowi >> l) & 1) == 1
            q_side = upper if fwd else jnp.logical_not(upper)
            if l == 0:
                qe, ke = q * fc, k
            else:
                if l == 1:
                    first = jnp.concatenate([rows_of(8 * g + near, 8) for g in range(c // 8)], axis=0)
                    second = jnp.concatenate([rows_of(8 * g + 4 + near, 8) for g in range(c // 8)], axis=0)
                    bnd = jnp.where((rowi & 4) == 0, first, second)
                else:
                    bnd = jnp.concatenate([rows_of(8 * g + near, 8) for g in range(c // 8)], axis=0)
                e = jnp.exp2(-jnp.abs(a - bnd))
                qe, ke = q * e, k * e
            return jnp.where(q_side, qe, 0.0).astype(bf), jnp.where(q_side, 0.0, ke).astype(bf)
        qp, kp = [], []
        for j in range(c // w):
            rs = slice(j * w, (j + 1) * w)
            bnd = rows_of((j // 2) * 2 * w + near, w)
            zero = jnp.concatenate([zeros8] * (w // 8), axis=0)
            if (j % 2 == 1) == fwd:
                qp.append(q[rs] * jnp.exp2(a[rs] - bnd))
                kp.append(zero)
            else:
                qp.append(zero)
                kp.append(k[rs] * jnp.exp2(bnd - a[rs]))
        return jnp.concatenate(qp, axis=0).astype(bf), jnp.concatenate(kp, axis=0).astype(bf)

    scores = jnp.where(lvl == HG_LEVELS,
                       lax.dot_general(q.astype(bf), k.astype(bf), nt, preferred_element_type=jnp.float32), 0.0)
    for l in range(HG_LEVELS):
        qm, km = level_operands(l)
        sc = lax.dot_general(qm, km, nt, preferred_element_type=jnp.float32)
        scores = jnp.where(lvl == l, sc, scores)
    return scores.astype(bf)


def _hg_finish(q, k, v, a, scores, st_ref, fwd):
    c = HG_CHUNK
    bf = jnp.bfloat16
    nt = (((1,), (1,)), ((), ()))
    last = c - 1 if fwd else 0
    e_in = jnp.exp2(a)
    e_out = jnp.exp2(a[last:last + 1, :] - a)
    st = st_ref[...]
    o = (jnp.dot(scores, v.astype(bf), preferred_element_type=jnp.float32)
         + lax.dot_general((q * e_in).astype(bf), st.astype(bf), nt, preferred_element_type=jnp.float32))
    st_ref[...] = (e_in[last:last + 1, :] * st
                   + jnp.dot(v.T.astype(bf), (k * e_out).astype(bf), preferred_element_type=jnp.float32))
    return o


def _hg_scan_kernel(qf_ref, vf_ref, zf_ref, qb_ref, vb_ref, zb_ref, lbf_ref, lbb_ref, trif_ref, trib_ref,
                    lvlf_ref, lvlb_ref, of_ref, ob_ref, sf_ref, sb_ref, a_ref):
    @pl.when(pl.program_id(2) == 0)
    def _():
        sf_ref[...] = jnp.zeros_like(sf_ref)
        sb_ref[...] = jnp.zeros_like(sb_ref)

    c = HG_CHUNK
    nch = HG_ROWS // c
    lvlf, lvlb = lvlf_ref[...], lvlb_ref[...]
    units = []
    for hh in range(HG_HB):
        cs = slice(hh * HG_DK, (hh + 1) * HG_DK)
        for step, ci in enumerate(range(nch)):
            units.append(dict(step=step, rows=slice(ci * c, (ci + 1) * c), cols=cs, fwd=True, q=qf_ref, v=vf_ref,
                              z=zf_ref, lb=lbf_ref, tri=trif_ref, lvl=lvlf, st=sf_ref.at[hh], out=of_ref))
        for step, ci in enumerate(reversed(range(nch))):
            units.append(dict(step=step, rows=slice(ci * c, (ci + 1) * c), cols=cs, fwd=False, q=qb_ref, v=vb_ref,
                              z=zb_ref, lb=lbb_ref, tri=trib_ref, lvl=lvlb, st=sb_ref.at[hh], out=ob_ref))
    for n, u in enumerate(units):
        u["a_ref"] = a_ref.at[n]
        u["q"], u["k"], u["fc"], u["a"] = _hg_gates(u["q"][u["rows"], u["cols"]], u["z"][u["rows"], u["cols"]],
                                                    u["lb"][:, u["cols"]], u["tri"], u["a_ref"])
    for u in units:
        u["scores"] = _hg_scores(u["q"], u["k"], u["fc"], u["a"], u["a_ref"], u["lvl"], u["fwd"])
    for step in range(nch):
        for u in units:
            if u["step"] == step:
                o = _hg_finish(u["q"], u["k"], u["v"][u["rows"], u["cols"]], u["a"], u["scores"], u["st"], u["fwd"])
                u["out"][u["rows"], u["cols"]] = o.astype(u["out"].dtype)


def _hg_scan(p, lbf, lbb, geom):
    nb, nct_rows, seq = geom
    t = p.shape[0]
    r = HG_ROWS
    cb, lb_ = (nct_rows // nb) // r, seq // r
    nsteps = cb + lb_
    lat0 = nct_rows // r

    def fblk(b, s):
        return jnp.where(s < cb, b * cb + s, lat0 + b * lb_ + (s - cb))

    def bblk(b, s):
        return jnp.where(s < cb, b * cb + (cb - 1 - s), lat0 + b * lb_ + (lb_ - 1 - (s - cb)))

    wcol = HG_HB * HG_DK
    ngrp = HG_HEADS // HG_HB
    units = 2 * HG_HB * (r // HG_CHUNK)

    def spec(blk, colblock):
        return pl.BlockSpec((r, wcol), lambda b, h, s: (blk(b, s), colblock * ngrp + h))

    lbspec = pl.BlockSpec((1, wcol), lambda b, h, s: (0, h))
    const = lambda shape: pl.BlockSpec(shape, lambda b, h, s: (0, 0))
    trif, lvlf = _hg_tables(True)
    trib, lvlb = _hg_tables(False)
    return pl.pallas_call(
        _hg_scan_kernel,
        out_shape=(jax.ShapeDtypeStruct((t, D), jnp.bfloat16), jax.ShapeDtypeStruct((t, D), jnp.bfloat16)),
        grid=(nb, ngrp, nsteps),
        in_specs=[spec(fblk, 0), spec(fblk, 1), spec(fblk, 2),
                  spec(bblk, 0), spec(bblk, 1), spec(bblk, 3),
                  lbspec, lbspec, const(trif.shape), const(trib.shape), const(lvlf.shape), const(lvlb.shape)],
        out_specs=(pl.BlockSpec((r, wcol), lambda b, h, s: (fblk(b, s), h)),
                   pl.BlockSpec((r, wcol), lambda b, h, s: (bblk(b, s), h))),
        scratch_shapes=[pltpu.VMEM((HG_HB, HG_DK, HG_DK), jnp.float32),
                        pltpu.VMEM((HG_HB, HG_DK, HG_DK), jnp.float32),
                        pltpu.VMEM((units, HG_CHUNK, HG_DK), jnp.float32)],
        name="hg_scan",
        compiler_params=_cparams(("arbitrary", "arbitrary", "arbitrary")),
    )(p, p, p, p, p, p, lbf, lbb, jnp.asarray(trif, jnp.bfloat16), jnp.asarray(trib, jnp.bfloat16),
      jnp.asarray(lvlf), jnp.asarray(lvlb))


def _hg_out_kernel(of_ref, ob_ref, g_ref, nw_ref, w_ref, x_ref, gate_ref, o_ref, h_ref):
    @pl.when(pl.program_id(1) == 0)
    def _():
        nw = nw_ref[...]
        for h in range(HG_HEADS):
            cs = slice(h * HG_DK, (h + 1) * HG_DK)
            o = of_ref[:, cs].astype(jnp.float32) + ob_ref[:, cs].astype(jnp.float32)
            y = o * lax.rsqrt(jnp.mean(o * o, axis=-1, keepdims=True) + EPS) * nw
            g = g_ref[:, cs]
            h_ref[:, cs] = (y * (g * _sigmoid(g))).astype(jnp.bfloat16)

    acc = jnp.dot(h_ref[...], w_ref[...], preferred_element_type=jnp.float32)
    o_ref[...] = x_ref[...] + gate_ref[...] * acc


def _hg_out(of, ob, p, norm_w, w, x, mods, *, geom, tm, tn):
    nb, nct_rows, seq = geom
    t = x.shape[0]
    nct, lt = nct_rows // tm, seq // tm
    gate = pl.BlockSpec((None, 1, tn), lambda i, j: (_mod_row(i, nct, lt, nb) * 6 + 2, 0, j))
    return pl.pallas_call(
        _hg_out_kernel,
        out_shape=jax.ShapeDtypeStruct((t, D), jnp.float32),
        grid=(t // tm, D // tn),
        in_specs=[pl.BlockSpec((tm, D), lambda i, j: (i, 0)),
                  pl.BlockSpec((tm, D), lambda i, j: (i, 0)),
                  pl.BlockSpec((tm, D), lambda i, j: (i, 4)),
                  pl.BlockSpec((1, HG_DK), lambda i, j: (0, 0)),
                  pl.BlockSpec((D, tn), lambda i, j: (0, j)),
                  pl.BlockSpec((tm, tn), lambda i, j: (i, j)),
                  gate],
        out_specs=pl.BlockSpec((tm, tn), lambda i, j: (i, j)),
        scratch_shapes=[pltpu.VMEM((tm, D), jnp.bfloat16)],
        name="hg_out_proj",
        compiler_params=_cparams(("arbitrary", "arbitrary")),
    )(of, ob, p, norm_w, w, x, mods)


def _resmm_kernel(a_ref, w_ref, x_ref, gate_ref, o_ref):
    acc = jnp.dot(a_ref[...], w_ref[...], preferred_element_type=jnp.float32)
    o_ref[...] = x_ref[...] + gate_ref[...] * acc


def _res_matmul(a, w, x, mods, *, geom, tm, tn):
    nb, nct_rows, seq = geom
    rows = a.shape[0]
    t0 = (x.shape[0] - rows) // tm
    nct, lt = nct_rows // tm, seq // tm
    gate = pl.BlockSpec((None, 1, tn), lambda i, j: (_mod_row(i + t0, nct, lt, nb) * 6 + 2, 0, j))
    return pl.pallas_call(
        _resmm_kernel,
        out_shape=jax.ShapeDtypeStruct((rows, D), jnp.float32),
        grid=(rows // tm, D // tn),
        in_specs=[pl.BlockSpec((tm, a.shape[1]), lambda i, j: (i, 0)),
                  pl.BlockSpec((a.shape[1], tn), lambda i, j: (0, j)),
                  pl.BlockSpec((tm, tn), lambda i, j: (i + t0, j)),
                  gate],
        out_specs=pl.BlockSpec((tm, tn), lambda i, j: (i, j)),
        name="attn_out_proj",
        compiler_params=_cparams(("arbitrary", "arbitrary")),
    )(a, w, x, mods)


def _rope_tables(seq, tm):
    rows = seq // GRID_W
    row = np.repeat(np.arange(rows, dtype=np.float32), GRID_W)
    col = np.tile(np.arange(GRID_W, dtype=np.float32), rows)
    nf = SW_DH // 4
    inv = (ROPE_THETA ** (-np.arange(nf, dtype=np.float32) / nf)).astype(np.float32)
    ang_r = row[:, None] * inv
    ang_c = col[:, None] * inv
    cos = np.concatenate([np.cos(ang_r), np.cos(ang_r), np.cos(ang_c), np.cos(ang_c)], axis=1)
    sin = np.concatenate([-np.sin(ang_r), np.sin(ang_r), -np.sin(ang_c), np.sin(ang_c)], axis=1)
    cos = np.concatenate([np.tile(cos, (1, 2)), np.ones((tm, LANES), np.float32)], axis=0)
    sin = np.concatenate([np.tile(sin, (1, 2)), np.zeros((tm, LANES), np.float32)], axis=0)
    return jnp.asarray(cos, jnp.float32), jnp.asarray(sin, jnp.float32)


def _qkv_kernel(x_ref, sh_ref, sc_ref, w_ref, nw_ref, ones_ref, cos_ref, sin_ref, o_ref, h_ref):
    j = pl.program_id(1)

    @pl.when(j == 0)
    def _():
        h_ref[...] = _modulate(x_ref[...], sh_ref[...], sc_ref[...]).astype(jnp.bfloat16)

    acc = jnp.dot(h_ref[...], w_ref[...], preferred_element_type=jnp.float32)

    @pl.when(j == 2)
    def _():
        o_ref[...] = acc.astype(o_ref.dtype)

    @pl.when(j < 2)
    def _():
        ss = jnp.dot((acc * acc).astype(jnp.bfloat16), ones_ref[...], preferred_element_type=jnp.float32)
        y = acc * lax.rsqrt(ss * (1.0 / SW_DH) + EPS) * nw_ref[...]
        cos = cos_ref[...]
        sin = sin_ref[...]
        lane = lax.broadcasted_iota(jnp.int32, (y.shape[0], LANES), 1)
        first = (lane & 16) == 0
        scale = jnp.where(j == 0, SW_DH ** -0.5 * LOG2E, 1.0)
        for g in range(D // LANES):
            cs = slice(g * LANES, (g + 1) * LANES)
            yg = y[:, cs]
            partner = jnp.where(first, pltpu.roll(yg, LANES - 16, 1), pltpu.roll(yg, 16, 1))
            o_ref[:, cs] = ((yg * cos + partner * sin) * scale).astype(o_ref.dtype)


def _qkv_proj(x, mods, w3, nw2, *, geom, tm):
    nb, nct_rows, seq = geom
    t = x.shape[0]
    nct, lt = nct_rows // tm, seq // tm
    cos, sin = _rope_tables(seq, tm)
    blk = np.kron(np.eye(D // SW_DH, dtype=np.float32), np.ones((SW_DH, SW_DH), np.float32))
    ones = jnp.asarray(blk, jnp.bfloat16)
    tab = lambda i, j: (jnp.where(i < nct, lt, (i - nct) % lt), 0)
    return pl.pallas_call(
        _qkv_kernel,
        out_shape=jax.ShapeDtypeStruct((t, 3 * D), jnp.bfloat16),
        grid=(t // tm, 3),
        in_specs=[pl.BlockSpec((tm, D), lambda i, j: (i, 0)),
                  _mod_spec(0, nct, lt, nb),
                  _mod_spec(1, nct, lt, nb),
                  pl.BlockSpec((D, D), lambda i, j: (0, j)),
                  pl.BlockSpec((None, 1, D), lambda i, j: (jnp.minimum(j, 1), 0, 0)),
                  pl.BlockSpec((D, D), lambda i, j: (0, 0)),
                  pl.BlockSpec((tm, LANES), tab),
                  pl.BlockSpec((tm, LANES), tab)],
        out_specs=pl.BlockSpec((tm, D), lambda i, j: (i, j)),
        scratch_shapes=[pltpu.VMEM((tm, D), jnp.bfloat16)],
        name="qkv_proj",
        compiler_params=_cparams(("arbitrary", "arbitrary")),
    )(x, mods, mods, w3, nw2, ones, cos, sin)


def _attn_items(items, sink_ref, kv, o_ref):
    nt = (((1,), (1,)), ((), ()))
    work = [(it, g) for it in items for g in range(SW_GROUP)]
    scores = []
    for (_, q, kcat, _, _), g in work:
        grp = lax.broadcasted_iota(jnp.int32, q.shape, 1) // SW_DH
        scores.append(lax.dot_general(jnp.where(grp == g, q, jnp.zeros_like(q)), kcat, nt,
                                      preferred_element_type=jnp.float32))
    probs, denoms = [], []
    for ((_, _, _, _, valid), g), s in zip(work, scores):
        sink = sink_ref[kv * SW_GROUP + g] * LOG2E
        if valid is not None:
            s = jnp.where(valid, s, NEG_BIG)
        m = jnp.maximum(jnp.max(s, axis=-1, keepdims=True), sink)
        p = jnp.exp2(s - m)
        denoms.append(jnp.sum(p, axis=-1, keepdims=True) + jnp.exp2(sink - m))
        probs.append(p.astype(jnp.bfloat16))
    outs = {}
    for n, ((rows, q, _, vcat, _), g) in enumerate(work):
        grp = lax.broadcasted_iota(jnp.int32, q.shape, 1) // SW_DH
        og = jnp.dot(probs[n], vcat, preferred_element_type=jnp.float32) / denoms[n]
        outs[rows] = og if g == 0 else jnp.where(grp == g, og, outs[rows])
    for rows, out in outs.items():
        o_ref[rows[0]:rows[1], :] = out.astype(o_ref.dtype)


def _attn_kernel(sink_ref, q_ref, kp_ref, kc_ref, kn_ref, kx_ref, vp_ref, vc_ref, vn_ref, vx_ref, o_ref,
                 *, nstep, ctx_out):
    kv, j = pl.program_id(1), pl.program_id(2)
    blk = SW_BLOCK

    def latent():
        kx, vx = kx_ref[...], vx_ref[...]
        k0, k1, v0, v1 = kc_ref[0:blk, :], kc_ref[blk:2 * blk, :], vc_ref[0:blk, :], vc_ref[blk:2 * blk, :]
        ns = 3 * blk + kx.shape[0]
        t = lax.broadcasted_iota(jnp.int32, (blk, ns), 0)
        s = lax.broadcasted_iota(jnp.int32, (blk, ns), 1)

        def window(has_prev, has_next):
            lo = jnp.maximum(t, jnp.where(has_prev, 0, blk))
            hi = jnp.minimum(t + 2 * blk, jnp.where(has_next, 3 * blk - 1, 2 * blk - 1))
            return ((s >= lo) & (s <= hi)) | (s >= 3 * blk)

        items = [((0, blk), q_ref[0:blk, :], jnp.concatenate([kp_ref[...], k0, k1, kx], axis=0),
                  jnp.concatenate([vp_ref[...], v0, v1, vx], axis=0), window(j > 0, True)),
                 ((blk, 2 * blk), q_ref[blk:2 * blk, :], jnp.concatenate([k0, k1, kn_ref[...], kx], axis=0),
                  jnp.concatenate([v0, v1, vn_ref[...], vx], axis=0), window(True, j < nstep - 1))]
        _attn_items(items, sink_ref, kv, o_ref)

    if not ctx_out:
        latent()
        return
    pl.when(j < nstep)(latent)

    @pl.when(j >= nstep)
    def _():
        _attn_items([((0, 2 * blk), q_ref[...], kx_ref[...], vx_ref[...], None)], sink_ref, kv, o_ref)


def _attention(qkv, sink, geom, ctx_out):
    nb, nct_rows, seq = geom
    ctx_len = nct_rows // nb
    pair = 2 * SW_BLOCK
    assert ctx_len == pair and seq % pair == 0
    nblk = seq // SW_BLOCK
    nstep = seq // pair
    lat0 = nct_rows // SW_BLOCK
    w = SW_GROUP * SW_DH
    smem = pl.BlockSpec(memory_space=pltpu.SMEM)

    def qrow(b, j):
        return jnp.where(j < nstep, nct_rows // pair + b * nstep + j, b)

    def edge(off, colblock):
        return pl.BlockSpec((SW_BLOCK, w), lambda b, kv, j: (
            lat0 + b * nblk + jnp.clip(2 * jnp.minimum(j, nstep - 1) + off, 0, nblk - 1), colblock * SW_KV + kv))

    def center(colblock):
        return pl.BlockSpec((pair, w), lambda b, kv, j: (qrow(b, jnp.minimum(j, nstep - 1)), colblock * SW_KV + kv))

    def ctxkv(colblock):
        return pl.BlockSpec((ctx_len, w), lambda b, kv, j: (b, colblock * SW_KV + kv))

    if ctx_out:
        steps, out_rows = nstep + 1, qkv.shape[0]
        out_spec = pl.BlockSpec((pair, w), lambda b, kv, j: (qrow(b, j), kv))
    else:
        steps, out_rows = nstep, nb * seq
        out_spec = pl.BlockSpec((pair, w), lambda b, kv, j: (b * nstep + j, kv))
    return pl.pallas_call(
        functools.partial(_attn_kernel, nstep=nstep, ctx_out=ctx_out),
        out_shape=jax.ShapeDtypeStruct((out_rows, D), jnp.bfloat16),
        grid=(nb, SW_KV, steps),
        in_specs=[smem, pl.BlockSpec((pair, w), lambda b, kv, j: (qrow(b, j), kv)),
                  edge(-1, 1), center(1), edge(2, 1), ctxkv(1),
                  edge(-1, 2), center(2), edge(2, 2), ctxkv(2)],
        out_specs=out_spec,
        name="attention",
        compiler_params=_cparams(("arbitrary", "arbitrary", "arbitrary")),
    )(sink, qkv, qkv, qkv, qkv, qkv, qkv, qkv, qkv, qkv)


def _router_kernel(x_ref, sh_ref, sc_ref, r_ref, h_ref, info_ref):
    h = _modulate(x_ref[...], sh_ref[...], sc_ref[...])
    h_ref[...] = h
    logits = jnp.dot(h, r_ref[...], precision=lax.Precision.HIGHEST, preferred_element_type=jnp.float32)
    lane = lax.broadcasted_iota(jnp.int32, logits.shape, 1).astype(jnp.float32)
    logits = jnp.where(lane < N_EXPERTS, logits, -jnp.inf)
    l1 = jnp.max(logits, axis=-1, keepdims=True)
    i1 = jnp.min(jnp.where(logits == l1, lane, float(LANES)), axis=-1, keepdims=True)
    rest = jnp.where(lane == i1, -jnp.inf, logits)
    l2 = jnp.max(rest, axis=-1, keepdims=True)
    i2 = jnp.min(jnp.where(rest == l2, lane, float(LANES)), axis=-1, keepdims=True)
    w1 = 1.0 / (1.0 + jnp.exp(l2 - l1))
    info_ref[...] = jnp.where(lane == 0.0, i1, jnp.where(lane == 1.0, i2, jnp.where(lane == 2.0, w1, 1.0 - w1)))


def _router(x, mods, router_pad, *, geom, tm):
    nb, nct_rows, seq = geom
    t = x.shape[0]
    nct, lt = nct_rows // tm, seq // tm
    return pl.pallas_call(
        _router_kernel,
        out_shape=(jax.ShapeDtypeStruct((t, D), jnp.float32), jax.ShapeDtypeStruct((t, LANES), jnp.float32)),
        grid=(t // tm,),
        in_specs=[pl.BlockSpec((tm, D), lambda i: (i, 0)), _mod_spec(3, nct, lt, nb), _mod_spec(4, nct, lt, nb),
                  pl.BlockSpec((D, LANES), lambda i: (0, 0))],
        out_specs=(pl.BlockSpec((tm, D), lambda i: (i, 0)), pl.BlockSpec((tm, LANES), lambda i: (i, 0))),
        name="moe_router",
        compiler_params=_cparams(("arbitrary",)),
    )(x, mods, mods, router_pad)


def _swiglu_step(h, wg_ref, wu_ref, wd_ref, row_scale):
    g = jnp.dot(h, wg_ref[...], preferred_element_type=jnp.float32)
    u = jnp.dot(h, wu_ref[...], preferred_element_type=jnp.float32)
    a = g * _sigmoid(g) * u
    if row_scale is not None:
        a = a * row_scale
    return jnp.dot(a.astype(jnp.bfloat16), wd_ref[...], preferred_element_type=jnp.float32)


def _ffn_kernel(x_ref, sh_ref, sc_ref, gate_ref, wg_ref, wu_ref, wd_ref, o_ref, h_ref, acc_ref):
    f = pl.program_id(1)

    @pl.when(f == 0)
    def _():
        h_ref[...] = _modulate(x_ref[...], sh_ref[...], sc_ref[...]).astype(jnp.bfloat16)
        acc_ref[...] = jnp.zeros_like(acc_ref)

    acc_ref[...] += _swiglu_step(h_ref[...], wg_ref, wu_ref, wd_ref, None)

    @pl.when(f == pl.num_programs(1) - 1)
    def _():
        o_ref[...] = x_ref[...] + gate_ref[...] * acc_ref[...]


def _ffn(x, mods, wgu, wd, *, geom, tm, tf):
    nb, nct_rows, seq = geom
    t = x.shape[0]
    nf = wd.shape[0] // tf
    nct, lt = nct_rows // tm, seq // tm
    return pl.pallas_call(
        _ffn_kernel,
        out_shape=jax.ShapeDtypeStruct((t, D), jnp.float32),
        grid=(t // tm, nf),
        in_specs=[pl.BlockSpec((tm, D), lambda i, f: (i, 0)),
                  _mod_spec(3, nct, lt, nb),
                  _mod_spec(4, nct, lt, nb),
                  _mod_spec(5, nct, lt, nb),
                  pl.BlockSpec((D, tf), lambda i, f: (0, f)),
                  pl.BlockSpec((D, tf), lambda i, f: (0, nf + f)),
                  pl.BlockSpec((tf, D), lambda i, f: (f, 0))],
        out_specs=pl.BlockSpec((tm, D), lambda i, f: (i, 0)),
        scratch_shapes=[pltpu.VMEM((tm, D), jnp.bfloat16), pltpu.VMEM((tm, D), jnp.float32)],
        name="dense_ffn",
        compiler_params=_cparams(("arbitrary", "arbitrary")),
    )(x, mods, mods, mods, wgu, wgu, wd)


def _row_gather_start(src_hbm, rows_ref, buf, sem, first, count):
    for r in range(count):
        pltpu.make_async_copy(src_hbm.at[pl.ds(rows_ref[0, first + r], 1), :],
                              buf.at[pl.ds(first + r, 1), :], sem).start()


def _moe_kernel(te_ref, nt_ref, rows0_ref, rows1_ref, h_hbm, wg_ref, wu_ref, wd_ref, o_ref,
                hbuf, hb_ref, acc_ref, sem, *, tm):
    del te_ref
    i, f = pl.program_id(0), pl.program_id(1)
    last_i, last_f = pl.num_programs(0) - 1, pl.num_programs(1) - 1
    slot = i % 2
    nt = nt_ref[0]
    wait_all = lambda s: pltpu.make_async_copy(hbuf.at[s], hbuf.at[s], sem.at[s]).wait()

    @pl.when((i == 0) & (f == 0))
    def _():
        _row_gather_start(h_hbm, rows0_ref, hbuf.at[0], sem.at[0], 0, tm)

    @pl.when((f == 0) & (i <= nt))
    def _():
        wait_all(slot)
        hb_ref[...] = hbuf[slot].astype(jnp.bfloat16)

    @pl.when((i < nt) & (f == 0))
    def _():
        _row_gather_start(h_hbm, rows1_ref, hbuf.at[1 - slot], sem.at[1 - slot], 0, tm)
        acc_ref[...] = _swiglu_step(hb_ref[...], wg_ref, wu_ref, wd_ref, None)

    @pl.when((i < nt) & (f > 0))
    def _():
        acc_ref[...] += _swiglu_step(hb_ref[...], wg_ref, wu_ref, wd_ref, None)

    @pl.when(f == last_f)
    def _():
        o_ref[...] = acc_ref[...]

    @pl.when((i == last_i) & (f == last_f) & (i < nt))
    def _():
        wait_all(1 - slot)


def _moe_experts(h, tile_expert, n_tiles, rows, wgu, wd, *, tm, tf):
    nt = rows.shape[0]
    nf = wd.shape[1] // tf
    grid_spec = pltpu.PrefetchScalarGridSpec(
        num_scalar_prefetch=2,
        grid=(nt, nf),
        in_specs=[pl.BlockSpec((None, 1, tm), lambda i, f, te, n: (i, 0, 0), memory_space=pltpu.SMEM),
                  pl.BlockSpec((None, 1, tm), lambda i, f, te, n: (jnp.minimum(i + 1, nt - 1), 0, 0),
                               memory_space=pltpu.SMEM),
                  pl.BlockSpec(memory_space=pl.ANY),
                  pl.BlockSpec((None, D, tf), lambda i, f, te, n: (te[i], 0, f)),
                  pl.BlockSpec((None, D, tf), lambda i, f, te, n: (te[i], 0, nf + f)),
                  pl.BlockSpec((None, tf, D), lambda i, f, te, n: (te[i], f, 0))],
        out_specs=pl.BlockSpec((tm, D), lambda i, f, te, n: (i, 0)),
        scratch_shapes=[pltpu.VMEM((2, tm, D), jnp.float32), pltpu.VMEM((tm, D), jnp.bfloat16),
                        pltpu.VMEM((tm, D), jnp.float32), pltpu.SemaphoreType.DMA((2,))])
    return pl.pallas_call(
        functools.partial(_moe_kernel, tm=tm),
        out_shape=jax.ShapeDtypeStruct((nt * tm, D), jnp.float32),
        grid_spec=grid_spec,
        name="moe_experts",
        compiler_params=_cparams(("arbitrary", "arbitrary")),
    )(tile_expert, n_tiles, rows, rows, h, wgu, wgu, wd)


def _combine_kernel(rows0_ref, rows1_ref, ys_hbm, x_ref, gate_ref, info_ref, o_ref, buf, sem, *, tmc):
    i = pl.program_id(0)
    slot = i % 2
    n = 2 * tmc

    def start(rows_ref, s):
        def body(r, carry):
            pltpu.make_async_copy(ys_hbm.at[pl.ds(rows_ref[0, r], 1), :], buf.at[s, pl.ds(r, 1), :],
                                  sem.at[s]).start()
            return carry
        lax.fori_loop(0, n, body, 0, unroll=8)

    @pl.when(i == 0)
    def _():
        start(rows0_ref, 0)

    @pl.when(i + 1 < pl.num_programs(0))
    def _():
        start(rows1_ref, 1 - slot)

    pltpu.make_async_copy(buf.at[slot], buf.at[slot], sem.at[slot]).wait()
    info = info_ref[...]
    y = info[:, 2:3] * buf[slot, 0:tmc, :] + info[:, 3:4] * buf[slot, tmc:n, :]
    o_ref[...] = x_ref[...] + gate_ref[...] * y


def _moe_combine(x, ys, rows, info, mods, *, geom, tmc):
    nb, nct_rows, seq = geom
    t = x.shape[0]
    nct, lt = nct_rows // tmc, seq // tmc
    ntile = rows.shape[0]
    return pl.pallas_call(
        functools.partial(_combine_kernel, tmc=tmc),
        out_shape=jax.ShapeDtypeStruct((t, D), jnp.float32),
        grid=(ntile,),
        in_specs=[pl.BlockSpec((None, 1, 2 * tmc), lambda i: (i, 0, 0), memory_space=pltpu.SMEM),
                  pl.BlockSpec((None, 1, 2 * tmc), lambda i: (jnp.minimum(i + 1, ntile - 1), 0, 0),
                               memory_space=pltpu.SMEM),
                  pl.BlockSpec(memory_space=pl.ANY),
                  pl.BlockSpec((tmc, D), lambda i: (i, 0)),
                  _mod_spec(5, nct, lt, nb),
                  pl.BlockSpec((tmc, LANES), lambda i: (i, 0))],
        out_specs=pl.BlockSpec((tmc, D), lambda i: (i, 0)),
        scratch_shapes=[pltpu.VMEM((2, 2 * tmc, D), jnp.float32), pltpu.SemaphoreType.DMA((2,))],
        name="moe_combine",
        compiler_params=_cparams(("arbitrary",)),
    )(rows, rows, ys, x, mods, info)


def _moe(x, mods, router_pad, wgu, wd, *, geom, tm, tf, tmc):
    tr = x.shape[0]
    h, info = _router(x, mods, router_pad, geom=geom, tm=tm)
    e = info[:, 0:2].astype(jnp.int32).reshape(-1)
    onehot = (e[:, None] == jnp.arange(N_EXPERTS, dtype=jnp.int32)[None, :]).astype(jnp.int32)
    csum = jnp.cumsum(onehot, axis=0)
    rank = jnp.sum(onehot * (csum - 1), axis=1)
    counts = csum[-1]
    padded = ((counts + tm - 1) // tm) * tm
    ends = jnp.cumsum(padded)
    dest = (ends - padded)[e] + rank
    nt = 2 * tr // tm + N_EXPERTS
    tile_expert = jnp.minimum(jnp.sum(jnp.arange(nt, dtype=jnp.int32)[:, None] * tm >= ends[None, :], axis=1),
                              N_EXPERTS - 1).astype(jnp.int32)
    n_tiles = (ends[-1:] // tm).astype(jnp.int32)
    token_row = jnp.arange(2 * tr, dtype=jnp.int32) // 2
    src = jnp.zeros((nt * tm,), jnp.int32).at[dest].set(token_row)
    ys = _moe_experts(h, tile_expert, n_tiles, src.reshape(nt, 1, tm), wgu, wd, tm=tm, tf=tf)
    crow = dest.reshape(tr // tmc, tmc, 2).transpose(0, 2, 1).reshape(tr // tmc, 1, 2 * tmc)
    return _moe_combine(x, ys, crow, info, mods, geom=geom, tmc=tmc)


def kernel(x, c, ctx, c_ctx, w_mod, b_mod, hg_w_in, hg_lb_logits, hg_norm_w, hg_w_out, sw_w_qkv, sw_q_norm,
           sw_k_norm, sw_sink, sw_w_out, ff_w_gate_up, ff_w_down, moe_router, moe_w_gate_up, moe_w_down):
    nb, seq, _ = x.shape
    ctx_len = ctx.shape[1]
    depth = w_mod.shape[0]
    nct_rows = nb * ctx_len
    geom = (nb, nct_rows, seq)
    tm = 512
    tmb = 1024 if nct_rows % 1024 == 0 and seq % 1024 == 0 else tm
    bf = jnp.bfloat16

    xs = jnp.concatenate([ctx.reshape(nct_rows, D), x.reshape(nb * seq, D)], axis=0)
    cpad = jnp.concatenate([c, c_ctx[None, :], jnp.zeros((8 - nb - 1, D), jnp.float32)], axis=0)
    mods_all = _modvecs(cpad, w_mod, b_mod)

    p_lb = jax.nn.softmax(hg_lb_logits.astype(jnp.float32), axis=0)
    lower_bounds = jnp.cumsum(p_lb, axis=0) - p_lb[:1]

    for i in range(depth):
        ctx_live = i < depth - 1
        mods = mods_all[i]
        j = i // 2
        if i % 2 == 0:
            p = _mod_matmul(xs, mods, hg_w_in[j].astype(bf), k_shift=0, geom=geom, tm=tmb, tn=1024,
                            out_dtype=jnp.float32)
            of, ob = _hg_scan(p, lower_bounds[j, 0:1], lower_bounds[j, 1:2], geom)
            xs = _hg_out(of, ob, p, hg_norm_w[j][None, :], hg_w_out[j].astype(bf), xs, mods,
                         geom=geom, tm=tmb, tn=1024)
        else:
            wq, wk, wv = jnp.split(sw_w_qkv[j], [D, D + SW_KV * SW_DH], axis=1)
            rep = lambda w: jnp.repeat(w.reshape(D, SW_KV, 1, SW_DH), SW_GROUP, axis=2).reshape(D, D)
            w3 = jnp.concatenate([wq, rep(wk), rep(wv)], axis=1).astype(bf)
            nw2 = jnp.stack([jnp.tile(sw_q_norm[j], SW_HEADS), jnp.tile(sw_k_norm[j], SW_HEADS)])[:, None, :]
            qkv = _qkv_proj(xs, mods, w3, nw2, geom=geom, tm=tmb)
            o = _attention(qkv, sw_sink[j], geom, ctx_live)
            xs = _res_matmul(o, sw_w_out[j].astype(bf), xs, mods, geom=geom, tm=tmb, tn=1024)
            if not ctx_live:
                geom = (nb, 0, seq)
        if i % 2 == 0:
            xs = _ffn(xs, mods, ff_w_gate_up[j].astype(bf), ff_w_down[j].astype(bf), geom=geom, tm=tm, tf=1408)
        else:
            rpad = jnp.pad(moe_router[j], ((0, 0), (0, LANES - N_EXPERTS)))
            xs = _moe(xs, mods, rpad, moe_w_gate_up[j].astype(bf), moe_w_down[j].astype(bf),
                      geom=geom, tm=tm, tf=1792, tmc=256)
    return xs[xs.shape[0] - nb * seq:].reshape(nb, seq, D)
```

```python
import functools

import numpy as np
import jax
import jax.numpy as jnp
from jax import lax
from jax.experimental import pallas as pl
from jax.experimental.pallas import tpu as pltpu

D = 1024
EPS = 1e-6
NEG_BIG = -1e30
LOG2E = 1.4426950408889634
GRID_W = 64
ROPE_THETA = 10000.0

HG_HEADS = 8
HG_DK = 128
HG_FDIM = HG_HEADS * HG_DK
HG_CHUNK = 128
HG_LEVELS = 7
HG_ROWS = 256
HG_HB = 4

SW_HEADS = 16
SW_KV = 4
SW_GROUP = 4
SW_DH = 64
SW_BLOCK = 128

N_EXPERTS = 8
LANES = 128

VMEM_LIMIT = 56 * 1024 * 1024


def _cparams(sem):
    return pltpu.CompilerParams(dimension_semantics=sem, vmem_limit_bytes=VMEM_LIMIT)


def _sigmoid(x):
    return 1.0 / (1.0 + jnp.exp2(x * -LOG2E))


def _mod_row(i, nct, lt, nb):
    return jnp.where(i < nct, nb, (i - nct) // lt)


def _mod_spec(k, nct, lt, nb):
    return pl.BlockSpec((None, 1, D), lambda i, *_: (_mod_row(i, nct, lt, nb) * 6 + k, 0, 0))


def _modulate(x, shift, scale):
    ms = jnp.mean(x * x, axis=-1, keepdims=True)
    return (x * lax.rsqrt(ms + EPS)) * (1.0 + scale) + shift


def _modvec_kernel(c_ref, w_ref, b_ref, o_ref):
    c = c_ref[...]
    s = c * _sigmoid(c)
    o_ref[...] = jnp.dot(s, w_ref[...], precision=lax.Precision.HIGHEST,
                         preferred_element_type=jnp.float32) + b_ref[...]


def _modvecs(cpad, w_mod, b_mod):
    depth = w_mod.shape[0]
    tn = 1024
    out = pl.pallas_call(
        _modvec_kernel,
        out_shape=jax.ShapeDtypeStruct((depth, 8, 6 * D), jnp.float32),
        grid=(depth, 6 * D // tn),
        in_specs=[pl.BlockSpec((8, D), lambda l, j: (0, 0)),
                  pl.BlockSpec((None, D, tn), lambda l, j: (l, 0, j)),
                  pl.BlockSpec((None, 1, tn), lambda l, j: (l, 0, j))],
        out_specs=pl.BlockSpec((None, 8, tn), lambda l, j: (l, 0, j)),
        name="adaln_vectors",
        compiler_params=_cparams(("arbitrary", "arbitrary")),
    )(cpad, w_mod, b_mod.reshape(depth, 1, 6 * D))
    return out.reshape(depth, 8 * 6, 1, D)


def _modmm_kernel(x_ref, sh_ref, sc_ref, w_ref, o_ref, h_ref):
    @pl.when(pl.program_id(1) == 0)
    def _():
        h_ref[...] = _modulate(x_ref[...], sh_ref[...], sc_ref[...]).astype(jnp.bfloat16)

    o_ref[...] = jnp.dot(h_ref[...], w_ref[...], preferred_element_type=jnp.float32).astype(o_ref.dtype)


def _mod_matmul(x, mods, w, *, k_shift, geom, tm, tn, out_dtype):
    nb, nct_rows, seq = geom
    t, n = x.shape[0], w.shape[1]
    nct, lt = nct_rows // tm, seq // tm
    return pl.pallas_call(
        _modmm_kernel,
        out_shape=jax.ShapeDtypeStruct((t, n), out_dtype),
        grid=(t // tm, n // tn),
        in_specs=[pl.BlockSpec((tm, D), lambda i, j: (i, 0)),
                  _mod_spec(k_shift, nct, lt, nb),
                  _mod_spec(k_shift + 1, nct, lt, nb),
                  pl.BlockSpec((D, tn), lambda i, j: (0, j))],
        out_specs=pl.BlockSpec((tm, tn), lambda i, j: (i, j)),
        scratch_shapes=[pltpu.VMEM((tm, D), jnp.bfloat16)],
        name="hg_in_proj",
        compiler_params=_cparams(("arbitrary", "arbitrary")),
    )(x, mods, mods, w)


def _hg_tables(fwd):
    c = HG_CHUNK
    t = np.arange(c)[:, None]
    u = np.arange(c)[None, :]
    reach = (u <= t) if fwd else (u >= t)
    x = t ^ u
    lvl = np.where(x > 0, np.floor(np.log2(np.maximum(x, 1))), HG_LEVELS)
    lvl = np.where(reach, lvl, -1).astype(np.int32)
    tri = np.concatenate([reach.astype(np.float32)] * 3, axis=1)
    return tri, lvl


def _hg_gates(q_raw, z, lb, tri_ref, a_ref):
    bf = jnp.bfloat16
    q = q_raw * _sigmoid(q_raw)
    sig = _sigmoid(z)
    fc = jnp.maximum(lb + (1.0 - lb) * sig, 1e-30)
    lf = jnp.log(fc) * LOG2E
    k = (1.0 - lb) * (1.0 - sig)
    hi = lf.astype(bf)
    rest = lf - hi.astype(jnp.float32)
    mid = rest.astype(bf)
    lo = (rest - mid.astype(jnp.float32)).astype(bf)
    a = jnp.dot(tri_ref[...], jnp.concatenate([hi, mid, lo], axis=0), preferred_element_type=jnp.float32)
    a_ref[...] = a
    return q, k, fc, a


def _hg_scores(q, k, fc, a, a_ref, lvl, fwd):
    c = HG_CHUNK
    bf = jnp.bfloat16

    def rows_of(row, n):
        return jnp.broadcast_to(a_ref[row:row + 1, :], (n, HG_DK))

    rowi = lax.broadcasted_iota(jnp.int32, (c, HG_DK), 0)
    nt = (((1,), (1,)), ((), ()))
    zeros8 = jnp.zeros((8, HG_DK), jnp.float32)

    def level_operands(l):
        w = 1 << l
        near = w - 1 if fwd else w
        if w < 8:
            upper = ((rowi >> l) & 1) == 1
            q_side = upper if fwd else jnp.logical_not(upper)
            if l == 0:
                qe, ke = q * fc, k
            else:
                if l == 1:
                    first = jnp.concatenate([rows_of(8 * g + near, 8) for g in range(c // 8)], axis=0)
                    second = jnp.concatenate([rows_of(8 * g + 4 + near, 8) for g in range(c // 8)], axis=0)
                    bnd = jnp.where((rowi & 4) == 0, first, second)
                else:
                    bnd = jnp.concatenate([rows_of(8 * g + near, 8) for g in range(c // 8)], axis=0)
                e = jnp.exp2(-jnp.abs(a - bnd))
                qe, ke = q * e, k * e
            return jnp.where(q_side, qe, 0.0).astype(bf), jnp.where(q_side, 0.0, ke).astype(bf)
        qp, kp = [], []
        for j in range(c // w):
            rs = slice(j * w, (j + 1) * w)
            bnd = rows_of((j // 2) * 2 * w + near, w)
            zero = jnp.concatenate([zeros8] * (w // 8), axis=0)
            if (j % 2 == 1) == fwd:
                qp.append(q[rs] * jnp.exp2(a[rs] - bnd))
                kp.append(zero)
            else:
                qp.append(zero)
                kp.append(k[rs] * jnp.exp2(bnd - a[rs]))
        return jnp.concatenate(qp, axis=0).astype(bf), jnp.concatenate(kp, axis=0).astype(bf)

    scores = jnp.where(lvl == HG_LEVELS,
                       lax.dot_general(q.astype(bf), k.astype(bf), nt, preferred_element_type=jnp.float32), 0.0)
    for l in range(HG_LEVELS):
        qm, km = level_operands(l)
        sc = lax.dot_general(qm, km, nt, preferred_element_type=jnp.float32)
        scores = jnp.where(lvl == l, sc, scores)
    return scores.astype(bf)


def _hg_finish(q, k, v, a, scores, st_ref, fwd):
    c = HG_CHUNK
    bf = jnp.bfloat16
    nt = (((1,), (1,)), ((), ()))
    last = c - 1 if fwd else 0
    e_in = jnp.exp2(a)
    e_out = jnp.exp2(a[last:last + 1, :] - a)
    st = st_ref[...]
    o = (jnp.dot(scores, v.astype(bf), preferred_element_type=jnp.float32)
         + lax.dot_general((q * e_in).astype(bf), st.astype(bf), nt, preferred_element_type=jnp.float32))
    st_ref[...] = (e_in[last:last + 1, :] * st
                   + jnp.dot(v.T.astype(bf), (k * e_out).astype(bf), preferred_element_type=jnp.float32))
    return o


def _hg_scan_kernel(qf_ref, vf_ref, zf_ref, qb_ref, vb_ref, zb_ref, lbf_ref, lbb_ref, trif_ref, trib_ref,
                    lvlf_ref, lvlb_ref, of_ref, ob_ref, sf_ref, sb_ref, a_ref):
    @pl.when(pl.program_id(2) == 0)
    def _():
        sf_ref[...] = jnp.zeros_like(sf_ref)
        sb_ref[...] = jnp.zeros_like(sb_ref)

    c = HG_CHUNK
    nch = HG_ROWS // c
    lvlf, lvlb = lvlf_ref[...], lvlb_ref[...]
    units = []
    for hh in range(HG_HB):
        cs = slice(hh * HG_DK, (hh + 1) * HG_DK)
        for step, ci in enumerate(range(nch)):
            units.append(dict(step=step, rows=slice(ci * c, (ci + 1) * c), cols=cs, fwd=True, q=qf_ref, v=vf_ref,
                              z=zf_ref, lb=lbf_ref, tri=trif_ref, lvl=lvlf, st=sf_ref.at[hh], out=of_ref))
        for step, ci in enumerate(reversed(range(nch))):
            units.append(dict(step=step, rows=slice(ci * c, (ci + 1) * c), cols=cs, fwd=False, q=qb_ref, v=vb_ref,
                              z=zb_ref, lb=lbb_ref, tri=trib_ref, lvl=lvlb, st=sb_ref.at[hh], out=ob_ref))
    for n, u in enumerate(units):
        u["a_ref"] = a_ref.at[n]
        u["q"], u["k"], u["fc"], u["a"] = _hg_gates(u["q"][u["rows"], u["cols"]], u["z"][u["rows"], u["cols"]],
                                                    u["lb"][:, u["cols"]], u["tri"], u["a_ref"])
    for u in units:
        u["scores"] = _hg_scores(u["q"], u["k"], u["fc"], u["a"], u["a_ref"], u["lvl"], u["fwd"])
    for step in range(nch):
        for u in units:
            if u["step"] == step:
                o = _hg_finish(u["q"], u["k"], u["v"][u["rows"], u["cols"]], u["a"], u["scores"], u["st"], u["fwd"])
                u["out"][u["rows"], u["cols"]] = o.astype(u["out"].dtype)


def _hg_scan(p, lbf, lbb, geom):
    nb, nct_rows, seq = geom
    t = p.shape[0]
    r = HG_ROWS
    cb, lb_ = (nct_rows // nb) // r, seq // r
    nsteps = cb + lb_
    lat0 = nct_rows // r

    def fblk(b, s):
        return jnp.where(s < cb, b * cb + s, lat0 + b * lb_ + (s - cb))

    def bblk(b, s):
        return jnp.where(s < cb, b * cb + (cb - 1 - s), lat0 + b * lb_ + (lb_ - 1 - (s - cb)))

    wcol = HG_HB * HG_DK
    ngrp = HG_HEADS // HG_HB
    units = 2 * HG_HB * (r // HG_CHUNK)

    def spec(blk, colblock):
        return pl.BlockSpec((r, wcol), lambda b, h, s: (blk(b, s), colblock * ngrp + h))

    lbspec = pl.BlockSpec((1, wcol), lambda b, h, s: (0, h))
    const = lambda shape: pl.BlockSpec(shape, lambda b, h, s: (0, 0))
    trif, lvlf = _hg_tables(True)
    trib, lvlb = _hg_tables(False)
    return pl.pallas_call(
        _hg_scan_kernel,
        out_shape=(jax.ShapeDtypeStruct((t, D), jnp.bfloat16), jax.ShapeDtypeStruct((t, D), jnp.bfloat16)),
        grid=(nb, ngrp, nsteps),
        in_specs=[spec(fblk, 0), spec(fblk, 1), spec(fblk, 2),
                  spec(bblk, 0), spec(bblk, 1), spec(bblk, 3),
                  lbspec, lbspec, const(trif.shape), const(trib.shape), const(lvlf.shape), const(lvlb.shape)],
        out_specs=(pl.BlockSpec((r, wcol), lambda b, h, s: (fblk(b, s), h)),
                   pl.BlockSpec((r, wcol), lambda b, h, s: (bblk(b, s), h))),
        scratch_shapes=[pltpu.VMEM((HG_HB, HG_DK, HG_DK), jnp.float32),
                        pltpu.VMEM((HG_HB, HG_DK, HG_DK), jnp.float32),
                        pltpu.VMEM((units, HG_CHUNK, HG_DK), jnp.float32)],
        name="hg_scan",
        compiler_params=_cparams(("arbitrary", "arbitrary", "arbitrary")),
    )(p, p, p, p, p, p, lbf, lbb, jnp.asarray(trif, jnp.bfloat16), jnp.asarray(trib, jnp.bfloat16),
      jnp.asarray(lvlf), jnp.asarray(lvlb))


def _hg_out_kernel(of_ref, ob_ref, g_ref, nw_ref, w_ref, x_ref, gate_ref, o_ref, h_ref):
    @pl.when(pl.program_id(1) == 0)
    def _():
        nw = nw_ref[...]
        for h in range(HG_HEADS):
            cs = slice(h * HG_DK, (h + 1) * HG_DK)
            o = of_ref[:, cs].astype(jnp.float32) + ob_ref[:, cs].astype(jnp.float32)
            y = o * lax.rsqrt(jnp.mean(o * o, axis=-1, keepdims=True) + EPS) * nw
            g = g_ref[:, cs]
            h_ref[:, cs] = (y * (g * _sigmoid(g))).astype(jnp.bfloat16)

    acc = jnp.dot(h_ref[...], w_ref[...], preferred_element_type=jnp.float32)
    o_ref[...] = x_ref[...] + gate_ref[...] * acc


def _hg_out(of, ob, p, norm_w, w, x, mods, *, geom, tm, tn):
    nb, nct_rows, seq = geom
    t = x.shape[0]
    nct, lt = nct_rows // tm, seq // tm
    gate = pl.BlockSpec((None, 1, tn), lambda i, j: (_mod_row(i, nct, lt, nb) * 6 + 2, 0, j))
    return pl.pallas_call(
        _hg_out_kernel,
        out_shape=jax.ShapeDtypeStruct((t, D), jnp.float32),
        grid=(t // tm, D // tn),
        in_specs=[pl.BlockSpec((tm, D), lambda i, j: (i, 0)),
                  pl.BlockSpec((tm, D), lambda i, j: (i, 0)),
                  pl.BlockSpec((tm, D), lambda i, j: (i, 4)),
                  pl.BlockSpec((1, HG_DK), lambda i, j: (0, 0)),
                  pl.BlockSpec((D, tn), lambda i, j: (0, j)),
                  pl.BlockSpec((tm, tn), lambda i, j: (i, j)),
                  gate],
        out_specs=pl.BlockSpec((tm, tn), lambda i, j: (i, j)),
        scratch_shapes=[pltpu.VMEM((tm, D), jnp.bfloat16)],
        name="hg_out_proj",
        compiler_params=_cparams(("arbitrary", "arbitrary")),
    )(of, ob, p, norm_w, w, x, mods)


def _resmm_kernel(a_ref, w_ref, x_ref, gate_ref, o_ref):
    acc = jnp.dot(a_ref[...], w_ref[...], preferred_element_type=jnp.float32)
    o_ref[...] = x_ref[...] + gate_ref[...] * acc


def _res_matmul(a, w, x, mods, *, geom, tm, tn):
    nb, nct_rows, seq = geom
    rows = a.shape[0]
    t0 = (x.shape[0] - rows) // tm
    nct, lt = nct_rows // tm, seq // tm
    gate = pl.BlockSpec((None, 1, tn), lambda i, j: (_mod_row(i + t0, nct, lt, nb) * 6 + 2, 0, j))
    return pl.pallas_call(
        _resmm_kernel,
        out_shape=jax.ShapeDtypeStruct((rows, D), jnp.float32),
        grid=(rows // tm, D // tn),
        in_specs=[pl.BlockSpec((tm, a.shape[1]), lambda i, j: (i, 0)),
                  pl.BlockSpec((a.shape[1], tn), lambda i, j: (0, j)),
                  pl.BlockSpec((tm, tn), lambda i, j: (i + t0, j)),
                  gate],
        out_specs=pl.BlockSpec((tm, tn), lambda i, j: (i, j)),
        name="attn_out_proj",
        compiler_params=_cparams(("arbitrary", "arbitrary")),
    )(a, w, x, mods)


def _rope_tables(seq, tm):
    rows = seq // GRID_W
    row = np.repeat(np.arange(rows, dtype=np.float32), GRID_W)
    col = np.tile(np.arange(GRID_W, dtype=np.float32), rows)
    nf = SW_DH // 4
    inv = (ROPE_THETA ** (-np.arange(nf, dtype=np.float32) / nf)).astype(np.float32)
    ang_r = row[:, None] * inv
    ang_c = col[:, None] * inv
    cos = np.concatenate([np.cos(ang_r), np.cos(ang_r), np.cos(ang_c), np.cos(ang_c)], axis=1)
    sin = np.concatenate([-np.sin(ang_r), np.sin(ang_r), -np.sin(ang_c), np.sin(ang_c)], axis=1)
    cos = np.concatenate([np.tile(cos, (1, 2)), np.ones((tm, LANES), np.float32)], axis=0)
    sin = np.concatenate([np.tile(sin, (1, 2)), np.zeros((tm, LANES), np.float32)], axis=0)
    return jnp.asarray(cos, jnp.float32), jnp.asarray(sin, jnp.float32)


def _qkv_kernel(x_ref, sh_ref, sc_ref, w_ref, nw_ref, ones_ref, cos_ref, sin_ref, o_ref, h_ref):
    j = pl.program_id(1)

    @pl.when(j == 0)
    def _():
        h_ref[...] = _modulate(x_ref[...], sh_ref[...], sc_ref[...]).astype(jnp.bfloat16)

    acc = jnp.dot(h_ref[...], w_ref[...], preferred_element_type=jnp.float32)

    @pl.when(j == 2)
    def _():
        o_ref[...] = acc.astype(o_ref.dtype)

    @pl.when(j < 2)
    def _():
        ss = jnp.dot((acc * acc).astype(jnp.bfloat16), ones_ref[...], preferred_element_type=jnp.float32)
        y = acc * lax.rsqrt(ss * (1.0 / SW_DH) + EPS) * nw_ref[...]
        cos = cos_ref[...]
        sin = sin_ref[...]
        lane = lax.broadcasted_iota(jnp.int32, (y.shape[0], LANES), 1)
        first = (lane & 16) == 0
        scale = jnp.where(j == 0, SW_DH ** -0.5 * LOG2E, 1.0)
        for g in range(D // LANES):
            cs = slice(g * LANES, (g + 1) * LANES)
            yg = y[:, cs]
            partner = jnp.where(first, pltpu.roll(yg, LANES - 16, 1), pltpu.roll(yg, 16, 1))
            o_ref[:, cs] = ((yg * cos + partner * sin) * scale).astype(o_ref.dtype)


def _qkv_proj(x, mods, w3, nw2, *, geom, tm):
    nb, nct_rows, seq = geom
    t = x.shape[0]
    nct, lt = nct_rows // tm, seq // tm
    cos, sin = _rope_tables(seq, tm)
    blk = np.kron(np.eye(D // SW_DH, dtype=np.float32), np.ones((SW_DH, SW_DH), np.float32))
    ones = jnp.asarray(blk, jnp.bfloat16)
    tab = lambda i, j: (jnp.where(i < nct, lt, (i - nct) % lt), 0)
    return pl.pallas_call(
        _qkv_kernel,
        out_shape=jax.ShapeDtypeStruct((t, 3 * D), jnp.bfloat16),
        grid=(t // tm, 3),
        in_specs=[pl.BlockSpec((tm, D), lambda i, j: (i, 0)),
                  _mod_spec(0, nct, lt, nb),
                  _mod_spec(1, nct, lt, nb),
                  pl.BlockSpec((D, D), lambda i, j: (0, j)),
                  pl.BlockSpec((None, 1, D), lambda i, j: (jnp.minimum(j, 1), 0, 0)),
                  pl.BlockSpec((D, D), lambda i, j: (0, 0)),
                  pl.BlockSpec((tm, LANES), tab),
                  pl.BlockSpec((tm, LANES), tab)],
        out_specs=pl.BlockSpec((tm, D), lambda i, j: (i, j)),
        scratch_shapes=[pltpu.VMEM((tm, D), jnp.bfloat16)],
        name="qkv_proj",
        compiler_params=_cparams(("arbitrary", "arbitrary")),
    )(x, mods, mods, w3, nw2, ones, cos, sin)


def _attn_items(items, sink_ref, kv, o_ref):
    nt = (((1,), (1,)), ((), ()))
    work = [(it, g) for it in items for g in range(SW_GROUP)]
    scores = []
    for (_, q, kcat, _, _), g in work:
        grp = lax.broadcasted_iota(jnp.int32, q.shape, 1) // SW_DH
        scores.append(lax.dot_general(jnp.where(grp == g, q, jnp.zeros_like(q)), kcat, nt,
                                      preferred_element_type=jnp.float32))
    probs, denoms = [], []
    for ((_, _, _, _, valid), g), s in zip(work, scores):
        sink = sink_ref[kv * SW_GROUP + g] * LOG2E
        if valid is not None:
            s = jnp.where(valid, s, NEG_BIG)
        m = jnp.maximum(jnp.max(s, axis=-1, keepdims=True), sink)
        p = jnp.exp2(s - m)
        denoms.append(jnp.sum(p, axis=-1, keepdims=True) + jnp.exp2(sink - m))
        probs.append(p.astype(jnp.bfloat16))
    outs = {}
    for n, ((rows, q, _, vcat, _), g) in enumerate(work):
        grp = lax.broadcasted_iota(jnp.int32, q.shape, 1) // SW_DH
        og = jnp.dot(probs[n], vcat, preferred_element_type=jnp.float32) / denoms[n]
        outs[rows] = og if g == 0 else jnp.where(grp == g, og, outs[rows])
    for rows, out in outs.items():
        o_ref[rows[0]:rows[1], :] = out.astype(o_ref.dtype)


def _attn_kernel(sink_ref, q_ref, kp_ref, kc_ref, kn_ref, kx_ref, vp_ref, vc_ref, vn_ref, vx_ref, o_ref,
                 *, nstep, ctx_out):
    kv, j = pl.program_id(1), pl.program_id(2)
    blk = SW_BLOCK

    def latent():
        kx, vx = kx_ref[...], vx_ref[...]
        k0, k1, v0, v1 = kc_ref[0:blk, :], kc_ref[blk:2 * blk, :], vc_ref[0:blk, :], vc_ref[blk:2 * blk, :]
        ns = 3 * blk + kx.shape[0]
        t = lax.broadcasted_iota(jnp.int32, (blk, ns), 0)
        s = lax.broadcasted_iota(jnp.int32, (blk, ns), 1)

        def window(has_prev, has_next):
            lo = jnp.maximum(t, jnp.where(has_prev, 0, blk))
            hi = jnp.minimum(t + 2 * blk, jnp.where(has_next, 3 * blk - 1, 2 * blk - 1))
            return ((s >= lo) & (s <= hi)) | (s >= 3 * blk)

        items = [((0, blk), q_ref[0:blk, :], jnp.concatenate([kp_ref[...], k0, k1, kx], axis=0),
                  jnp.concatenate([vp_ref[...], v0, v1, vx], axis=0), window(j > 0, True)),
                 ((blk, 2 * blk), q_ref[blk:2 * blk, :], jnp.concatenate([k0, k1, kn_ref[...], kx], axis=0),
                  jnp.concatenate([v0, v1, vn_ref[...], vx], axis=0), window(True, j < nstep - 1))]
        _attn_items(items, sink_ref, kv, o_ref)

    if not ctx_out:
        latent()
        return
    pl.when(j < nstep)(latent)

    @pl.when(j >= nstep)
    def _():
        _attn_items([((0, 2 * blk), q_ref[...], kx_ref[...], vx_ref[...], None)], sink_ref, kv, o_ref)


def _attention(qkv, sink, geom, ctx_out):
    nb, nct_rows, seq = geom
    ctx_len = nct_rows // nb
    pair = 2 * SW_BLOCK
    assert ctx_len == pair and seq % pair == 0
    nblk = seq // SW_BLOCK
    nstep = seq // pair
    lat0 = nct_rows // SW_BLOCK
    w = SW_GROUP * SW_DH
    smem = pl.BlockSpec(memory_space=pltpu.SMEM)

    def qrow(b, j):
        return jnp.where(j < nstep, nct_rows // pair + b * nstep + j, b)

    def edge(off, colblock):
        return pl.BlockSpec((SW_BLOCK, w), lambda b, kv, j: (
            lat0 + b * nblk + jnp.clip(2 * jnp.minimum(j, nstep - 1) + off, 0, nblk - 1), colblock * SW_KV + kv))

    def center(colblock):
        return pl.BlockSpec((pair, w), lambda b, kv, j: (qrow(b, jnp.minimum(j, nstep - 1)), colblock * SW_KV + kv))

    def ctxkv(colblock):
        return pl.BlockSpec((ctx_len, w), lambda b, kv, j: (b, colblock * SW_KV + kv))

    if ctx_out:
        steps, out_rows = nstep + 1, qkv.shape[0]
        out_spec = pl.BlockSpec((pair, w), lambda b, kv, j: (qrow(b, j), kv))
    else:
        steps, out_rows = nstep, nb * seq
        out_spec = pl.BlockSpec((pair, w), lambda b, kv, j: (b * nstep + j, kv))
    return pl.pallas_call(
        functools.partial(_attn_kernel, nstep=nstep, ctx_out=ctx_out),
        out_shape=jax.ShapeDtypeStruct((out_rows, D), jnp.bfloat16),
        grid=(nb, SW_KV, steps),
        in_specs=[smem, pl.BlockSpec((pair, w), lambda b, kv, j: (qrow(b, j), kv)),
                  edge(-1, 1), center(1), edge(2, 1), ctxkv(1),
                  edge(-1, 2), center(2), edge(2, 2), ctxkv(2)],
        out_specs=out_spec,
        name="attention",
        compiler_params=_cparams(("arbitrary", "arbitrary", "arbitrary")),
    )(sink, qkv, qkv, qkv, qkv, qkv, qkv, qkv, qkv, qkv)


def _router_kernel(x_ref, sh_ref, sc_ref, r_ref, h_ref, info_ref):
    h = _modulate(x_ref[...], sh_ref[...], sc_ref[...])
    h_ref[...] = h
    logits = jnp.dot(h, r_ref[...], precision=lax.Precision.HIGHEST, preferred_element_type=jnp.float32)
    lane = lax.broadcasted_iota(jnp.int32, logits.shape, 1).astype(jnp.float32)
    logits = jnp.where(lane < N_EXPERTS, logits, -jnp.inf)
    l1 = jnp.max(logits, axis=-1, keepdims=True)
    i1 = jnp.min(jnp.where(logits == l1, lane, float(LANES)), axis=-1, keepdims=True)
    rest = jnp.where(lane == i1, -jnp.inf, logits)
    l2 = jnp.max(rest, axis=-1, keepdims=True)
    i2 = jnp.min(jnp.where(rest == l2, lane, float(LANES)), axis=-1, keepdims=True)
    w1 = 1.0 / (1.0 + jnp.exp(l2 - l1))
    info_ref[...] = jnp.where(lane == 0.0, i1, jnp.where(lane == 1.0, i2, jnp.where(lane == 2.0, w1, 1.0 - w1)))


def _router(x, mods, router_pad, *, geom, tm):
    nb, nct_rows, seq = geom
    t = x.shape[0]
    nct, lt = nct_rows // tm, seq // tm
    return pl.pallas_call(
        _router_kernel,
        out_shape=(jax.ShapeDtypeStruct((t, D), jnp.float32), jax.ShapeDtypeStruct((t, LANES), jnp.float32)),
        grid=(t // tm,),
        in_specs=[pl.BlockSpec((tm, D), lambda i: (i, 0)), _mod_spec(3, nct, lt, nb), _mod_spec(4, nct, lt, nb),
                  pl.BlockSpec((D, LANES), lambda i: (0, 0))],
        out_specs=(pl.BlockSpec((tm, D), lambda i: (i, 0)), pl.BlockSpec((tm, LANES), lambda i: (i, 0))),
        name="moe_router",
        compiler_params=_cparams(("arbitrary",)),
    )(x, mods, mods, router_pad)


def _swiglu_step(h, wg_ref, wu_ref, wd_ref, row_scale):
    g = jnp.dot(h, wg_ref[...], preferred_element_type=jnp.float32)
    u = jnp.dot(h, wu_ref[...], preferred_element_type=jnp.float32)
    a = g * _sigmoid(g) * u
    if row_scale is not None:
        a = a * row_scale
    return jnp.dot(a.astype(jnp.bfloat16), wd_ref[...], preferred_element_type=jnp.float32)


def _ffn_kernel(x_ref, sh_ref, sc_ref, gate_ref, wg_ref, wu_ref, wd_ref, o_ref, h_ref, acc_ref, *, nf):
    f = pl.program_id(1)

    @pl.when(f == 0)
    def _():
        h_ref[...] = _modulate(x_ref[...], sh_ref[...], sc_ref[...]).astype(jnp.bfloat16)

    y = _swiglu_step(h_ref[...], wg_ref, wu_ref, wd_ref, None)
    if nf == 1:
        o_ref[...] = x_ref[...] + gate_ref[...] * y
        return

    @pl.when(f == 0)
    def _():
        acc_ref[...] = y

    @pl.when((f > 0) & (f < nf - 1))
    def _():
        acc_ref[...] += y

    @pl.when(f == nf - 1)
    def _():
        o_ref[...] = x_ref[...] + gate_ref[...] * (acc_ref[...] + y)


def _ffn(x, mods, wgu, wd, *, geom, tm, tf):
    nb, nct_rows, seq = geom
    t = x.shape[0]
    nf = wd.shape[0] // tf
    nct, lt = nct_rows // tm, seq // tm
    return pl.pallas_call(
        functools.partial(_ffn_kernel, nf=nf),
        out_shape=jax.ShapeDtypeStruct((t, D), jnp.float32),
        grid=(t // tm, nf),
        in_specs=[pl.BlockSpec((tm, D), lambda i, f: (i, 0)),
                  _mod_spec(3, nct, lt, nb),
                  _mod_spec(4, nct, lt, nb),
                  _mod_spec(5, nct, lt, nb),
                  pl.BlockSpec((D, tf), lambda i, f: (0, f)),
                  pl.BlockSpec((D, tf), lambda i, f: (0, nf + f)),
                  pl.BlockSpec((tf, D), lambda i, f: (f, 0))],
        out_specs=pl.BlockSpec((tm, D), lambda i, f: (i, 0)),
        scratch_shapes=[pltpu.VMEM((tm, D), jnp.bfloat16), pltpu.VMEM((tm, D), jnp.float32)],
        name="dense_ffn",
        compiler_params=_cparams(("arbitrary", "arbitrary")),
    )(x, mods, mods, mods, wgu, wgu, wd)


def _row_gather_start(src_hbm, rows_ref, buf, sem, first, count):
    for r in range(count):
        pltpu.make_async_copy(src_hbm.at[pl.ds(rows_ref[0, first + r], 1), :],
                              buf.at[pl.ds(first + r, 1), :], sem).start()


def _moe_kernel(te_ref, nt_ref, rows0_ref, rows1_ref, h_hbm, wg_ref, wu_ref, wd_ref, o_ref,
                hbuf, hb_ref, acc_ref, sem, *, tm):
    del te_ref
    i, f = pl.program_id(0), pl.program_id(1)
    last_i, last_f = pl.num_programs(0) - 1, pl.num_programs(1) - 1
    slot = i % 2
    nt = nt_ref[0]
    wait_all = lambda s: pltpu.make_async_copy(hbuf.at[s], hbuf.at[s], sem.at[s]).wait()

    @pl.when((i == 0) & (f == 0))
    def _():
        _row_gather_start(h_hbm, rows0_ref, hbuf.at[0], sem.at[0], 0, tm)

    @pl.when((f == 0) & (i <= nt))
    def _():
        wait_all(slot)
        hb_ref[...] = hbuf[slot].astype(jnp.bfloat16)

    @pl.when((i < nt) & (f == 0))
    def _():
        _row_gather_start(h_hbm, rows1_ref, hbuf.at[1 - slot], sem.at[1 - slot], 0, tm)
        acc_ref[...] = _swiglu_step(hb_ref[...], wg_ref, wu_ref, wd_ref, None)

    @pl.when((i < nt) & (f > 0))
    def _():
        acc_ref[...] += _swiglu_step(hb_ref[...], wg_ref, wu_ref, wd_ref, None)

    @pl.when(f == last_f)
    def _():
        o_ref[...] = acc_ref[...]

    @pl.when((i == last_i) & (f == last_f) & (i < nt))
    def _():
        wait_all(1 - slot)


def _moe_experts(h, tile_expert, n_tiles, rows, wgu, wd, *, tm, tf):
    nt = rows.shape[0]
    nf = wd.shape[1] // tf
    grid_spec = pltpu.PrefetchScalarGridSpec(
        num_scalar_prefetch=2,
        grid=(nt, nf),
        in_specs=[pl.BlockSpec((None, 1, tm), lambda i, f, te, n: (i, 0, 0), memory_space=pltpu.SMEM),
                  pl.BlockSpec((None, 1, tm), lambda i, f, te, n: (jnp.minimum(i + 1, nt - 1), 0, 0),
                               memory_space=pltpu.SMEM),
                  pl.BlockSpec(memory_space=pl.ANY),
                  pl.BlockSpec((None, D, tf), lambda i, f, te, n: (te[i], 0, f)),
                  pl.BlockSpec((None, D, tf), lambda i, f, te, n: (te[i], 0, nf + f)),
                  pl.BlockSpec((None, tf, D), lambda i, f, te, n: (te[i], f, 0))],
        out_specs=pl.BlockSpec((tm, D), lambda i, f, te, n: (i, 0)),
        scratch_shapes=[pltpu.VMEM((2, tm, D), jnp.float32), pltpu.VMEM((tm, D), jnp.bfloat16),
                        pltpu.VMEM((tm, D), jnp.float32), pltpu.SemaphoreType.DMA((2,))])
    return pl.pallas_call(
        functools.partial(_moe_kernel, tm=tm),
        out_shape=jax.ShapeDtypeStruct((nt * tm, D), jnp.float32),
        grid_spec=grid_spec,
        name="moe_experts",
        compiler_params=_cparams(("arbitrary", "arbitrary")),
    )(tile_expert, n_tiles, rows, rows, h, wgu, wgu, wd)


def _combine_kernel(rows0_ref, rows1_ref, ys_hbm, x_ref, gate_ref, info_ref, o_ref, buf, sem, *, tmc):
    i = pl.program_id(0)
    slot = i % 2
    n = 2 * tmc

    def start(rows_ref, s):
        def body(r, carry):
            pltpu.make_async_copy(ys_hbm.at[pl.ds(rows_ref[0, r], 1), :], buf.at[s, pl.ds(r, 1), :],
                                  sem.at[s]).start()
            return carry
        lax.fori_loop(0, n, body, 0, unroll=8)

    @pl.when(i == 0)
    def _():
        start(rows0_ref, 0)

    @pl.when(i + 1 < pl.num_programs(0))
    def _():
        start(rows1_ref, 1 - slot)

    pltpu.make_async_copy(buf.at[slot], buf.at[slot], sem.at[slot]).wait()
    info = info_ref[...]
    y = info[:, 2:3] * buf[slot, 0:tmc, :] + info[:, 3:4] * buf[slot, tmc:n, :]
    o_ref[...] = x_ref[...] + gate_ref[...] * y


def _moe_combine(x, ys, rows, info, mods, *, geom, tmc):
    nb, nct_rows, seq = geom
    t = x.shape[0]
    nct, lt = nct_rows // tmc, seq // tmc
    ntile = rows.shape[0]
    return pl.pallas_call(
        functools.partial(_combine_kernel, tmc=tmc),
        out_shape=jax.ShapeDtypeStruct((t, D), jnp.float32),
        grid=(ntile,),
        in_specs=[pl.BlockSpec((None, 1, 2 * tmc), lambda i: (i, 0, 0), memory_space=pltpu.SMEM),
                  pl.BlockSpec((None, 1, 2 * tmc), lambda i: (jnp.minimum(i + 1, ntile - 1), 0, 0),
                               memory_space=pltpu.SMEM),
                  pl.BlockSpec(memory_space=pl.ANY),
                  pl.BlockSpec((tmc, D), lambda i: (i, 0)),
                  _mod_spec(5, nct, lt, nb),
                  pl.BlockSpec((tmc, LANES), lambda i: (i, 0))],
        out_specs=pl.BlockSpec((tmc, D), lambda i: (i, 0)),
        scratch_shapes=[pltpu.VMEM((2, 2 * tmc, D), jnp.float32), pltpu.SemaphoreType.DMA((2,))],
        name="moe_combine",
        compiler_params=_cparams(("arbitrary",)),
    )(rows, rows, ys, x, mods, info)


def _moe(x, mods, router_pad, wgu, wd, *, geom, tm, tf, tmc):
    tr = x.shape[0]
    h, info = _router(x, mods, router_pad, geom=geom, tm=tm)
    e = info[:, 0:2].astype(jnp.int32).reshape(-1)
    onehot = (e[:, None] == jnp.arange(N_EXPERTS, dtype=jnp.int32)[None, :]).astype(jnp.int32)
    csum = jnp.cumsum(onehot, axis=0)
    rank = jnp.sum(onehot * (csum - 1), axis=1)
    counts = csum[-1]
    padded = ((counts + tm - 1) // tm) * tm
    ends = jnp.cumsum(padded)
    dest = (ends - padded)[e] + rank
    nt = 2 * tr // tm + N_EXPERTS
    tile_expert = jnp.minimum(jnp.sum(jnp.arange(nt, dtype=jnp.int32)[:, None] * tm >= ends[None, :], axis=1),
                              N_EXPERTS - 1).astype(jnp.int32)
    n_tiles = (ends[-1:] // tm).astype(jnp.int32)
    token_row = jnp.arange(2 * tr, dtype=jnp.int32) // 2
    src = jnp.zeros((nt * tm,), jnp.int32).at[dest].set(token_row)
    ys = _moe_experts(h, tile_expert, n_tiles, src.reshape(nt, 1, tm), wgu, wd, tm=tm, tf=tf)
    crow = dest.reshape(tr // tmc, tmc, 2).transpose(0, 2, 1).reshape(tr // tmc, 1, 2 * tmc)
    return _moe_combine(x, ys, crow, info, mods, geom=geom, tmc=tmc)


def kernel(x, c, ctx, c_ctx, w_mod, b_mod, hg_w_in, hg_lb_logits, hg_norm_w, hg_w_out, sw_w_qkv, sw_q_norm,
           sw_k_norm, sw_sink, sw_w_out, ff_w_gate_up, ff_w_down, moe_router, moe_w_gate_up, moe_w_down):
    nb, seq, _ = x.shape
    ctx_len = ctx.shape[1]
    depth = w_mod.shape[0]
    nct_rows = nb * ctx_len
    geom = (nb, nct_rows, seq)
    tm = 512
    tmb = 1024 if nct_rows % 1024 == 0 and seq % 1024 == 0 else tm
    bf = jnp.bfloat16

    xs = jnp.concatenate([ctx.reshape(nct_rows, D), x.reshape(nb * seq, D)], axis=0)
    cpad = jnp.concatenate([c, c_ctx[None, :], jnp.zeros((8 - nb - 1, D), jnp.float32)], axis=0)
    mods_all = _modvecs(cpad, w_mod, b_mod)

    p_lb = jax.nn.softmax(hg_lb_logits.astype(jnp.float32), axis=0)
    lower_bounds = jnp.cumsum(p_lb, axis=0) - p_lb[:1]

    for i in range(depth):
        ctx_live = i < depth - 1
        mods = mods_all[i]
        j = i // 2
        if i % 2 == 0:
            p = _mod_matmul(xs, mods, hg_w_in[j].astype(bf), k_shift=0, geom=geom, tm=tmb, tn=1024,
                            out_dtype=jnp.float32)
            of, ob = _hg_scan(p, lower_bounds[j, 0:1], lower_bounds[j, 1:2], geom)
            xs = _hg_out(of, ob, p, hg_norm_w[j][None, :], hg_w_out[j].astype(bf), xs, mods,
                         geom=geom, tm=tmb, tn=1024)
        else:
            wq, wk, wv = jnp.split(sw_w_qkv[j], [D, D + SW_KV * SW_DH], axis=1)
            rep = lambda w: jnp.repeat(w.reshape(D, SW_KV, 1, SW_DH), SW_GROUP, axis=2).reshape(D, D)
            w3 = jnp.concatenate([wq, rep(wk), rep(wv)], axis=1).astype(bf)
            nw2 = jnp.stack([jnp.tile(sw_q_norm[j], SW_HEADS), jnp.tile(sw_k_norm[j], SW_HEADS)])[:, None, :]
            qkv = _qkv_proj(xs, mods, w3, nw2, geom=geom, tm=tmb)
            o = _attention(qkv, sw_sink[j], geom, ctx_live)
            xs = _res_matmul(o, sw_w_out[j].astype(bf), xs, mods, geom=geom, tm=tmb, tn=1024)
            if not ctx_live:
                geom = (nb, 0, seq)
        if i % 2 == 0:
            xs = _ffn(xs, mods, ff_w_gate_up[j].astype(bf), ff_w_down[j].astype(bf), geom=geom, tm=tm, tf=1408)
        else:
            rpad = jnp.pad(moe_router[j], ((0, 0), (0, LANES - N_EXPERTS)))
            xs = _moe(xs, mods, rpad, moe_w_gate_up[j].astype(bf), moe_w_down[j].astype(bf),
                      geom=geom, tm=tm, tf=1792, tmc=256)
    return xs[xs.shape[0] - nb * seq:].reshape(nb, seq, D)
```

```python
import functools

import numpy as np
import jax
import jax.numpy as jnp
from jax import lax
from jax.experimental import pallas as pl
from jax.experimental.pallas import tpu as pltpu

D = 1024
EPS = 1e-6
NEG_BIG = -1e30
LOG2E = 1.4426950408889634
GRID_W = 64
ROPE_THETA = 10000.0

HG_HEADS = 8
HG_DK = 128
HG_FDIM = HG_HEADS * HG_DK
HG_CHUNK = 128
HG_LEVELS = 7
HG_ROWS = 256
HG_HB = 4

SW_HEADS = 16
SW_KV = 4
SW_GROUP = 4
SW_DH = 64
SW_BLOCK = 128

N_EXPERTS = 8
LANES = 128

VMEM_LIMIT = 56 * 1024 * 1024


def _cparams(sem):
    return pltpu.CompilerParams(dimension_semantics=sem, vmem_limit_bytes=VMEM_LIMIT)


def _sigmoid(x):
    return 1.0 / (1.0 + jnp.exp2(x * -LOG2E))


def _mod_row(i, nct, lt, nb):
    return jnp.where(i < nct, nb, (i - nct) // lt)


def _mod_spec(k, nct, lt, nb):
    return pl.BlockSpec((None, 1, D), lambda i, *_: (_mod_row(i, nct, lt, nb) * 6 + k, 0, 0))


def _modulate(x, shift, scale):
    ms = jnp.mean(x * x, axis=-1, keepdims=True)
    return (x * lax.rsqrt(ms + EPS)) * (1.0 + scale) + shift


def _modvec_kernel(c_ref, w_ref, b_ref, o_ref):
    c = c_ref[...]
    s = c * _sigmoid(c)
    o_ref[...] = jnp.dot(s, w_ref[...], precision=lax.Precision.HIGHEST,
                         preferred_element_type=jnp.float32) + b_ref[...]


def _modvecs(cpad, w_mod, b_mod):
    depth = w_mod.shape[0]
    tn = 1024
    out = pl.pallas_call(
        _modvec_kernel,
        out_shape=jax.ShapeDtypeStruct((depth, 8, 6 * D), jnp.float32),
        grid=(depth, 6 * D // tn),
        in_specs=[pl.BlockSpec((8, D), lambda l, j: (0, 0)),
                  pl.BlockSpec((None, D, tn), lambda l, j: (l, 0, j)),
                  pl.BlockSpec((None, 1, tn), lambda l, j: (l, 0, j))],
        out_specs=pl.BlockSpec((None, 8, tn), lambda l, j: (l, 0, j)),
        name="adaln_vectors",
        compiler_params=_cparams(("arbitrary", "arbitrary")),
    )(cpad, w_mod, b_mod.reshape(depth, 1, 6 * D))
    return out.reshape(depth, 8 * 6, 1, D)


def _modmm_kernel(x_ref, sh_ref, sc_ref, w_ref, o_ref, h_ref):
    @pl.when(pl.program_id(1) == 0)
    def _():
        h_ref[...] = _modulate(x_ref[...], sh_ref[...], sc_ref[...]).astype(jnp.bfloat16)

    o_ref[...] = jnp.dot(h_ref[...], w_ref[...], preferred_element_type=jnp.float32).astype(o_ref.dtype)


def _mod_matmul(x, mods, w, *, k_shift, geom, tm, tn, out_dtype):
    nb, nct_rows, seq = geom
    t, n = x.shape[0], w.shape[1]
    nct, lt = nct_rows // tm, seq // tm
    return pl.pallas_call(
        _modmm_kernel,
        out_shape=jax.ShapeDtypeStruct((t, n), out_dtype),
        grid=(t // tm, n // tn),
        in_specs=[pl.BlockSpec((tm, D), lambda i, j: (i, 0)),
                  _mod_spec(k_shift, nct, lt, nb),
                  _mod_spec(k_shift + 1, nct, lt, nb),
                  pl.BlockSpec((D, tn), lambda i, j: (0, j))],
        out_specs=pl.BlockSpec((tm, tn), lambda i, j: (i, j)),
        scratch_shapes=[pltpu.VMEM((tm, D), jnp.bfloat16)],
        name="hg_in_proj",
        compiler_params=_cparams(("arbitrary", "arbitrary")),
    )(x, mods, mods, w)


def _hg_tables(fwd):
    c = HG_CHUNK
    t = np.arange(c)[:, None]
    u = np.arange(c)[None, :]
    reach = (u <= t) if fwd else (u >= t)
    x = t ^ u
    lvl = np.where(x > 0, np.floor(np.log2(np.maximum(x, 1))), HG_LEVELS)
    lvl = np.where(reach, lvl, -1).astype(np.int32)
    tri = np.concatenate([reach.astype(np.float32)] * 3, axis=1)
    return tri, lvl


def _hg_gates(q_raw, z, lb, tri_ref, a_ref):
    bf = jnp.bfloat16
    q = q_raw * _sigmoid(q_raw)
    sig = _sigmoid(z)
    fc = jnp.maximum(lb + (1.0 - lb) * sig, 1e-30)
    lf = jnp.log(fc) * LOG2E
    k = (1.0 - lb) * (1.0 - sig)
    hi = lf.astype(bf)
    rest = lf - hi.astype(jnp.float32)
    mid = rest.astype(bf)
    lo = (rest - mid.astype(jnp.float32)).astype(bf)
    a = jnp.dot(tri_ref[...], jnp.concatenate([hi, mid, lo], axis=0), preferred_element_type=jnp.float32)
    a_ref[...] = a
    return q, k, fc, a


def _hg_scores(q, k, fc, a, a_ref, lvl, fwd):
    c = HG_CHUNK
    bf = jnp.bfloat16

    def rows_of(row, n):
        return jnp.broadcast_to(a_ref[row:row + 1, :], (n, HG_DK))

    rowi = lax.broadcasted_iota(jnp.int32, (c, HG_DK), 0)
    nt = (((1,), (1,)), ((), ()))
    zeros8 = jnp.zeros((8, HG_DK), jnp.float32)

    def level_operands(l):
        w = 1 << l
        near = w - 1 if fwd else w
        if w < 8:
            upper = ((rowi >> l) & 1) == 1
            q_side = upper if fwd else jnp.logical_not(upper)
            if l == 0:
                qe, ke = q * fc, k
            else:
                if l == 1:
                    first = jnp.concatenate([rows_of(8 * g + near, 8) for g in range(c // 8)], axis=0)
                    second = jnp.concatenate([rows_of(8 * g + 4 + near, 8) for g in range(c // 8)], axis=0)
                    bnd = jnp.where((rowi & 4) == 0, first, second)
                else:
                    bnd = jnp.concatenate([rows_of(8 * g + near, 8) for g in range(c // 8)], axis=0)
                e = jnp.exp2(-jnp.abs(a - bnd))
                qe, ke = q * e, k * e
            return jnp.where(q_side, qe, 0.0).astype(bf), jnp.where(q_side, 0.0, ke).astype(bf)
        qp, kp = [], []
        for j in range(c // w):
            rs = slice(j * w, (j + 1) * w)
            bnd = rows_of((j // 2) * 2 * w + near, w)
            zero = jnp.concatenate([zeros8] * (w // 8), axis=0)
            if (j % 2 == 1) == fwd:
                qp.append(q[rs] * jnp.exp2(a[rs] - bnd))
                kp.append(zero)
            else:
                qp.append(zero)
                kp.append(k[rs] * jnp.exp2(bnd - a[rs]))
        return jnp.concatenate(qp, axis=0).astype(bf), jnp.concatenate(kp, axis=0).astype(bf)

    scores = jnp.where(lvl == HG_LEVELS,
                       lax.dot_general(q.astype(bf), k.astype(bf), nt, preferred_element_type=jnp.float32), 0.0)
    for l in range(HG_LEVELS):
        qm, km = level_operands(l)
        sc = lax.dot_general(qm, km, nt, preferred_element_type=jnp.float32)
        scores = jnp.where(lvl == l, sc, scores)
    return scores.astype(bf)


def _hg_finish(q, k, v, a, scores, st_ref, fwd):
    c = HG_CHUNK
    bf = jnp.bfloat16
    nt = (((1,), (1,)), ((), ()))
    last = c - 1 if fwd else 0
    e_in = jnp.exp2(a)
    e_out = jnp.exp2(a[last:last + 1, :] - a)
    st = st_ref[...]
    o = (jnp.dot(scores, v.astype(bf), preferred_element_type=jnp.float32)
         + lax.dot_general((q * e_in).astype(bf), st.astype(bf), nt, preferred_element_type=jnp.float32))
    st_ref[...] = (e_in[last:last + 1, :] * st
                   + jnp.dot(v.T.astype(bf), (k * e_out).astype(bf), preferred_element_type=jnp.float32))
    return o


def _hg_scan_kernel(qf_ref, vf_ref, zf_ref, qb_ref, vb_ref, zb_ref, lbf_ref, lbb_ref, trif_ref, trib_ref,
                    lvlf_ref, lvlb_ref, of_ref, ob_ref, sf_ref, sb_ref, a_ref):
    @pl.when(pl.program_id(2) == 0)
    def _():
        sf_ref[...] = jnp.zeros_like(sf_ref)
        sb_ref[...] = jnp.zeros_like(sb_ref)

    c = HG_CHUNK
    nch = HG_ROWS // c
    lvlf, lvlb = lvlf_ref[...], lvlb_ref[...]
    units = []
    for hh in range(HG_HB):
        cs = slice(hh * HG_DK, (hh + 1) * HG_DK)
        for step, ci in enumerate(range(nch)):
            units.append(dict(step=step, rows=slice(ci * c, (ci + 1) * c), cols=cs, fwd=True, q=qf_ref, v=vf_ref,
                              z=zf_ref, lb=lbf_ref, tri=trif_ref, lvl=lvlf, st=sf_ref.at[hh], out=of_ref))
        for step, ci in enumerate(reversed(range(nch))):
            units.append(dict(step=step, rows=slice(ci * c, (ci + 1) * c), cols=cs, fwd=False, q=qb_ref, v=vb_ref,
                              z=zb_ref, lb=lbb_ref, tri=trib_ref, lvl=lvlb, st=sb_ref.at[hh], out=ob_ref))
    for n, u in enumerate(units):
        u["a_ref"] = a_ref.at[n]
        u["q"], u["k"], u["fc"], u["a"] = _hg_gates(u["q"][u["rows"], u["cols"]], u["z"][u["rows"], u["cols"]],
                                                    u["lb"][:, u["cols"]], u["tri"], u["a_ref"])
    for u in units:
        u["scores"] = _hg_scores(u["q"], u["k"], u["fc"], u["a"], u["a_ref"], u["lvl"], u["fwd"])
    for step in range(nch):
        for u in units:
            if u["step"] == step:
                o = _hg_finish(u["q"], u["k"], u["v"][u["rows"], u["cols"]], u["a"], u["scores"], u["st"], u["fwd"])
                u["out"][u["rows"], u["cols"]] = o.astype(u["out"].dtype)


def _hg_scan(p, lbf, lbb, geom):
    nb, nct_rows, seq = geom
    t = p.shape[0]
    r = HG_ROWS
    cb, lb_ = (nct_rows // nb) // r, seq // r
    nsteps = cb + lb_
    lat0 = nct_rows // r

    def fblk(b, s):
        return jnp.where(s < cb, b * cb + s, lat0 + b * lb_ + (s - cb))

    def bblk(b, s):
        return jnp.where(s < cb, b * cb + (cb - 1 - s), lat0 + b * lb_ + (lb_ - 1 - (s - cb)))

    wcol = HG_HB * HG_DK
    ngrp = HG_HEADS // HG_HB
    units = 2 * HG_HB * (r // HG_CHUNK)

    def spec(blk, colblock):
        return pl.BlockSpec((r, wcol), lambda b, h, s: (blk(b, s), colblock * ngrp + h))

    lbspec = pl.BlockSpec((1, wcol), lambda b, h, s: (0, h))
    const = lambda shape: pl.BlockSpec(shape, lambda b, h, s: (0, 0))
    trif, lvlf = _hg_tables(True)
    trib, lvlb = _hg_tables(False)
    return pl.pallas_call(
        _hg_scan_kernel,
        out_shape=(jax.ShapeDtypeStruct((t, D), jnp.bfloat16), jax.ShapeDtypeStruct((t, D), jnp.bfloat16)),
        grid=(nb, ngrp, nsteps),
        in_specs=[spec(fblk, 0), spec(fblk, 1), spec(fblk, 2),
                  spec(bblk, 0), spec(bblk, 1), spec(bblk, 3),
                  lbspec, lbspec, const(trif.shape), const(trib.shape), const(lvlf.shape), const(lvlb.shape)],
        out_specs=(pl.BlockSpec((r, wcol), lambda b, h, s: (fblk(b, s), h)),
                   pl.BlockSpec((r, wcol), lambda b, h, s: (bblk(b, s), h))),
        scratch_shapes=[pltpu.VMEM((HG_HB, HG_DK, HG_DK), jnp.float32),
                        pltpu.VMEM((HG_HB, HG_DK, HG_DK), jnp.float32),
                        pltpu.VMEM((units, HG_CHUNK, HG_DK), jnp.float32)],
        name="hg_scan",
        compiler_params=_cparams(("arbitrary", "arbitrary", "arbitrary")),
    )(p, p, p, p, p, p, lbf, lbb, jnp.asarray(trif, jnp.bfloat16), jnp.asarray(trib, jnp.bfloat16),
      jnp.asarray(lvlf), jnp.asarray(lvlb))


def _hg_out_kernel(of_ref, ob_ref, g_ref, nw_ref, w_ref, x_ref, gate_ref, o_ref, h_ref):
    @pl.when(pl.program_id(1) == 0)
    def _():
        nw = nw_ref[...]
        for h in range(HG_HEADS):
            cs = slice(h * HG_DK, (h + 1) * HG_DK)
            o = of_ref[:, cs].astype(jnp.float32) + ob_ref[:, cs].astype(jnp.float32)
            y = o * lax.rsqrt(jnp.mean(o * o, axis=-1, keepdims=True) + EPS) * nw
            g = g_ref[:, cs]
            h_ref[:, cs] = (y * (g * _sigmoid(g))).astype(jnp.bfloat16)

    acc = jnp.dot(h_ref[...], w_ref[...], preferred_element_type=jnp.float32)
    o_ref[...] = x_ref[...] + gate_ref[...] * acc


def _hg_out(of, ob, p, norm_w, w, x, mods, *, geom, tm, tn):
    nb, nct_rows, seq = geom
    t = x.shape[0]
    nct, lt = nct_rows // tm, seq // tm
    gate = pl.BlockSpec((None, 1, tn), lambda i, j: (_mod_row(i, nct, lt, nb) * 6 + 2, 0, j))
    return pl.pallas_call(
        _hg_out_kernel,
        out_shape=jax.ShapeDtypeStruct((t, D), jnp.float32),
        grid=(t // tm, D // tn),
        in_specs=[pl.BlockSpec((tm, D), lambda i, j: (i, 0)),
                  pl.BlockSpec((tm, D), lambda i, j: (i, 0)),
                  pl.BlockSpec((tm, D), lambda i, j: (i, 4)),
                  pl.BlockSpec((1, HG_DK), lambda i, j: (0, 0)),
                  pl.BlockSpec((D, tn), lambda i, j: (0, j)),
                  pl.BlockSpec((tm, tn), lambda i, j: (i, j)),
                  gate],
        out_specs=pl.BlockSpec((tm, tn), lambda i, j: (i, j)),
        scratch_shapes=[pltpu.VMEM((tm, D), jnp.bfloat16)],
        name="hg_out_proj",
        compiler_params=_cparams(("arbitrary", "arbitrary")),
    )(of, ob, p, norm_w, w, x, mods)


def _resmm_kernel(a_ref, w_ref, x_ref, gate_ref, o_ref):
    acc = jnp.dot(a_ref[...], w_ref[...], preferred_element_type=jnp.float32)
    o_ref[...] = x_ref[...] + gate_ref[...] * acc


def _res_matmul(a, w, x, mods, *, geom, tm, tn):
    nb, nct_rows, seq = geom
    rows = a.shape[0]
    t0 = (x.shape[0] - rows) // tm
    nct, lt = nct_rows // tm, seq // tm
    gate = pl.BlockSpec((None, 1, tn), lambda i, j: (_mod_row(i + t0, nct, lt, nb) * 6 + 2, 0, j))
    return pl.pallas_call(
        _resmm_kernel,
        out_shape=jax.ShapeDtypeStruct((rows, D), jnp.float32),
        grid=(rows // tm, D // tn),
        in_specs=[pl.BlockSpec((tm, a.shape[1]), lambda i, j: (i, 0)),
                  pl.BlockSpec((a.shape[1], tn), lambda i, j: (0, j)),
                  pl.BlockSpec((tm, tn), lambda i, j: (i + t0, j)),
                  gate],
        out_specs=pl.BlockSpec((tm, tn), lambda i, j: (i, j)),
        name="attn_out_proj",
        compiler_params=_cparams(("arbitrary", "arbitrary")),
    )(a, w, x, mods)


def _rope_tables(seq, tm):
    rows = seq // GRID_W
    row = np.repeat(np.arange(rows, dtype=np.float32), GRID_W)
    col = np.tile(np.arange(GRID_W, dtype=np.float32), rows)
    nf = SW_DH // 4
    inv = (ROPE_THETA ** (-np.arange(nf, dtype=np.float32) / nf)).astype(np.float32)
    ang_r = row[:, None] * inv
    ang_c = col[:, None] * inv
    cos = np.concatenate([np.cos(ang_r), np.cos(ang_r), np.cos(ang_c), np.cos(ang_c)], axis=1)
    sin = np.concatenate([-np.sin(ang_r), np.sin(ang_r), -np.sin(ang_c), np.sin(ang_c)], axis=1)
    cos = np.concatenate([np.tile(cos, (1, 2)), np.ones((tm, LANES), np.float32)], axis=0)
    sin = np.concatenate([np.tile(sin, (1, 2)), np.zeros((tm, LANES), np.float32)], axis=0)
    return jnp.asarray(cos, jnp.float32), jnp.asarray(sin, jnp.float32)


def _qkv_kernel(x_ref, sh_ref, sc_ref, w_ref, nw_ref, ones_ref, cos_ref, sin_ref, o_ref, h_ref):
    j = pl.program_id(1)

    @pl.when(j == 0)
    def _():
        h_ref[...] = _modulate(x_ref[...], sh_ref[...], sc_ref[...]).astype(jnp.bfloat16)

    acc = jnp.dot(h_ref[...], w_ref[...], preferred_element_type=jnp.float32)

    @pl.when(j == 2)
    def _():
        o_ref[...] = acc.astype(o_ref.dtype)

    @pl.when(j < 2)
    def _():
        ss = jnp.dot((acc * acc).astype(jnp.bfloat16), ones_ref[...], preferred_element_type=jnp.float32)
        y = acc * lax.rsqrt(ss * (1.0 / SW_DH) + EPS) * nw_ref[...]
        cos = cos_ref[...]
        sin = sin_ref[...]
        lane = lax.broadcasted_iota(jnp.int32, (y.shape[0], LANES), 1)
        first = (lane & 16) == 0
        scale = jnp.where(j == 0, SW_DH ** -0.5 * LOG2E, 1.0)
        for g in range(D // LANES):
            cs = slice(g * LANES, (g + 1) * LANES)
            yg = y[:, cs]
            partner = jnp.where(first, pltpu.roll(yg, LANES - 16, 1), pltpu.roll(yg, 16, 1))
            o_ref[:, cs] = ((yg * cos + partner * sin) * scale).astype(o_ref.dtype)


def _qkv_proj(x, mods, w3, nw2, *, geom, tm):
    nb, nct_rows, seq = geom
    t = x.shape[0]
    nct, lt = nct_rows // tm, seq // tm
    cos, sin = _rope_tables(seq, tm)
    blk = np.kron(np.eye(D // SW_DH, dtype=np.float32), np.ones((SW_DH, SW_DH), np.float32))
    ones = jnp.asarray(blk, jnp.bfloat16)
    tab = lambda i, j: (jnp.where(i < nct, lt, (i - nct) % lt), 0)
    return pl.pallas_call(
        _qkv_kernel,
        out_shape=jax.ShapeDtypeStruct((t, 3 * D), jnp.bfloat16),
        grid=(t // tm, 3),
        in_specs=[pl.BlockSpec((tm, D), lambda i, j: (i, 0)),
                  _mod_spec(0, nct, lt, nb),
                  _mod_spec(1, nct, lt, nb),
                  pl.BlockSpec((D, D), lambda i, j: (0, j)),
                  pl.BlockSpec((None, 1, D), lambda i, j: (jnp.minimum(j, 1), 0, 0)),
                  pl.BlockSpec((D, D), lambda i, j: (0, 0)),
                  pl.BlockSpec((tm, LANES), tab),
                  pl.BlockSpec((tm, LANES), tab)],
        out_specs=pl.BlockSpec((tm, D), lambda i, j: (i, j)),
        scratch_shapes=[pltpu.VMEM((tm, D), jnp.bfloat16)],
        name="qkv_proj",
        compiler_params=_cparams(("arbitrary", "arbitrary")),
    )(x, mods, mods, w3, nw2, ones, cos, sin)


def _attn_items(items, sink_ref, kv, o_ref):
    nt = (((1,), (1,)), ((), ()))
    work = [(it, g) for it in items for g in range(SW_GROUP)]
    scores = []
    for (_, q, kcat, _, _), g in work:
        grp = lax.broadcasted_iota(jnp.int32, q.shape, 1) // SW_DH
        scores.append(lax.dot_general(jnp.where(grp == g, q, jnp.zeros_like(q)), kcat, nt,
                                      preferred_element_type=jnp.float32))
    probs, denoms = [], []
    for ((_, _, _, _, valid), g), s in zip(work, scores):
        sink = sink_ref[kv * SW_GROUP + g] * LOG2E
        if valid is not None:
            s = jnp.where(valid, s, NEG_BIG)
        m = jnp.maximum(jnp.max(s, axis=-1, keepdims=True), sink)
        p = jnp.exp2(s - m)
        denoms.append(jnp.sum(p, axis=-1, keepdims=True) + jnp.exp2(sink - m))
        probs.append(p.astype(jnp.bfloat16))
    outs = {}
    for n, ((rows, q, _, vcat, _), g) in enumerate(work):
        grp = lax.broadcasted_iota(jnp.int32, q.shape, 1) // SW_DH
        og = jnp.dot(probs[n], vcat, preferred_element_type=jnp.float32) / denoms[n]
        outs[rows] = og if g == 0 else jnp.where(grp == g, og, outs[rows])
    for rows, out in outs.items():
        o_ref[rows[0]:rows[1], :] = out.astype(o_ref.dtype)


def _attn_kernel(sink_ref, q_ref, kp_ref, kc_ref, kn_ref, kx_ref, vp_ref, vc_ref, vn_ref, vx_ref, o_ref,
                 *, nstep, ctx_out):
    kv, j = pl.program_id(1), pl.program_id(2)
    blk = SW_BLOCK

    def latent():
        kx, vx = kx_ref[...], vx_ref[...]
        k0, k1, v0, v1 = kc_ref[0:blk, :], kc_ref[blk:2 * blk, :], vc_ref[0:blk, :], vc_ref[blk:2 * blk, :]
        ns = 3 * blk + kx.shape[0]
        t = lax.broadcasted_iota(jnp.int32, (blk, ns), 0)
        s = lax.broadcasted_iota(jnp.int32, (blk, ns), 1)

        def window(has_prev, has_next):
            lo = jnp.maximum(t, jnp.where(has_prev, 0, blk))
            hi = jnp.minimum(t + 2 * blk, jnp.where(has_next, 3 * blk - 1, 2 * blk - 1))
            return ((s >= lo) & (s <= hi)) | (s >= 3 * blk)

        items = [((0, blk), q_ref[0:blk, :], jnp.concatenate([kp_ref[...], k0, k1, kx], axis=0),
                  jnp.concatenate([vp_ref[...], v0, v1, vx], axis=0), window(j > 0, True)),
                 ((blk, 2 * blk), q_ref[blk:2 * blk, :], jnp.concatenate([k0, k1, kn_ref[...], kx], axis=0),
                  jnp.concatenate([v0, v1, vn_ref[...], vx], axis=0), window(True, j < nstep - 1))]
        _attn_items(items, sink_ref, kv, o_ref)

    if not ctx_out:
        latent()
        return
    pl.when(j < nstep)(latent)

    @pl.when(j >= nstep)
    def _():
        _attn_items([((0, 2 * blk), q_ref[...], kx_ref[...], vx_ref[...], None)], sink_ref, kv, o_ref)


def _attention(qkv, sink, geom, ctx_out):
    nb, nct_rows, seq = geom
    ctx_len = nct_rows // nb
    pair = 2 * SW_BLOCK
    assert ctx_len == pair and seq % pair == 0
    nblk = seq // SW_BLOCK
    nstep = seq // pair
    lat0 = nct_rows // SW_BLOCK
    w = SW_GROUP * SW_DH
    smem = pl.BlockSpec(memory_space=pltpu.SMEM)

    def qrow(b, j):
        return jnp.where(j < nstep, nct_rows // pair + b * nstep + j, b)

    def edge(off, colblock):
        return pl.BlockSpec((SW_BLOCK, w), lambda b, kv, j: (
            lat0 + b * nblk + jnp.clip(2 * jnp.minimum(j, nstep - 1) + off, 0, nblk - 1), colblock * SW_KV + kv))

    def center(colblock):
        return pl.BlockSpec((pair, w), lambda b, kv, j: (qrow(b, jnp.minimum(j, nstep - 1)), colblock * SW_KV + kv))

    def ctxkv(colblock):
        return pl.BlockSpec((ctx_len, w), lambda b, kv, j: (b, colblock * SW_KV + kv))

    if ctx_out:
        steps, out_rows = nstep + 1, qkv.shape[0]
        out_spec = pl.BlockSpec((pair, w), lambda b, kv, j: (qrow(b, j), kv))
    else:
        steps, out_rows = nstep, nb * seq
        out_spec = pl.BlockSpec((pair, w), lambda b, kv, j: (b * nstep + j, kv))
    return pl.pallas_call(
        functools.partial(_attn_kernel, nstep=nstep, ctx_out=ctx_out),
        out_shape=jax.ShapeDtypeStruct((out_rows, D), jnp.bfloat16),
        grid=(nb, SW_KV, steps),
        in_specs=[smem, pl.BlockSpec((pair, w), lambda b, kv, j: (qrow(b, j), kv)),
                  edge(-1, 1), center(1), edge(2, 1), ctxkv(1),
                  edge(-1, 2), center(2), edge(2, 2), ctxkv(2)],
        out_specs=out_spec,
        name="attention",
        compiler_params=_cparams(("arbitrary", "arbitrary", "arbitrary")),
    )(sink, qkv, qkv, qkv, qkv, qkv, qkv, qkv, qkv, qkv)


def _router_kernel(x_ref, sh_ref, sc_ref, r_ref, h_ref, info_ref):
    h = _modulate(x_ref[...], sh_ref[...], sc_ref[...])
    h_ref[...] = h
    logits = jnp.dot(h, r_ref[...], precision=lax.Precision.HIGHEST, preferred_element_type=jnp.float32)
    lane = lax.broadcasted_iota(jnp.int32, logits.shape, 1).astype(jnp.float32)
    logits = jnp.where(lane < N_EXPERTS, logits, -jnp.inf)
    l1 = jnp.max(logits, axis=-1, keepdims=True)
    i1 = jnp.min(jnp.where(logits == l1, lane, float(LANES)), axis=-1, keepdims=True)
    rest = jnp.where(lane == i1, -jnp.inf, logits)
    l2 = jnp.max(rest, axis=-1, keepdims=True)
    i2 = jnp.min(jnp.where(rest == l2, lane, float(LANES)), axis=-1, keepdims=True)
    w1 = 1.0 / (1.0 + jnp.exp(l2 - l1))
    info_ref[...] = jnp.where(lane == 0.0, i1, jnp.where(lane == 1.0, i2, jnp.where(lane == 2.0, w1, 1.0 - w1)))


def _router(x, mods, router_pad, *, geom, tm):
    nb, nct_rows, seq = geom
    t = x.shape[0]
    nct, lt = nct_rows // tm, seq // tm
    return pl.pallas_call(
        _router_kernel,
        out_shape=(jax.ShapeDtypeStruct((t, D), jnp.float32), jax.ShapeDtypeStruct((t, LANES), jnp.float32)),
        grid=(t // tm,),
        in_specs=[pl.BlockSpec((tm, D), lambda i: (i, 0)), _mod_spec(3, nct, lt, nb), _mod_spec(4, nct, lt, nb),
                  pl.BlockSpec((D, LANES), lambda i: (0, 0))],
        out_specs=(pl.BlockSpec((tm, D), lambda i: (i, 0)), pl.BlockSpec((tm, LANES), lambda i: (i, 0))),
        name="moe_router",
        compiler_params=_cparams(("arbitrary",)),
    )(x, mods, mods, router_pad)


def _swiglu_step(h, wg_ref, wu_ref, wd_ref, row_scale):
    g = jnp.dot(h, wg_ref[...], preferred_element_type=jnp.float32)
    u = jnp.dot(h, wu_ref[...], preferred_element_type=jnp.float32)
    a = g * _sigmoid(g) * u
    if row_scale is not None:
        a = a * row_scale
    return jnp.dot(a.astype(jnp.bfloat16), wd_ref[...], preferred_element_type=jnp.float32)


def _ffn_kernel(x_ref, sh_ref, sc_ref, gate_ref, wg_ref, wu_ref, wd_ref, o_ref, h_ref, acc_ref, *, nf):
    f = pl.program_id(1)

    @pl.when(f == 0)
    def _():
        h_ref[...] = _modulate(x_ref[...], sh_ref[...], sc_ref[...]).astype(jnp.bfloat16)

    y = _swiglu_step(h_ref[...], wg_ref, wu_ref, wd_ref, None)
    if nf == 1:
        o_ref[...] = x_ref[...] + gate_ref[...] * y
        return

    @pl.when(f == 0)
    def _():
        acc_ref[...] = y

    @pl.when((f > 0) & (f < nf - 1))
    def _():
        acc_ref[...] += y

    @pl.when(f == nf - 1)
    def _():
        o_ref[...] = x_ref[...] + gate_ref[...] * (acc_ref[...] + y)


def _ffn(x, mods, wgu, wd, *, geom, tm, tf):
    nb, nct_rows, seq = geom
    t = x.shape[0]
    nf = wd.shape[0] // tf
    nct, lt = nct_rows // tm, seq // tm
    return pl.pallas_call(
        functools.partial(_ffn_kernel, nf=nf),
        out_shape=jax.ShapeDtypeStruct((t, D), jnp.float32),
        grid=(t // tm, nf),
        in_specs=[pl.BlockSpec((tm, D), lambda i, f: (i, 0)),
                  _mod_spec(3, nct, lt, nb),
                  _mod_spec(4, nct, lt, nb),
                  _mod_spec(5, nct, lt, nb),
                  pl.BlockSpec((D, tf), lambda i, f: (0, f)),
                  pl.BlockSpec((D, tf), lambda i, f: (0, nf + f)),
                  pl.BlockSpec((tf, D), lambda i, f: (f, 0))],
        out_specs=pl.BlockSpec((tm, D), lambda i, f: (i, 0)),
        scratch_shapes=[pltpu.VMEM((tm, D), jnp.bfloat16), pltpu.VMEM((tm, D), jnp.float32)],
        name="dense_ffn",
        compiler_params=_cparams(("arbitrary", "arbitrary")),
    )(x, mods, mods, mods, wgu, wgu, wd)


MOE_SUB = 512


def _moe_kernel(te_ref, nt_ref, rows0_ref, rows1_ref, rows2_ref, outrows_ref, h_hbm, wgu_ref, wd_ref, y_hbm,
                hbuf, hb_ref, obuf, gsem, ssem, *, tm):
    del te_ref
    i = pl.program_id(0)
    nt = nt_ref[0]
    g_slot, o_slot = i % 3, i % 2
    fdim = wd_ref.shape[0]

    def gather(rows_ref, s):
        for r in range(tm):
            pltpu.make_async_copy(h_hbm.at[pl.ds(rows_ref[0, r], 1), :], hbuf.at[s, pl.ds(r, 1), :],
                                  gsem.at[s]).start()

    def scatter_prev():
        for r in range(tm):
            pltpu.make_async_copy(obuf.at[1 - o_slot, pl.ds(r, 1), :], y_hbm.at[pl.ds(outrows_ref[0, r], 1), :],
                                  ssem.at[1 - o_slot]).start()

    @pl.when(i == 0)
    def _():
        gather(rows0_ref, 0)
        gather(rows1_ref, 1)
        obuf[1] = jnp.zeros((tm, D), jnp.float32)

    @pl.when((i >= 1) & (i - 2 < nt))
    def _():
        pltpu.make_async_copy(obuf.at[o_slot], obuf.at[o_slot], ssem.at[o_slot]).wait()

    @pl.when(i < nt + 2)
    def _():
        pltpu.make_async_copy(hbuf.at[g_slot], hbuf.at[g_slot], gsem.at[g_slot]).wait()
        hb_ref[...] = hbuf[g_slot].astype(jnp.bfloat16)

    @pl.when(i < nt)
    def _():
        gather(rows2_ref, (i + 2) % 3)
        scatter_prev()
        h = hb_ref[...]
        nsub = fdim // MOE_SUB
        up = lambda s: (jnp.dot(h, wgu_ref[:, s * MOE_SUB:(s + 1) * MOE_SUB], preferred_element_type=jnp.float32),
                        jnp.dot(h, wgu_ref[:, fdim + s * MOE_SUB:fdim + (s + 1) * MOE_SUB],
                                preferred_element_type=jnp.float32))
        down = lambda gu, s: jnp.dot((gu[0] * _sigmoid(gu[0]) * gu[1]).astype(jnp.bfloat16),
                                     wd_ref[s * MOE_SUB:(s + 1) * MOE_SUB, :], preferred_element_type=jnp.float32)
        gu = up(0)
        acc = None
        for s in range(1, nsub):
            gu_next = up(s)
            y = down(gu, s - 1)
            acc = y if acc is None else acc + y
            gu = gu_next
        y = down(gu, nsub - 1)
        obuf[o_slot] = y if acc is None else acc + y

    @pl.when(i == nt)
    def _():
        scatter_prev()


def _moe_experts(h, tile_expert, n_tiles, rows, outrows, wgu, wd, n_out, *, tm):
    nt = rows.shape[0]
    fdim = wd.shape[1]
    assert fdim % MOE_SUB == 0 and nt >= 3
    smem_rows = lambda off: pl.BlockSpec((None, 1, tm), lambda i, te, n: (jnp.minimum(i + off, nt - 1), 0, 0),
                                         memory_space=pltpu.SMEM)
    once = pl.Buffered(1)
    grid_spec = pltpu.PrefetchScalarGridSpec(
        num_scalar_prefetch=2,
        grid=(nt,),
        in_specs=[smem_rows(0), smem_rows(1), smem_rows(2),
                  pl.BlockSpec((None, 1, tm), lambda i, te, n: (i, 0, 0), memory_space=pltpu.SMEM),
                  pl.BlockSpec(memory_space=pl.ANY),
                  pl.BlockSpec((None, D, 2 * fdim), lambda i, te, n: (te[i], 0, 0), pipeline_mode=once),
                  pl.BlockSpec((None, fdim, D), lambda i, te, n: (te[i], 0, 0), pipeline_mode=once)],
        out_specs=pl.BlockSpec(memory_space=pl.ANY),
        scratch_shapes=[pltpu.VMEM((3, tm, D), jnp.float32), pltpu.VMEM((tm, D), jnp.bfloat16),
                        pltpu.VMEM((2, tm, D), jnp.float32), pltpu.SemaphoreType.DMA((3,)),
                        pltpu.SemaphoreType.DMA((2,))])
    return pl.pallas_call(
        functools.partial(_moe_kernel, tm=tm),
        out_shape=jax.ShapeDtypeStruct((n_out, D), jnp.float32),
        grid_spec=grid_spec,
        name="moe_experts",
        compiler_params=_cparams(("arbitrary",)),
    )(tile_expert, n_tiles, rows, rows, rows, outrows, h, wgu, wd)


def _combine_kernel(y_ref, x_ref, gate_ref, info_ref, o_ref):
    info = info_ref[...]
    y = info[:, 2:3] * y_ref[:, 0:D] + info[:, 3:4] * y_ref[:, D:2 * D]
    o_ref[...] = x_ref[...] + gate_ref[...] * y


def _moe_combine(x, y2, info, mods, *, geom, tmc):
    nb, nct_rows, seq = geom
    t = x.shape[0]
    nct, lt = nct_rows // tmc, seq // tmc
    return pl.pallas_call(
        _combine_kernel,
        out_shape=jax.ShapeDtypeStruct((t, D), jnp.float32),
        grid=(t // tmc,),
        in_specs=[pl.BlockSpec((tmc, 2 * D), lambda i: (i, 0)),
                  pl.BlockSpec((tmc, D), lambda i: (i, 0)),
                  _mod_spec(5, nct, lt, nb),
                  pl.BlockSpec((tmc, LANES), lambda i: (i, 0))],
        out_specs=pl.BlockSpec((tmc, D), lambda i: (i, 0)),
        name="moe_combine",
        compiler_params=_cparams(("arbitrary",)),
    )(y2, x, mods, info)


def _moe(x, mods, router_pad, wgu, wd, *, geom, tm, tmc):
    tr = x.shape[0]
    h, info = _router(x, mods, router_pad, geom=geom, tm=tm)
    e = info[:, 0:2].astype(jnp.int32).reshape(-1)
    onehot = (e[:, None] == jnp.arange(N_EXPERTS, dtype=jnp.int32)[None, :]).astype(jnp.int32)
    csum = jnp.cumsum(onehot, axis=0)
    rank = jnp.sum(onehot * (csum - 1), axis=1)
    counts = csum[-1]
    padded = ((counts + tm - 1) // tm) * tm
    ends = jnp.cumsum(padded)
    dest = (ends - padded)[e] + rank
    nt = 2 * tr // tm + N_EXPERTS + 2
    tile_expert = jnp.minimum(jnp.sum(jnp.arange(nt, dtype=jnp.int32)[:, None] * tm >= ends[None, :], axis=1),
                              N_EXPERTS - 1).astype(jnp.int32)
    n_tiles = (ends[-1:] // tm).astype(jnp.int32)
    slot_of_row = jnp.full((nt * tm,), -1, jnp.int32).at[dest].set(jnp.arange(2 * tr, dtype=jnp.int32))
    spare = 2 * tr + jnp.arange(nt * tm, dtype=jnp.int32) % tm
    rows = (jnp.maximum(slot_of_row, 0) // 2).reshape(nt, 1, tm)
    outrows = jnp.concatenate([spare[:tm], jnp.where(slot_of_row >= 0, slot_of_row, spare)]).reshape(nt + 1, 1, tm)
    y = _moe_experts(h, tile_expert, n_tiles, rows, outrows, wgu, wd, 2 * tr + tm, tm=tm)
    return _moe_combine(x, y.reshape(tr + tm // 2, 2 * D), info, mods, geom=geom, tmc=tmc)


def kernel(x, c, ctx, c_ctx, w_mod, b_mod, hg_w_in, hg_lb_logits, hg_norm_w, hg_w_out, sw_w_qkv, sw_q_norm,
           sw_k_norm, sw_sink, sw_w_out, ff_w_gate_up, ff_w_down, moe_router, moe_w_gate_up, moe_w_down):
    nb, seq, _ = x.shape
    ctx_len = ctx.shape[1]
    depth = w_mod.shape[0]
    nct_rows = nb * ctx_len
    geom = (nb, nct_rows, seq)
    tm = 512
    tmb = 1024 if nct_rows % 1024 == 0 and seq % 1024 == 0 else tm
    bf = jnp.bfloat16

    xs = jnp.concatenate([ctx.reshape(nct_rows, D), x.reshape(nb * seq, D)], axis=0)
    cpad = jnp.concatenate([c, c_ctx[None, :], jnp.zeros((8 - nb - 1, D), jnp.float32)], axis=0)
    mods_all = _modvecs(cpad, w_mod, b_mod)

    p_lb = jax.nn.softmax(hg_lb_logits.astype(jnp.float32), axis=0)
    lower_bounds = jnp.cumsum(p_lb, axis=0) - p_lb[:1]

    for i in range(depth):
        ctx_live = i < depth - 1
        mods = mods_all[i]
        j = i // 2
        if i % 2 == 0:
            p = _mod_matmul(xs, mods, hg_w_in[j].astype(bf), k_shift=0, geom=geom, tm=tmb, tn=1024,
                            out_dtype=jnp.float32)
            of, ob = _hg_scan(p, lower_bounds[j, 0:1], lower_bounds[j, 1:2], geom)
            xs = _hg_out(of, ob, p, hg_norm_w[j][None, :], hg_w_out[j].astype(bf), xs, mods,
                         geom=geom, tm=tmb, tn=1024)
        else:
            wq, wk, wv = jnp.split(sw_w_qkv[j], [D, D + SW_KV * SW_DH], axis=1)
            rep = lambda w: jnp.repeat(w.reshape(D, SW_KV, 1, SW_DH), SW_GROUP, axis=2).reshape(D, D)
            w3 = jnp.concatenate([wq, rep(wk), rep(wv)], axis=1).astype(bf)
            nw2 = jnp.stack([jnp.tile(sw_q_norm[j], SW_HEADS), jnp.tile(sw_k_norm[j], SW_HEADS)])[:, None, :]
            qkv = _qkv_proj(xs, mods, w3, nw2, geom=geom, tm=tmb)
            o = _attention(qkv, sw_sink[j], geom, ctx_live)
            xs = _res_matmul(o, sw_w_out[j].astype(bf), xs, mods, geom=geom, tm=tmb, tn=1024)
            if not ctx_live:
                geom = (nb, 0, seq)
        if i % 2 == 0:
            xs = _ffn(xs, mods, ff_w_gate_up[j].astype(bf), ff_w_down[j].astype(bf), geom=geom, tm=tm, tf=1408)
        else:
            rpad = jnp.pad(moe_router[j], ((0, 0), (0, LANES - N_EXPERTS)))
            xs = _moe(xs, mods, rpad, moe_w_gate_up[j].astype(bf), moe_w_down[j].astype(bf),
                      geom=geom, tm=tm, tmc=tm)
    return xs[xs.shape[0] - nb * seq:].reshape(nb, seq, D)
```

```python
import functools

import numpy as np
import jax
import jax.numpy as jnp
from jax import lax
from jax.experimental import pallas as pl
from jax.experimental.pallas import tpu as pltpu

D = 1024
EPS = 1e-6
NEG_BIG = -1e30
LOG2E = 1.4426950408889634
GRID_W = 64
ROPE_THETA = 10000.0

HG_HEADS = 8
HG_DK = 128
HG_FDIM = HG_HEADS * HG_DK
HG_CHUNK = 128
HG_LEVELS = 7
HG_ROWS = 256
HG_HB = 4

SW_HEADS = 16
SW_KV = 4
SW_GROUP = 4
SW_DH = 64
SW_BLOCK = 128

N_EXPERTS = 8
LANES = 128

VMEM_LIMIT = 56 * 1024 * 1024


def _cparams(sem):
    return pltpu.CompilerParams(dimension_semantics=sem, vmem_limit_bytes=VMEM_LIMIT)


def _sigmoid(x):
    return 1.0 / (1.0 + jnp.exp2(x * -LOG2E))


def _mod_row(i, nct, lt, nb):
    return jnp.where(i < nct, nb, (i - nct) // lt)


def _mod_spec(k, nct, lt, nb):
    return pl.BlockSpec((None, 1, D), lambda i, *_: (_mod_row(i, nct, lt, nb) * 6 + k, 0, 0))


def _modulate(x, shift, scale):
    ms = jnp.mean(x * x, axis=-1, keepdims=True)
    return (x * lax.rsqrt(ms + EPS)) * (1.0 + scale) + shift


def _modvec_kernel(c_ref, w_ref, b_ref, o_ref):
    c = c_ref[...]
    s = c * _sigmoid(c)
    o_ref[...] = jnp.dot(s, w_ref[...], precision=lax.Precision.HIGHEST,
                         preferred_element_type=jnp.float32) + b_ref[...]


def _modvecs(cpad, w_mod, b_mod):
    depth = w_mod.shape[0]
    tn = 1024
    out = pl.pallas_call(
        _modvec_kernel,
        out_shape=jax.ShapeDtypeStruct((depth, 8, 6 * D), jnp.float32),
        grid=(depth, 6 * D // tn),
        in_specs=[pl.BlockSpec((8, D), lambda l, j: (0, 0)),
                  pl.BlockSpec((None, D, tn), lambda l, j: (l, 0, j)),
                  pl.BlockSpec((None, 1, tn), lambda l, j: (l, 0, j))],
        out_specs=pl.BlockSpec((None, 8, tn), lambda l, j: (l, 0, j)),
        name="adaln_vectors",
        compiler_params=_cparams(("arbitrary", "arbitrary")),
    )(cpad, w_mod, b_mod.reshape(depth, 1, 6 * D))
    return out.reshape(depth, 8 * 6, 1, D)


def _modmm_kernel(x_ref, sh_ref, sc_ref, w_ref, o_ref, h_ref):
    @pl.when(pl.program_id(1) == 0)
    def _():
        h_ref[...] = _modulate(x_ref[...], sh_ref[...], sc_ref[...]).astype(jnp.bfloat16)

    o_ref[...] = jnp.dot(h_ref[...], w_ref[...], preferred_element_type=jnp.float32).astype(o_ref.dtype)


def _mod_matmul(x, mods, w, *, k_shift, geom, tm, tn, out_dtype):
    nb, nct_rows, seq = geom
    t, n = x.shape[0], w.shape[1]
    nct, lt = nct_rows // tm, seq // tm
    return pl.pallas_call(
        _modmm_kernel,
        out_shape=jax.ShapeDtypeStruct((t, n), out_dtype),
        grid=(t // tm, n // tn),
        in_specs=[pl.BlockSpec((tm, D), lambda i, j: (i, 0)),
                  _mod_spec(k_shift, nct, lt, nb),
                  _mod_spec(k_shift + 1, nct, lt, nb),
                  pl.BlockSpec((D, tn), lambda i, j: (0, j))],
        out_specs=pl.BlockSpec((tm, tn), lambda i, j: (i, j)),
        scratch_shapes=[pltpu.VMEM((tm, D), jnp.bfloat16)],
        name="hg_in_proj",
        compiler_params=_cparams(("arbitrary", "arbitrary")),
    )(x, mods, mods, w)


def _hg_tables(fwd):
    c = HG_CHUNK
    t = np.arange(c)[:, None]
    u = np.arange(c)[None, :]
    reach = (u <= t) if fwd else (u >= t)
    x = t ^ u
    lvl = np.where(x > 0, np.floor(np.log2(np.maximum(x, 1))), HG_LEVELS)
    lvl = np.where(reach, lvl, -1).astype(np.int32)
    tri = np.concatenate([reach.astype(np.float32)] * 3, axis=1)
    return tri, lvl


def _hg_gates(q_raw, z, lb, tri_ref, a_ref):
    bf = jnp.bfloat16
    q = q_raw * _sigmoid(q_raw)
    sig = _sigmoid(z)
    fc = jnp.maximum(lb + (1.0 - lb) * sig, 1e-30)
    lf = jnp.log(fc) * LOG2E
    k = (1.0 - lb) * (1.0 - sig)
    hi = lf.astype(bf)
    rest = lf - hi.astype(jnp.float32)
    mid = rest.astype(bf)
    lo = (rest - mid.astype(jnp.float32)).astype(bf)
    a = jnp.dot(tri_ref[...], jnp.concatenate([hi, mid, lo], axis=0), preferred_element_type=jnp.float32)
    a_ref[...] = a
    return q, k, fc, a


def _hg_scores(q, k, fc, a, a_ref, lvl, fwd):
    c = HG_CHUNK
    bf = jnp.bfloat16

    def rows_of(row, n):
        return jnp.broadcast_to(a_ref[row:row + 1, :], (n, HG_DK))

    rowi = lax.broadcasted_iota(jnp.int32, (c, HG_DK), 0)
    nt = (((1,), (1,)), ((), ()))
    zeros8 = jnp.zeros((8, HG_DK), jnp.float32)

    def level_operands(l):
        w = 1 << l
        near = w - 1 if fwd else w
        if w < 8:
            upper = ((rowi >> l) & 1) == 1
            q_side = upper if fwd else jnp.logical_not(upper)
            if l == 0:
                qe, ke = q * fc, k
            else:
                if l == 1:
                    first = jnp.concatenate([rows_of(8 * g + near, 8) for g in range(c // 8)], axis=0)
                    second = jnp.concatenate([rows_of(8 * g + 4 + near, 8) for g in range(c // 8)], axis=0)
                    bnd = jnp.where((rowi & 4) == 0, first, second)
                else:
                    bnd = jnp.concatenate([rows_of(8 * g + near, 8) for g in range(c // 8)], axis=0)
                e = jnp.exp2(-jnp.abs(a - bnd))
                qe, ke = q * e, k * e
            return jnp.where(q_side, qe, 0.0).astype(bf), jnp.where(q_side, 0.0, ke).astype(bf)
        qp, kp = [], []
        for j in range(c // w):
            rs = slice(j * w, (j + 1) * w)
            bnd = rows_of((j // 2) * 2 * w + near, w)
            zero = jnp.concatenate([zeros8] * (w // 8), axis=0)
            if (j % 2 == 1) == fwd:
                qp.append(q[rs] * jnp.exp2(a[rs] - bnd))
                kp.append(zero)
            else:
                qp.append(zero)
                kp.append(k[rs] * jnp.exp2(bnd - a[rs]))
        return jnp.concatenate(qp, axis=0).astype(bf), jnp.concatenate(kp, axis=0).astype(bf)

    scores = jnp.where(lvl == HG_LEVELS,
                       lax.dot_general(q.astype(bf), k.astype(bf), nt, preferred_element_type=jnp.float32), 0.0)
    for l in range(HG_LEVELS):
        qm, km = level_operands(l)
        sc = lax.dot_general(qm, km, nt, preferred_element_type=jnp.float32)
        scores = jnp.where(lvl == l, sc, scores)
    return scores.astype(bf)


def _hg_finish(q, k, v, a, scores, st_ref, fwd):
    c = HG_CHUNK
    bf = jnp.bfloat16
    nt = (((1,), (1,)), ((), ()))
    last = c - 1 if fwd else 0
    e_in = jnp.exp2(a)
    e_out = jnp.exp2(a[last:last + 1, :] - a)
    st = st_ref[...]
    o = (jnp.dot(scores, v.astype(bf), preferred_element_type=jnp.float32)
         + lax.dot_general((q * e_in).astype(bf), st.astype(bf), nt, preferred_element_type=jnp.float32))
    st_ref[...] = (e_in[last:last + 1, :] * st
                   + jnp.dot(v.T.astype(bf), (k * e_out).astype(bf), preferred_element_type=jnp.float32))
    return o


def _hg_scan_kernel(qf_ref, vf_ref, zf_ref, qb_ref, vb_ref, zb_ref, lbf_ref, lbb_ref, trif_ref, trib_ref,
                    lvlf_ref, lvlb_ref, of_ref, ob_ref, sf_ref, sb_ref, a_ref):
    @pl.when(pl.program_id(2) == 0)
    def _():
        sf_ref[...] = jnp.zeros_like(sf_ref)
        sb_ref[...] = jnp.zeros_like(sb_ref)

    c = HG_CHUNK
    nch = HG_ROWS // c
    lvlf, lvlb = lvlf_ref[...], lvlb_ref[...]
    units = []
    for hh in range(HG_HB):
        cs = slice(hh * HG_DK, (hh + 1) * HG_DK)
        for step, ci in enumerate(range(nch)):
            units.append(dict(step=step, rows=slice(ci * c, (ci + 1) * c), cols=cs, fwd=True, q=qf_ref, v=vf_ref,
                              z=zf_ref, lb=lbf_ref, tri=trif_ref, lvl=lvlf, st=sf_ref.at[hh], out=of_ref))
        for step, ci in enumerate(reversed(range(nch))):
            units.append(dict(step=step, rows=slice(ci * c, (ci + 1) * c), cols=cs, fwd=False, q=qb_ref, v=vb_ref,
                              z=zb_ref, lb=lbb_ref, tri=trib_ref, lvl=lvlb, st=sb_ref.at[hh], out=ob_ref))
    for n, u in enumerate(units):
        u["a_ref"] = a_ref.at[n]
        u["q"], u["k"], u["fc"], u["a"] = _hg_gates(u["q"][u["rows"], u["cols"]], u["z"][u["rows"], u["cols"]],
                                                    u["lb"][:, u["cols"]], u["tri"], u["a_ref"])
    for u in units:
        u["scores"] = _hg_scores(u["q"], u["k"], u["fc"], u["a"], u["a_ref"], u["lvl"], u["fwd"])
    for step in range(nch):
        for u in units:
            if u["step"] == step:
                o = _hg_finish(u["q"], u["k"], u["v"][u["rows"], u["cols"]], u["a"], u["scores"], u["st"], u["fwd"])
                u["out"][u["rows"], u["cols"]] = o.astype(u["out"].dtype)


def _hg_scan(p, lbf, lbb, geom):
    nb, nct_rows, seq = geom
    t = p.shape[0]
    r = HG_ROWS
    cb, lb_ = (nct_rows // nb) // r, seq // r
    nsteps = cb + lb_
    lat0 = nct_rows // r

    def fblk(b, s):
        return jnp.where(s < cb, b * cb + s, lat0 + b * lb_ + (s - cb))

    def bblk(b, s):
        return jnp.where(s < cb, b * cb + (cb - 1 - s), lat0 + b * lb_ + (lb_ - 1 - (s - cb)))

    wcol = HG_HB * HG_DK
    ngrp = HG_HEADS // HG_HB
    units = 2 * HG_HB * (r // HG_CHUNK)

    def spec(blk, colblock):
        return pl.BlockSpec((r, wcol), lambda b, h, s: (blk(b, s), colblock * ngrp + h))

    lbspec = pl.BlockSpec((1, wcol), lambda b, h, s: (0, h))
    const = lambda shape: pl.BlockSpec(shape, lambda b, h, s: (0, 0))
    trif, lvlf = _hg_tables(True)
    trib, lvlb = _hg_tables(False)
    return pl.pallas_call(
        _hg_scan_kernel,
        out_shape=(jax.ShapeDtypeStruct((t, D), jnp.bfloat16), jax.ShapeDtypeStruct((t, D), jnp.bfloat16)),
        grid=(nb, ngrp, nsteps),
        in_specs=[spec(fblk, 0), spec(fblk, 1), spec(fblk, 2),
                  spec(bblk, 0), spec(bblk, 1), spec(bblk, 3),
                  lbspec, lbspec, const(trif.shape), const(trib.shape), const(lvlf.shape), const(lvlb.shape)],
        out_specs=(pl.BlockSpec((r, wcol), lambda b, h, s: (fblk(b, s), h)),
                   pl.BlockSpec((r, wcol), lambda b, h, s: (bblk(b, s), h))),
        scratch_shapes=[pltpu.VMEM((HG_HB, HG_DK, HG_DK), jnp.float32),
                        pltpu.VMEM((HG_HB, HG_DK, HG_DK), jnp.float32),
                        pltpu.VMEM((units, HG_CHUNK, HG_DK), jnp.float32)],
        name="hg_scan",
        compiler_params=_cparams(("arbitrary", "arbitrary", "arbitrary")),
    )(p, p, p, p, p, p, lbf, lbb, jnp.asarray(trif, jnp.bfloat16), jnp.asarray(trib, jnp.bfloat16),
      jnp.asarray(lvlf), jnp.asarray(lvlb))


def _hg_out_kernel(of_ref, ob_ref, g_ref, nw_ref, w_ref, x_ref, gate_ref, o_ref, h_ref):
    @pl.when(pl.program_id(1) == 0)
    def _():
        nw = nw_ref[...]
        for h in range(HG_HEADS):
            cs = slice(h * HG_DK, (h + 1) * HG_DK)
            o = of_ref[:, cs].astype(jnp.float32) + ob_ref[:, cs].astype(jnp.float32)
            y = o * lax.rsqrt(jnp.mean(o * o, axis=-1, keepdims=True) + EPS) * nw
            g = g_ref[:, cs]
            h_ref[:, cs] = (y * (g * _sigmoid(g))).astype(jnp.bfloat16)

    acc = jnp.dot(h_ref[...], w_ref[...], preferred_element_type=jnp.float32)
    o_ref[...] = x_ref[...] + gate_ref[...] * acc


def _hg_out(of, ob, p, norm_w, w, x, mods, *, geom, tm, tn):
    nb, nct_rows, seq = geom
    t = x.shape[0]
    nct, lt = nct_rows // tm, seq // tm
    gate = pl.BlockSpec((None, 1, tn), lambda i, j: (_mod_row(i, nct, lt, nb) * 6 + 2, 0, j))
    return pl.pallas_call(
        _hg_out_kernel,
        out_shape=jax.ShapeDtypeStruct((t, D), jnp.float32),
        grid=(t // tm, D // tn),
        in_specs=[pl.BlockSpec((tm, D), lambda i, j: (i, 0)),
                  pl.BlockSpec((tm, D), lambda i, j: (i, 0)),
                  pl.BlockSpec((tm, D), lambda i, j: (i, 4)),
                  pl.BlockSpec((1, HG_DK), lambda i, j: (0, 0)),
                  pl.BlockSpec((D, tn), lambda i, j: (0, j)),
                  pl.BlockSpec((tm, tn), lambda i, j: (i, j)),
                  gate],
        out_specs=pl.BlockSpec((tm, tn), lambda i, j: (i, j)),
        scratch_shapes=[pltpu.VMEM((tm, D), jnp.bfloat16)],
        name="hg_out_proj",
        compiler_params=_cparams(("arbitrary", "arbitrary")),
    )(of, ob, p, norm_w, w, x, mods)


def _resmm_kernel(a_ref, w_ref, x_ref, gate_ref, o_ref):
    acc = jnp.dot(a_ref[...], w_ref[...], preferred_element_type=jnp.float32)
    o_ref[...] = x_ref[...] + gate_ref[...] * acc


def _res_matmul(a, w, x, mods, *, geom, tm, tn):
    nb, nct_rows, seq = geom
    rows = a.shape[0]
    t0 = (x.shape[0] - rows) // tm
    nct, lt = nct_rows // tm, seq // tm
    gate = pl.BlockSpec((None, 1, tn), lambda i, j: (_mod_row(i + t0, nct, lt, nb) * 6 + 2, 0, j))
    return pl.pallas_call(
        _resmm_kernel,
        out_shape=jax.ShapeDtypeStruct((rows, D), jnp.float32),
        grid=(rows // tm, D // tn),
        in_specs=[pl.BlockSpec((tm, a.shape[1]), lambda i, j: (i, 0)),
                  pl.BlockSpec((a.shape[1], tn), lambda i, j: (0, j)),
                  pl.BlockSpec((tm, tn), lambda i, j: (i + t0, j)),
                  gate],
        out_specs=pl.BlockSpec((tm, tn), lambda i, j: (i, j)),
        name="attn_out_proj",
        compiler_params=_cparams(("arbitrary", "arbitrary")),
    )(a, w, x, mods)


def _rope_tables(seq, tm):
    rows = seq // GRID_W
    row = np.repeat(np.arange(rows, dtype=np.float32), GRID_W)
    col = np.tile(np.arange(GRID_W, dtype=np.float32), rows)
    nf = SW_DH // 4
    inv = (ROPE_THETA ** (-np.arange(nf, dtype=np.float32) / nf)).astype(np.float32)
    ang_r = row[:, None] * inv
    ang_c = col[:, None] * inv
    cos = np.concatenate([np.cos(ang_r), np.cos(ang_r), np.cos(ang_c), np.cos(ang_c)], axis=1)
    sin = np.concatenate([-np.sin(ang_r), np.sin(ang_r), -np.sin(ang_c), np.sin(ang_c)], axis=1)
    cos = np.concatenate([np.tile(cos, (1, 2)), np.ones((tm, LANES), np.float32)], axis=0)
    sin = np.concatenate([np.tile(sin, (1, 2)), np.zeros((tm, LANES), np.float32)], axis=0)
    return jnp.asarray(cos, jnp.float32), jnp.asarray(sin, jnp.float32)


def _qkv_kernel(x_ref, sh_ref, sc_ref, w_ref, nw_ref, ones_ref, cos_ref, sin_ref, o_ref, h_ref):
    j = pl.program_id(1)

    @pl.when(j == 0)
    def _():
        h_ref[...] = _modulate(x_ref[...], sh_ref[...], sc_ref[...]).astype(jnp.bfloat16)

    acc = jnp.dot(h_ref[...], w_ref[...], preferred_element_type=jnp.float32)

    @pl.when(j == 2)
    def _():
        o_ref[...] = acc.astype(o_ref.dtype)

    @pl.when(j < 2)
    def _():
        ss = jnp.dot((acc * acc).astype(jnp.bfloat16), ones_ref[...], preferred_element_type=jnp.float32)
        y = acc * lax.rsqrt(ss * (1.0 / SW_DH) + EPS) * nw_ref[...]
        cos = cos_ref[...]
        sin = sin_ref[...]
        lane = lax.broadcasted_iota(jnp.int32, (y.shape[0], LANES), 1)
        first = (lane & 16) == 0
        scale = jnp.where(j == 0, SW_DH ** -0.5 * LOG2E, 1.0)
        for g in range(D // LANES):
            cs = slice(g * LANES, (g + 1) * LANES)
            yg = y[:, cs]
            partner = jnp.where(first, pltpu.roll(yg, LANES - 16, 1), pltpu.roll(yg, 16, 1))
            o_ref[:, cs] = ((yg * cos + partner * sin) * scale).astype(o_ref.dtype)


def _qkv_proj(x, mods, w3, nw2, *, geom, tm):
    nb, nct_rows, seq = geom
    t = x.shape[0]
    nct, lt = nct_rows // tm, seq // tm
    cos, sin = _rope_tables(seq, tm)
    blk = np.kron(np.eye(D // SW_DH, dtype=np.float32), np.ones((SW_DH, SW_DH), np.float32))
    ones = jnp.asarray(blk, jnp.bfloat16)
    tab = lambda i, j: (jnp.where(i < nct, lt, (i - nct) % lt), 0)
    return pl.pallas_call(
        _qkv_kernel,
        out_shape=jax.ShapeDtypeStruct((t, 3 * D), jnp.bfloat16),
        grid=(t // tm, 3),
        in_specs=[pl.BlockSpec((tm, D), lambda i, j: (i, 0)),
                  _mod_spec(0, nct, lt, nb),
                  _mod_spec(1, nct, lt, nb),
                  pl.BlockSpec((D, D), lambda i, j: (0, j)),
                  pl.BlockSpec((None, 1, D), lambda i, j: (jnp.minimum(j, 1), 0, 0)),
                  pl.BlockSpec((D, D), lambda i, j: (0, 0)),
                  pl.BlockSpec((tm, LANES), tab),
                  pl.BlockSpec((tm, LANES), tab)],
        out_specs=pl.BlockSpec((tm, D), lambda i, j: (i, j)),
        scratch_shapes=[pltpu.VMEM((tm, D), jnp.bfloat16)],
        name="qkv_proj",
        compiler_params=_cparams(("arbitrary", "arbitrary")),
    )(x, mods, mods, w3, nw2, ones, cos, sin)


def _attn_items(items, sink_ref, kv, o_ref):
    nt = (((1,), (1,)), ((), ()))
    work = [(it, g) for it in items for g in range(SW_GROUP)]
    scores = []
    for (_, q, kcat, _, _), g in work:
        grp = lax.broadcasted_iota(jnp.int32, q.shape, 1) // SW_DH
        scores.append(lax.dot_general(jnp.where(grp == g, q, jnp.zeros_like(q)), kcat, nt,
                                      preferred_element_type=jnp.float32))
    probs, denoms = [], []
    for ((_, _, _, _, valid), g), s in zip(work, scores):
        sink = sink_ref[kv * SW_GROUP + g] * LOG2E
        if valid is not None:
            s = jnp.where(valid, s, NEG_BIG)
        m = jnp.maximum(jnp.max(s, axis=-1, keepdims=True), sink)
        p = jnp.exp2(s - m)
        denoms.append(jnp.sum(p, axis=-1, keepdims=True) + jnp.exp2(sink - m))
        probs.append(p.astype(jnp.bfloat16))
    outs = {}
    for n, ((rows, q, _, vcat, _), g) in enumerate(work):
        grp = lax.broadcasted_iota(jnp.int32, q.shape, 1) // SW_DH
        og = jnp.dot(probs[n], vcat, preferred_element_type=jnp.float32) / denoms[n]
        outs[rows] = og if g == 0 else jnp.where(grp == g, og, outs[rows])
    for rows, out in outs.items():
        o_ref[rows[0]:rows[1], :] = out.astype(o_ref.dtype)


def _attn_kernel(sink_ref, q_ref, kp_ref, kc_ref, kn_ref, kx_ref, vp_ref, vc_ref, vn_ref, vx_ref, o_ref,
                 *, nstep, ctx_out):
    kv, j = pl.program_id(1), pl.program_id(2)
    blk = SW_BLOCK

    def latent():
        kx, vx = kx_ref[...], vx_ref[...]
        k0, k1, v0, v1 = kc_ref[0:blk, :], kc_ref[blk:2 * blk, :], vc_ref[0:blk, :], vc_ref[blk:2 * blk, :]
        ns = 3 * blk + kx.shape[0]
        t = lax.broadcasted_iota(jnp.int32, (blk, ns), 0)
        s = lax.broadcasted_iota(jnp.int32, (blk, ns), 1)

        def window(has_prev, has_next):
            lo = jnp.maximum(t, jnp.where(has_prev, 0, blk))
            hi = jnp.minimum(t + 2 * blk, jnp.where(has_next, 3 * blk - 1, 2 * blk - 1))
            return ((s >= lo) & (s <= hi)) | (s >= 3 * blk)

        items = [((0, blk), q_ref[0:blk, :], jnp.concatenate([kp_ref[...], k0, k1, kx], axis=0),
                  jnp.concatenate([vp_ref[...], v0, v1, vx], axis=0), window(j > 0, True)),
                 ((blk, 2 * blk), q_ref[blk:2 * blk, :], jnp.concatenate([k0, k1, kn_ref[...], kx], axis=0),
                  jnp.concatenate([v0, v1, vn_ref[...], vx], axis=0), window(True, j < nstep - 1))]
        _attn_items(items, sink_ref, kv, o_ref)

    if not ctx_out:
        latent()
        return
    pl.when(j < nstep)(latent)

    @pl.when(j >= nstep)
    def _():
        _attn_items([((0, 2 * blk), q_ref[...], kx_ref[...], vx_ref[...], None)], sink_ref, kv, o_ref)


def _attention(qkv, sink, geom, ctx_out):
    nb, nct_rows, seq = geom
    ctx_len = nct_rows // nb
    pair = 2 * SW_BLOCK
    assert ctx_len == pair and seq % pair == 0
    nblk = seq // SW_BLOCK
    nstep = seq // pair
    lat0 = nct_rows // SW_BLOCK
    w = SW_GROUP * SW_DH
    smem = pl.BlockSpec(memory_space=pltpu.SMEM)

    def qrow(b, j):
        return jnp.where(j < nstep, nct_rows // pair + b * nstep + j, b)

    def edge(off, colblock):
        return pl.BlockSpec((SW_BLOCK, w), lambda b, kv, j: (
            lat0 + b * nblk + jnp.clip(2 * jnp.minimum(j, nstep - 1) + off, 0, nblk - 1), colblock * SW_KV + kv))

    def center(colblock):
        return pl.BlockSpec((pair, w), lambda b, kv, j: (qrow(b, jnp.minimum(j, nstep - 1)), colblock * SW_KV + kv))

    def ctxkv(colblock):
        return pl.BlockSpec((ctx_len, w), lambda b, kv, j: (b, colblock * SW_KV + kv))

    if ctx_out:
        steps, out_rows = nstep + 1, qkv.shape[0]
        out_spec = pl.BlockSpec((pair, w), lambda b, kv, j: (qrow(b, j), kv))
    else:
        steps, out_rows = nstep, nb * seq
        out_spec = pl.BlockSpec((pair, w), lambda b, kv, j: (b * nstep + j, kv))
    return pl.pallas_call(
        functools.partial(_attn_kernel, nstep=nstep, ctx_out=ctx_out),
        out_shape=jax.ShapeDtypeStruct((out_rows, D), jnp.bfloat16),
        grid=(nb, SW_KV, steps),
        in_specs=[smem, pl.BlockSpec((pair, w), lambda b, kv, j: (qrow(b, j), kv)),
                  edge(-1, 1), center(1), edge(2, 1), ctxkv(1),
                  edge(-1, 2), center(2), edge(2, 2), ctxkv(2)],
        out_specs=out_spec,
        name="attention",
        compiler_params=_cparams(("arbitrary", "arbitrary", "arbitrary")),
    )(sink, qkv, qkv, qkv, qkv, qkv, qkv, qkv, qkv, qkv)


def _router_kernel(x_ref, sh_ref, sc_ref, r_ref, h_ref, info_ref):
    h = _modulate(x_ref[...], sh_ref[...], sc_ref[...])
    h_ref[...] = h
    logits = jnp.dot(h, r_ref[...], precision=lax.Precision.HIGHEST, preferred_element_type=jnp.float32)
    lane = lax.broadcasted_iota(jnp.int32, logits.shape, 1).astype(jnp.float32)
    logits = jnp.where(lane < N_EXPERTS, logits, -jnp.inf)
    l1 = jnp.max(logits, axis=-1, keepdims=True)
    i1 = jnp.min(jnp.where(logits == l1, lane, float(LANES)), axis=-1, keepdims=True)
    rest = jnp.where(lane == i1, -jnp.inf, logits)
    l2 = jnp.max(rest, axis=-1, keepdims=True)
    i2 = jnp.min(jnp.where(rest == l2, lane, float(LANES)), axis=-1, keepdims=True)
    w1 = 1.0 / (1.0 + jnp.exp(l2 - l1))
    info_ref[...] = jnp.where(lane == 0.0, i1, jnp.where(lane == 1.0, i2, jnp.where(lane == 2.0, w1, 1.0 - w1)))


def _router(x, mods, router_pad, *, geom, tm):
    nb, nct_rows, seq = geom
    t = x.shape[0]
    nct, lt = nct_rows // tm, seq // tm
    return pl.pallas_call(
        _router_kernel,
        out_shape=(jax.ShapeDtypeStruct((t, D), jnp.float32), jax.ShapeDtypeStruct((t, LANES), jnp.float32)),
        grid=(t // tm,),
        in_specs=[pl.BlockSpec((tm, D), lambda i: (i, 0)), _mod_spec(3, nct, lt, nb), _mod_spec(4, nct, lt, nb),
                  pl.BlockSpec((D, LANES), lambda i: (0, 0))],
        out_specs=(pl.BlockSpec((tm, D), lambda i: (i, 0)), pl.BlockSpec((tm, LANES), lambda i: (i, 0))),
        name="moe_router",
        compiler_params=_cparams(("arbitrary",)),
    )(x, mods, mods, router_pad)


def _swiglu_step(h, wg_ref, wu_ref, wd_ref, row_scale):
    g = jnp.dot(h, wg_ref[...], preferred_element_type=jnp.float32)
    u = jnp.dot(h, wu_ref[...], preferred_element_type=jnp.float32)
    a = g * _sigmoid(g) * u
    if row_scale is not None:
        a = a * row_scale
    return jnp.dot(a.astype(jnp.bfloat16), wd_ref[...], preferred_element_type=jnp.float32)


def _ffn_kernel(x_ref, sh_ref, sc_ref, gate_ref, wg_ref, wu_ref, wd_ref, o_ref, h_ref, acc_ref, *, nf):
    f = pl.program_id(1)

    @pl.when(f == 0)
    def _():
        h_ref[...] = _modulate(x_ref[...], sh_ref[...], sc_ref[...]).astype(jnp.bfloat16)

    y = _swiglu_step(h_ref[...], wg_ref, wu_ref, wd_ref, None)
    if nf == 1:
        o_ref[...] = x_ref[...] + gate_ref[...] * y
        return

    @pl.when(f == 0)
    def _():
        acc_ref[...] = y

    @pl.when((f > 0) & (f < nf - 1))
    def _():
        acc_ref[...] += y

    @pl.when(f == nf - 1)
    def _():
        o_ref[...] = x_ref[...] + gate_ref[...] * (acc_ref[...] + y)


def _ffn(x, mods, wgu, wd, *, geom, tm, tf):
    nb, nct_rows, seq = geom
    t = x.shape[0]
    nf = wd.shape[0] // tf
    nct, lt = nct_rows // tm, seq // tm
    return pl.pallas_call(
        functools.partial(_ffn_kernel, nf=nf),
        out_shape=jax.ShapeDtypeStruct((t, D), jnp.float32),
        grid=(t // tm, nf),
        in_specs=[pl.BlockSpec((tm, D), lambda i, f: (i, 0)),
                  _mod_spec(3, nct, lt, nb),
                  _mod_spec(4, nct, lt, nb),
                  _mod_spec(5, nct, lt, nb),
                  pl.BlockSpec((D, tf), lambda i, f: (0, f)),
                  pl.BlockSpec((D, tf), lambda i, f: (0, nf + f)),
                  pl.BlockSpec((tf, D), lambda i, f: (f, 0))],
        out_specs=pl.BlockSpec((tm, D), lambda i, f: (i, 0)),
        scratch_shapes=[pltpu.VMEM((tm, D), jnp.bfloat16), pltpu.VMEM((tm, D), jnp.float32)],
        name="dense_ffn",
        compiler_params=_cparams(("arbitrary", "arbitrary")),
    )(x, mods, mods, mods, wgu, wgu, wd)


MOE_SUB = 512


def _moe_kernel(te_ref, nt_ref, rows0_ref, rows1_ref, rows2_ref, outrows_ref, h_hbm, wgu_ref, wd_ref, y_hbm,
                hbuf, hb_ref, obuf, gsem, ssem, *, tm):
    del te_ref
    i = pl.program_id(0)
    nt = nt_ref[0]
    g_slot, o_slot = i % 3, i % 2
    fdim = wd_ref.shape[0]

    def gather(rows_ref, s):
        for r in range(tm):
            pltpu.make_async_copy(h_hbm.at[pl.ds(rows_ref[0, r], 1), :], hbuf.at[s, pl.ds(r, 1), :],
                                  gsem.at[s]).start()

    def scatter_prev():
        for r in range(tm):
            pltpu.make_async_copy(obuf.at[1 - o_slot, pl.ds(r, 1), :], y_hbm.at[pl.ds(outrows_ref[0, r], 1), :],
                                  ssem.at[1 - o_slot]).start()

    @pl.when(i == 0)
    def _():
        gather(rows0_ref, 0)
        gather(rows1_ref, 1)
        obuf[1] = jnp.zeros((tm, D), jnp.float32)

    @pl.when((i >= 1) & (i - 2 < nt))
    def _():
        pltpu.make_async_copy(obuf.at[o_slot], obuf.at[o_slot], ssem.at[o_slot]).wait()

    @pl.when(i < nt + 2)
    def _():
        pltpu.make_async_copy(hbuf.at[g_slot], hbuf.at[g_slot], gsem.at[g_slot]).wait()
        hb_ref[...] = hbuf[g_slot].astype(jnp.bfloat16)

    @pl.when(i < nt)
    def _():
        gather(rows2_ref, (i + 2) % 3)
        scatter_prev()
        h = hb_ref[...]
        nsub = fdim // MOE_SUB
        up = lambda s: (jnp.dot(h, wgu_ref[:, s * MOE_SUB:(s + 1) * MOE_SUB], preferred_element_type=jnp.float32),
                        jnp.dot(h, wgu_ref[:, fdim + s * MOE_SUB:fdim + (s + 1) * MOE_SUB],
                                preferred_element_type=jnp.float32))
        down = lambda gu, s: jnp.dot((gu[0] * _sigmoid(gu[0]) * gu[1]).astype(jnp.bfloat16),
                                     wd_ref[s * MOE_SUB:(s + 1) * MOE_SUB, :], preferred_element_type=jnp.float32)
        gu = up(0)
        acc = None
        for s in range(1, nsub):
            gu_next = up(s)
            y = down(gu, s - 1)
            acc = y if acc is None else acc + y
            gu = gu_next
        y = down(gu, nsub - 1)
        obuf[o_slot] = y if acc is None else acc + y

    @pl.when(i == nt)
    def _():
        scatter_prev()


def _moe_experts(h, tile_expert, n_tiles, rows, outrows, wgu, wd, n_out, *, tm):
    nt = rows.shape[0]
    fdim = wd.shape[1]
    assert fdim % MOE_SUB == 0 and nt >= 3
    smem_rows = lambda off: pl.BlockSpec((None, 1, tm), lambda i, te, n: (jnp.minimum(i + off, nt - 1), 0, 0),
                                         memory_space=pltpu.SMEM)
    once = pl.Buffered(1)
    grid_spec = pltpu.PrefetchScalarGridSpec(
        num_scalar_prefetch=2,
        grid=(nt,),
        in_specs=[smem_rows(0), smem_rows(1), smem_rows(2),
                  pl.BlockSpec((None, 1, tm), lambda i, te, n: (i, 0, 0), memory_space=pltpu.SMEM),
                  pl.BlockSpec(memory_space=pl.ANY),
                  pl.BlockSpec((None, D, 2 * fdim), lambda i, te, n: (te[i], 0, 0), pipeline_mode=once),
                  pl.BlockSpec((None, fdim, D), lambda i, te, n: (te[i], 0, 0), pipeline_mode=once)],
        out_specs=pl.BlockSpec(memory_space=pl.ANY),
        scratch_shapes=[pltpu.VMEM((3, tm, D), jnp.float32), pltpu.VMEM((tm, D), jnp.bfloat16),
                        pltpu.VMEM((2, tm, D), jnp.float32), pltpu.SemaphoreType.DMA((3,)),
                        pltpu.SemaphoreType.DMA((2,))])
    return pl.pallas_call(
        functools.partial(_moe_kernel, tm=tm),
        out_shape=jax.ShapeDtypeStruct((n_out, D), jnp.float32),
        grid_spec=grid_spec,
        name="moe_experts",
        compiler_params=_cparams(("arbitrary",)),
    )(tile_expert, n_tiles, rows, rows, rows, outrows, h, wgu, wd)


def _combine_kernel(y0_ref, y1_ref, x_ref, gate_ref, info_ref, o_ref):
    info = info_ref[...]
    y = info[:, 2:3] * y0_ref[...] + info[:, 3:4] * y1_ref[...]
    o_ref[...] = x_ref[...] + gate_ref[...] * y


def _moe_combine(x, y2, info, mods, *, geom, tmc):
    nb, nct_rows, seq = geom
    t = x.shape[0]
    nct, lt = nct_rows // tmc, seq // tmc
    return pl.pallas_call(
        _combine_kernel,
        out_shape=jax.ShapeDtypeStruct((t, D), jnp.float32),
        grid=(t // tmc,),
        in_specs=[pl.BlockSpec((None, tmc, D), lambda i: (0, i, 0)),
                  pl.BlockSpec((None, tmc, D), lambda i: (1, i, 0)),
                  pl.BlockSpec((tmc, D), lambda i: (i, 0)),
                  _mod_spec(5, nct, lt, nb),
                  pl.BlockSpec((tmc, LANES), lambda i: (i, 0))],
        out_specs=pl.BlockSpec((tmc, D), lambda i: (i, 0)),
        name="moe_combine",
        compiler_params=_cparams(("arbitrary",)),
    )(y2, y2, x, mods, info)


def _moe(x, mods, router_pad, wgu, wd, *, geom, tm, tmc):
    tr = x.shape[0]
    h, info = _router(x, mods, router_pad, geom=geom, tm=tm)
    e = info[:, 0:2].astype(jnp.int32).reshape(-1)
    onehot = (e[:, None] == jnp.arange(N_EXPERTS, dtype=jnp.int32)[None, :]).astype(jnp.int32)
    csum = jnp.cumsum(onehot, axis=0)
    rank = jnp.sum(onehot * (csum - 1), axis=1)
    counts = csum[-1]
    padded = ((counts + tm - 1) // tm) * tm
    ends = jnp.cumsum(padded)
    dest = (ends - padded)[e] + rank
    nt = 2 * tr // tm + N_EXPERTS + 2
    tile_expert = jnp.minimum(jnp.sum(jnp.arange(nt, dtype=jnp.int32)[:, None] * tm >= ends[None, :], axis=1),
                              N_EXPERTS - 1).astype(jnp.int32)
    n_tiles = (ends[-1:] // tm).astype(jnp.int32)
    slot_of_row = jnp.full((nt * tm,), -1, jnp.int32).at[dest].set(jnp.arange(2 * tr, dtype=jnp.int32),
                                                                   unique_indices=True)
    tp = tr + tm // 2
    j = jnp.arange(nt * tm, dtype=jnp.int32) % tm
    spare = jnp.where(j < tm // 2, tr + j, tp + tr + j - tm // 2)
    out_row = (slot_of_row % 2) * tp + slot_of_row // 2
    rows = (jnp.maximum(slot_of_row, 0) // 2).reshape(nt, 1, tm)
    outrows = jnp.concatenate([spare[:tm], jnp.where(slot_of_row >= 0, out_row, spare)]).reshape(nt + 1, 1, tm)
    y = _moe_experts(h, tile_expert, n_tiles, rows, outrows, wgu, wd, 2 * tp, tm=tm)
    return _moe_combine(x, y.reshape(2, tp, D), info, mods, geom=geom, tmc=tmc)


def kernel(x, c, ctx, c_ctx, w_mod, b_mod, hg_w_in, hg_lb_logits, hg_norm_w, hg_w_out, sw_w_qkv, sw_q_norm,
           sw_k_norm, sw_sink, sw_w_out, ff_w_gate_up, ff_w_down, moe_router, moe_w_gate_up, moe_w_down):
    nb, seq, _ = x.shape
    ctx_len = ctx.shape[1]
    depth = w_mod.shape[0]
    nct_rows = nb * ctx_len
    geom = (nb, nct_rows, seq)
    tm = 512
    tmb = 1024 if nct_rows % 1024 == 0 and seq % 1024 == 0 else tm
    bf = jnp.bfloat16

    xs = jnp.concatenate([ctx.reshape(nct_rows, D), x.reshape(nb * seq, D)], axis=0)
    cpad = jnp.concatenate([c, c_ctx[None, :], jnp.zeros((8 - nb - 1, D), jnp.float32)], axis=0)
    mods_all = _modvecs(cpad, w_mod, b_mod)

    p_lb = jax.nn.softmax(hg_lb_logits.astype(jnp.float32), axis=0)
    lower_bounds = jnp.cumsum(p_lb, axis=0) - p_lb[:1]

    for i in range(depth):
        ctx_live = i < depth - 1
        mods = mods_all[i]
        j = i // 2
        if i % 2 == 0:
            p = _mod_matmul(xs, mods, hg_w_in[j].astype(bf), k_shift=0, geom=geom, tm=tmb, tn=1024,
                            out_dtype=jnp.float32)
            of, ob = _hg_scan(p, lower_bounds[j, 0:1], lower_bounds[j, 1:2], geom)
            xs = _hg_out(of, ob, p, hg_norm_w[j][None, :], hg_w_out[j].astype(bf), xs, mods,
                         geom=geom, tm=tmb, tn=1024)
        else:
            wq, wk, wv = jnp.split(sw_w_qkv[j], [D, D + SW_KV * SW_DH], axis=1)
            rep = lambda w: jnp.repeat(w.reshape(D, SW_KV, 1, SW_DH), SW_GROUP, axis=2).reshape(D, D)
            w3 = jnp.concatenate([wq, rep(wk), rep(wv)], axis=1).astype(bf)
            nw2 = jnp.stack([jnp.tile(sw_q_norm[j], SW_HEADS), jnp.tile(sw_k_norm[j], SW_HEADS)])[:, None, :]
            qkv = _qkv_proj(xs, mods, w3, nw2, geom=geom, tm=tmb)
            o = _attention(qkv, sw_sink[j], geom, ctx_live)
            xs = _res_matmul(o, sw_w_out[j].astype(bf), xs, mods, geom=geom, tm=tmb, tn=1024)
            if not ctx_live:
                geom = (nb, 0, seq)
        if i % 2 == 0:
            xs = _ffn(xs, mods, ff_w_gate_up[j].astype(bf), ff_w_down[j].astype(bf), geom=geom, tm=tm, tf=1408)
        else:
            rpad = jnp.pad(moe_router[j], ((0, 0), (0, LANES - N_EXPERTS)))
            xs = _moe(xs, mods, rpad, moe_w_gate_up[j].astype(bf), moe_w_down[j].astype(bf),
                      geom=geom, tm=tm, tmc=tm)
    return xs[xs.shape[0] - nb * seq:].reshape(nb, seq, D)
```

```python
import functools

import numpy as np
import jax
import jax.numpy as jnp
from jax import lax
from jax.experimental import pallas as pl
from jax.experimental.pallas import tpu as pltpu

D = 1024
EPS = 1e-6
NEG_BIG = -1e30
LOG2E = 1.4426950408889634
GRID_W = 64
ROPE_THETA = 10000.0

HG_HEADS = 8
HG_DK = 128
HG_FDIM = HG_HEADS * HG_DK
HG_CHUNK = 128
HG_LEVELS = 7
HG_ROWS = 256
HG_HB = 4

SW_HEADS = 16
SW_KV = 4
SW_GROUP = 4
SW_DH = 64
SW_BLOCK = 128

N_EXPERTS = 8
LANES = 128

VMEM_LIMIT = 56 * 1024 * 1024


def _cparams(sem):
    return pltpu.CompilerParams(dimension_semantics=sem, vmem_limit_bytes=VMEM_LIMIT)


def _sigmoid(x):
    return 1.0 / (1.0 + jnp.exp2(x * -LOG2E))


def _mod_row(i, nct, lt, nb):
    return jnp.where(i < nct, nb, (i - nct) // lt)


def _mod_spec(k, nct, lt, nb):
    return pl.BlockSpec((None, 1, D), lambda i, *_: (_mod_row(i, nct, lt, nb) * 6 + k, 0, 0))


def _modulate(x, shift, scale):
    ms = jnp.mean(x * x, axis=-1, keepdims=True)
    return (x * lax.rsqrt(ms + EPS)) * (1.0 + scale) + shift


def _modvec_kernel(c_ref, w_ref, b_ref, o_ref):
    c = c_ref[...]
    s = c * _sigmoid(c)
    o_ref[...] = jnp.dot(s, w_ref[...], precision=lax.Precision.HIGHEST,
                         preferred_element_type=jnp.float32) + b_ref[...]


def _modvecs(cpad, w_mod, b_mod):
    depth = w_mod.shape[0]
    tn = 1024
    out = pl.pallas_call(
        _modvec_kernel,
        out_shape=jax.ShapeDtypeStruct((depth, 8, 6 * D), jnp.float32),
        grid=(depth, 6 * D // tn),
        in_specs=[pl.BlockSpec((8, D), lambda l, j: (0, 0)),
                  pl.BlockSpec((None, D, tn), lambda l, j: (l, 0, j)),
                  pl.BlockSpec((None, 1, tn), lambda l, j: (l, 0, j))],
        out_specs=pl.BlockSpec((None, 8, tn), lambda l, j: (l, 0, j)),
        name="adaln_vectors",
        compiler_params=_cparams(("arbitrary", "arbitrary")),
    )(cpad, w_mod, b_mod.reshape(depth, 1, 6 * D))
    return out.reshape(depth, 8 * 6, 1, D)


def _modmm_kernel(x_ref, sh_ref, sc_ref, w_ref, o_ref, *, tn):
    h = _modulate(x_ref[...], sh_ref[...], sc_ref[...]).astype(jnp.bfloat16)
    for n0 in range(0, w_ref.shape[1], tn):
        o_ref[:, n0:n0 + tn] = jnp.dot(h, w_ref[:, n0:n0 + tn], preferred_element_type=jnp.float32).astype(o_ref.dtype)


def _mod_matmul(x, mods, w, *, k_shift, geom, tm, tn, out_dtype):
    nb, nct_rows, seq = geom
    t, n = x.shape[0], w.shape[1]
    nct, lt = nct_rows // tm, seq // tm
    return pl.pallas_call(
        functools.partial(_modmm_kernel, tn=tn),
        out_shape=jax.ShapeDtypeStruct((t, n), out_dtype),
        grid=(t // tm,),
        in_specs=[pl.BlockSpec((tm, D), lambda i: (i, 0)),
                  _mod_spec(k_shift, nct, lt, nb),
                  _mod_spec(k_shift + 1, nct, lt, nb),
                  pl.BlockSpec(w.shape, lambda i: (0, 0), pipeline_mode=pl.Buffered(1))],
        out_specs=pl.BlockSpec((tm, n), lambda i: (i, 0)),
        name="hg_in_proj",
        compiler_params=_cparams(("arbitrary",)),
    )(x, mods, mods, w)


def _hg_tables(fwd):
    c = HG_CHUNK
    t = np.arange(c)[:, None]
    u = np.arange(c)[None, :]
    reach = (u <= t) if fwd else (u >= t)
    x = t ^ u
    lvl = np.where(x > 0, np.floor(np.log2(np.maximum(x, 1))), HG_LEVELS)
    lvl = np.where(reach, lvl, -1).astype(np.int32)
    tri = np.concatenate([reach.astype(np.float32)] * 3, axis=1)
    return tri, lvl


def _hg_gates(q_raw, z, lb, tri_ref, a_ref):
    bf = jnp.bfloat16
    q = q_raw * _sigmoid(q_raw)
    sig = _sigmoid(z)
    fc = jnp.maximum(lb + (1.0 - lb) * sig, 1e-30)
    lf = jnp.log(fc) * LOG2E
    k = (1.0 - lb) * (1.0 - sig)
    hi = lf.astype(bf)
    rest = lf - hi.astype(jnp.float32)
    mid = rest.astype(bf)
    lo = (rest - mid.astype(jnp.float32)).astype(bf)
    a = jnp.dot(tri_ref[...], jnp.concatenate([hi, mid, lo], axis=0), preferred_element_type=jnp.float32)
    a_ref[...] = a
    return q, k, fc, a


def _hg_scores(q, k, fc, a, a_ref, lvl, fwd):
    c = HG_CHUNK
    bf = jnp.bfloat16

    def rows_of(row, n):
        return jnp.broadcast_to(a_ref[row:row + 1, :], (n, HG_DK))

    rowi = lax.broadcasted_iota(jnp.int32, (c, HG_DK), 0)
    nt = (((1,), (1,)), ((), ()))
    zeros8 = jnp.zeros((8, HG_DK), jnp.float32)

    def level_operands(l):
        w = 1 << l
        near = w - 1 if fwd else w
        if w < 8:
            upper = ((rowi >> l) & 1) == 1
            q_side = upper if fwd else jnp.logical_not(upper)
            if l == 0:
                qe, ke = q * fc, k
            else:
                if l == 1:
                    first = jnp.concatenate([rows_of(8 * g + near, 8) for g in range(c // 8)], axis=0)
                    second = jnp.concatenate([rows_of(8 * g + 4 + near, 8) for g in range(c // 8)], axis=0)
                    bnd = jnp.where((rowi & 4) == 0, first, second)
                else:
                    bnd = jnp.concatenate([rows_of(8 * g + near, 8) for g in range(c // 8)], axis=0)
                e = jnp.exp2(-jnp.abs(a - bnd))
                qe, ke = q * e, k * e
            return jnp.where(q_side, qe, 0.0).astype(bf), jnp.where(q_side, 0.0, ke).astype(bf)
        qp, kp = [], []
        for j in range(c // w):
            rs = slice(j * w, (j + 1) * w)
            bnd = rows_of((j // 2) * 2 * w + near, w)
            zero = jnp.concatenate([zeros8] * (w // 8), axis=0)
            if (j % 2 == 1) == fwd:
                qp.append(q[rs] * jnp.exp2(a[rs] - bnd))
                kp.append(zero)
            else:
                qp.append(zero)
                kp.append(k[rs] * jnp.exp2(bnd - a[rs]))
        return jnp.concatenate(qp, axis=0).astype(bf), jnp.concatenate(kp, axis=0).astype(bf)

    scores = jnp.where(lvl == HG_LEVELS,
                       lax.dot_general(q.astype(bf), k.astype(bf), nt, preferred_element_type=jnp.float32), 0.0)
    for l in range(HG_LEVELS):
        qm, km = level_operands(l)
        sc = lax.dot_general(qm, km, nt, preferred_element_type=jnp.float32)
        scores = jnp.where(lvl == l, sc, scores)
    return scores.astype(bf)


def _hg_finish(q, k, v, a, scores, st_ref, fwd):
    c = HG_CHUNK
    bf = jnp.bfloat16
    nt = (((1,), (1,)), ((), ()))
    last = c - 1 if fwd else 0
    e_in = jnp.exp2(a)
    e_out = jnp.exp2(a[last:last + 1, :] - a)
    st = st_ref[...]
    o = (jnp.dot(scores, v.astype(bf), preferred_element_type=jnp.float32)
         + lax.dot_general((q * e_in).astype(bf), st.astype(bf), nt, preferred_element_type=jnp.float32))
    st_ref[...] = (e_in[last:last + 1, :] * st
                   + jnp.dot(v.T.astype(bf), (k * e_out).astype(bf), preferred_element_type=jnp.float32))
    return o


def _hg_scan_kernel(qf_ref, vf_ref, zf_ref, qb_ref, vb_ref, zb_ref, lbf_ref, lbb_ref, trif_ref, trib_ref,
                    lvlf_ref, lvlb_ref, of_ref, ob_ref, sf_ref, sb_ref, a_ref):
    @pl.when(pl.program_id(2) == 0)
    def _():
        sf_ref[...] = jnp.zeros_like(sf_ref)
        sb_ref[...] = jnp.zeros_like(sb_ref)

    c = HG_CHUNK
    nch = HG_ROWS // c
    lvlf, lvlb = lvlf_ref[...], lvlb_ref[...]
    units = []
    for hh in range(HG_HB):
        cs = slice(hh * HG_DK, (hh + 1) * HG_DK)
        for step, ci in enumerate(range(nch)):
            units.append(dict(step=step, rows=slice(ci * c, (ci + 1) * c), cols=cs, fwd=True, q=qf_ref, v=vf_ref,
                              z=zf_ref, lb=lbf_ref, tri=trif_ref, lvl=lvlf, st=sf_ref.at[hh], out=of_ref))
        for step, ci in enumerate(reversed(range(nch))):
            units.append(dict(step=step, rows=slice(ci * c, (ci + 1) * c), cols=cs, fwd=False, q=qb_ref, v=vb_ref,
                              z=zb_ref, lb=lbb_ref, tri=trib_ref, lvl=lvlb, st=sb_ref.at[hh], out=ob_ref))
    for n, u in enumerate(units):
        u["a_ref"] = a_ref.at[n]
        u["q"], u["k"], u["fc"], u["a"] = _hg_gates(u["q"][u["rows"], u["cols"]], u["z"][u["rows"], u["cols"]],
                                                    u["lb"][:, u["cols"]], u["tri"], u["a_ref"])
    for u in units:
        u["scores"] = _hg_scores(u["q"], u["k"], u["fc"], u["a"], u["a_ref"], u["lvl"], u["fwd"])
    for step in range(nch):
        for u in units:
            if u["step"] == step:
                o = _hg_finish(u["q"], u["k"], u["v"][u["rows"], u["cols"]], u["a"], u["scores"], u["st"], u["fwd"])
                u["out"][u["rows"], u["cols"]] = o.astype(u["out"].dtype)


def _hg_scan(p, lbf, lbb, geom):
    nb, nct_rows, seq = geom
    t = p.shape[0]
    r = HG_ROWS
    cb, lb_ = (nct_rows // nb) // r, seq // r
    nsteps = cb + lb_
    lat0 = nct_rows // r

    def fblk(b, s):
        return jnp.where(s < cb, b * cb + s, lat0 + b * lb_ + (s - cb))

    def bblk(b, s):
        return jnp.where(s < cb, b * cb + (cb - 1 - s), lat0 + b * lb_ + (lb_ - 1 - (s - cb)))

    wcol = HG_HB * HG_DK
    ngrp = HG_HEADS // HG_HB
    units = 2 * HG_HB * (r // HG_CHUNK)

    def spec(blk, colblock):
        return pl.BlockSpec((r, wcol), lambda b, h, s: (blk(b, s), colblock * ngrp + h))

    lbspec = pl.BlockSpec((1, wcol), lambda b, h, s: (0, h))
    const = lambda shape: pl.BlockSpec(shape, lambda b, h, s: (0, 0))
    trif, lvlf = _hg_tables(True)
    trib, lvlb = _hg_tables(False)
    return pl.pallas_call(
        _hg_scan_kernel,
        out_shape=(jax.ShapeDtypeStruct((t, D), jnp.bfloat16), jax.ShapeDtypeStruct((t, D), jnp.bfloat16)),
        grid=(nb, ngrp, nsteps),
        in_specs=[spec(fblk, 0), spec(fblk, 1), spec(fblk, 2),
                  spec(bblk, 0), spec(bblk, 1), spec(bblk, 3),
                  lbspec, lbspec, const(trif.shape), const(trib.shape), const(lvlf.shape), const(lvlb.shape)],
        out_specs=(pl.BlockSpec((r, wcol), lambda b, h, s: (fblk(b, s), h)),
                   pl.BlockSpec((r, wcol), lambda b, h, s: (bblk(b, s), h))),
        scratch_shapes=[pltpu.VMEM((HG_HB, HG_DK, HG_DK), jnp.float32),
                        pltpu.VMEM((HG_HB, HG_DK, HG_DK), jnp.float32),
                        pltpu.VMEM((units, HG_CHUNK, HG_DK), jnp.float32)],
        name="hg_scan",
        compiler_params=_cparams(("arbitrary", "arbitrary", "arbitrary")),
    )(p, p, p, p, p, p, lbf, lbb, jnp.asarray(trif, jnp.bfloat16), jnp.asarray(trib, jnp.bfloat16),
      jnp.asarray(lvlf), jnp.asarray(lvlb))


def _hg_out_kernel(of_ref, ob_ref, g_ref, nw_ref, w_ref, x_ref, gate_ref, o_ref, h_ref):
    @pl.when(pl.program_id(1) == 0)
    def _():
        nw = nw_ref[...]
        for h in range(HG_HEADS):
            cs = slice(h * HG_DK, (h + 1) * HG_DK)
            o = of_ref[:, cs].astype(jnp.float32) + ob_ref[:, cs].astype(jnp.float32)
            y = o * lax.rsqrt(jnp.mean(o * o, axis=-1, keepdims=True) + EPS) * nw
            g = g_ref[:, cs]
            h_ref[:, cs] = (y * (g * _sigmoid(g))).astype(jnp.bfloat16)

    acc = jnp.dot(h_ref[...], w_ref[...], preferred_element_type=jnp.float32)
    o_ref[...] = x_ref[...] + gate_ref[...] * acc


def _hg_out(of, ob, p, norm_w, w, x, mods, *, geom, tm, tn):
    nb, nct_rows, seq = geom
    t = x.shape[0]
    nct, lt = nct_rows // tm, seq // tm
    gate = pl.BlockSpec((None, 1, tn), lambda i, j: (_mod_row(i, nct, lt, nb) * 6 + 2, 0, j))
    return pl.pallas_call(
        _hg_out_kernel,
        out_shape=jax.ShapeDtypeStruct((t, D), jnp.float32),
        grid=(t // tm, D // tn),
        in_specs=[pl.BlockSpec((tm, D), lambda i, j: (i, 0)),
                  pl.BlockSpec((tm, D), lambda i, j: (i, 0)),
                  pl.BlockSpec((tm, D), lambda i, j: (i, 4)),
                  pl.BlockSpec((1, HG_DK), lambda i, j: (0, 0)),
                  pl.BlockSpec((D, tn), lambda i, j: (0, j)),
                  pl.BlockSpec((tm, tn), lambda i, j: (i, j)),
                  gate],
        out_specs=pl.BlockSpec((tm, tn), lambda i, j: (i, j)),
        scratch_shapes=[pltpu.VMEM((tm, D), jnp.bfloat16)],
        name="hg_out_proj",
        compiler_params=_cparams(("arbitrary", "arbitrary")),
    )(of, ob, p, norm_w, w, x, mods)


def _resmm_kernel(a_ref, w_ref, x_ref, gate_ref, o_ref):
    acc = jnp.dot(a_ref[...], w_ref[...], preferred_element_type=jnp.float32)
    o_ref[...] = x_ref[...] + gate_ref[...] * acc


def _res_matmul(a, w, x, mods, *, geom, tm, tn):
    nb, nct_rows, seq = geom
    rows = a.shape[0]
    t0 = (x.shape[0] - rows) // tm
    nct, lt = nct_rows // tm, seq // tm
    gate = pl.BlockSpec((None, 1, tn), lambda i, j: (_mod_row(i + t0, nct, lt, nb) * 6 + 2, 0, j))
    return pl.pallas_call(
        _resmm_kernel,
        out_shape=jax.ShapeDtypeStruct((rows, D), jnp.float32),
        grid=(rows // tm, D // tn),
        in_specs=[pl.BlockSpec((tm, a.shape[1]), lambda i, j: (i, 0)),
                  pl.BlockSpec((a.shape[1], tn), lambda i, j: (0, j)),
                  pl.BlockSpec((tm, tn), lambda i, j: (i + t0, j)),
                  gate],
        out_specs=pl.BlockSpec((tm, tn), lambda i, j: (i, j)),
        name="attn_out_proj",
        compiler_params=_cparams(("arbitrary", "arbitrary")),
    )(a, w, x, mods)


def _rope_tables(seq, tm):
    rows = seq // GRID_W
    row = np.repeat(np.arange(rows, dtype=np.float32), GRID_W)
    col = np.tile(np.arange(GRID_W, dtype=np.float32), rows)
    nf = SW_DH // 4
    inv = (ROPE_THETA ** (-np.arange(nf, dtype=np.float32) / nf)).astype(np.float32)
    ang_r = row[:, None] * inv
    ang_c = col[:, None] * inv
    cos = np.concatenate([np.cos(ang_r), np.cos(ang_r), np.cos(ang_c), np.cos(ang_c)], axis=1)
    sin = np.concatenate([-np.sin(ang_r), np.sin(ang_r), -np.sin(ang_c), np.sin(ang_c)], axis=1)
    cos = np.concatenate([np.tile(cos, (1, 2)), np.ones((tm, LANES), np.float32)], axis=0)
    sin = np.concatenate([np.tile(sin, (1, 2)), np.zeros((tm, LANES), np.float32)], axis=0)
    return jnp.asarray(cos, jnp.float32), jnp.asarray(sin, jnp.float32)


def _norm_rope(y, ones_ref, nw, cos, sin, scale):
    ss = jnp.dot((y * y).astype(jnp.bfloat16), ones_ref[...], preferred_element_type=jnp.float32)
    y = y * lax.rsqrt(ss * (1.0 / SW_DH) + EPS) * nw
    first = (lax.broadcasted_iota(jnp.int32, (y.shape[0], LANES), 1) & 16) == 0
    pieces = []
    for g in range(y.shape[1] // LANES):
        yg = y[:, g * LANES:(g + 1) * LANES]
        partner = jnp.where(first, pltpu.roll(yg, LANES - 16, 1), pltpu.roll(yg, 16, 1))
        pieces.append((yg * cos + partner * sin) * scale)
    return pieces


def _q_kernel(x_ref, sh_ref, sc_ref, w_ref, nw_ref, ones_ref, cos_ref, sin_ref, o_ref):
    h = _modulate(x_ref[...], sh_ref[...], sc_ref[...]).astype(jnp.bfloat16)
    acc = jnp.dot(h, w_ref[...], preferred_element_type=jnp.float32)
    pieces = _norm_rope(acc, ones_ref, nw_ref[...], cos_ref[...], sin_ref[...], SW_DH ** -0.5 * LOG2E)
    for g, piece in enumerate(pieces):
        o_ref[:, g * LANES:(g + 1) * LANES] = piece.astype(o_ref.dtype)


def _kv_kernel(x_ref, sh_ref, sc_ref, w_ref, nw_ref, ones_ref, cos_ref, sin_ref, o_ref):
    h = _modulate(x_ref[...], sh_ref[...], sc_ref[...]).astype(jnp.bfloat16)
    acc = jnp.dot(h, w_ref[...], preferred_element_type=jnp.float32)
    nkv = SW_KV * SW_DH
    k_pieces = _norm_rope(acc[:, :nkv], ones_ref, nw_ref[...], cos_ref[...], sin_ref[...], 1.0)
    v_pieces = [acc[:, nkv + g * LANES:nkv + (g + 1) * LANES] for g in range(nkv // LANES)]
    low = lax.broadcasted_iota(jnp.int32, (acc.shape[0], LANES), 1) < SW_DH
    for base, pieces in ((0, k_pieces), (D, v_pieces)):
        for c, piece in enumerate(pieces):
            swapped = pltpu.roll(piece, SW_DH, 1)
            for half, rep in enumerate((jnp.where(low, piece, swapped), jnp.where(low, swapped, piece))):
                rep = rep.astype(o_ref.dtype)
                col = base + (2 * c + half) * SW_GROUP * SW_DH
                o_ref[:, col:col + LANES] = rep
                o_ref[:, col + LANES:col + 2 * LANES] = rep


def _qkv_proj(x, mods, w_qkv, q_norm, k_norm, *, geom, tm):
    nb, nct_rows, seq = geom
    t = x.shape[0]
    nct, lt = nct_rows // tm, seq // tm
    nkv = SW_KV * SW_DH
    cos, sin = _rope_tables(seq, tm)
    ones = lambda n: jnp.asarray(np.kron(np.eye(n // SW_DH, dtype=np.float32),
                                         np.ones((SW_DH, SW_DH), np.float32)), jnp.bfloat16)
    tab = lambda i: (jnp.where(i < nct, lt, (i - nct) % lt), 0)
    once = pl.Buffered(1)

    def call(body, w, nw, n_norm, n_out, name):
        return pl.pallas_call(
            body,
            out_shape=jax.ShapeDtypeStruct((t, n_out), jnp.bfloat16),
            grid=(t // tm,),
            in_specs=[pl.BlockSpec((tm, D), lambda i: (i, 0)),
                      _mod_spec(0, nct, lt, nb),
                      _mod_spec(1, nct, lt, nb),
                      pl.BlockSpec(w.shape, lambda i: (0, 0), pipeline_mode=once),
                      pl.BlockSpec((1, n_norm), lambda i: (0, 0)),
                      pl.BlockSpec((n_norm, n_norm), lambda i: (0, 0), pipeline_mode=once),
                      pl.BlockSpec((tm, LANES), tab),
                      pl.BlockSpec((tm, LANES), tab)],
            out_specs=pl.BlockSpec((tm, n_out), lambda i: (i, 0)),
            name=name,
            compiler_params=_cparams(("arbitrary",)),
        )(x, mods, mods, w, nw, ones(n_norm), cos, sin)

    bf = jnp.bfloat16
    q = call(_q_kernel, w_qkv[:, :D].astype(bf), jnp.tile(q_norm, SW_HEADS)[None, :], D, D, "q_proj")
    kv = call(_kv_kernel, w_qkv[:, D:].astype(bf), jnp.tile(k_norm, SW_KV)[None, :], nkv, 2 * D, "kv_proj")
    return q, kv


def _attn_items(items, sink_ref, kv, o_ref):
    nt = (((1,), (1,)), ((), ()))
    work = [(it, g) for it in items for g in range(SW_GROUP)]
    scores = []
    for (_, q, kcat, _, _), g in work:
        grp = lax.broadcasted_iota(jnp.int32, q.shape, 1) // SW_DH
        scores.append(lax.dot_general(jnp.where(grp == g, q, jnp.zeros_like(q)), kcat, nt,
                                      preferred_element_type=jnp.float32))
    probs, denoms = [], []
    for ((_, _, _, _, valid), g), s in zip(work, scores):
        sink = sink_ref[kv * SW_GROUP + g] * LOG2E
        if valid is not None:
            s = jnp.where(valid, s, NEG_BIG)
        m = jnp.maximum(jnp.max(s, axis=-1, keepdims=True), sink)
        p = jnp.exp2(s - m)
        denoms.append(jnp.sum(p, axis=-1, keepdims=True) + jnp.exp2(sink - m))
        probs.append(p.astype(jnp.bfloat16))
    outs = {}
    for n, ((rows, q, _, vcat, _), g) in enumerate(work):
        grp = lax.broadcasted_iota(jnp.int32, q.shape, 1) // SW_DH
        og = jnp.dot(probs[n], vcat, preferred_element_type=jnp.float32) / denoms[n]
        outs[rows] = og if g == 0 else jnp.where(grp == g, og, outs[rows])
    for rows, out in outs.items():
        o_ref[rows[0]:rows[1], :] = out.astype(o_ref.dtype)


def _attn_kernel(sink_ref, q_ref, kp_ref, kc_ref, kn_ref, kx_ref, vp_ref, vc_ref, vn_ref, vx_ref, o_ref,
                 *, nstep, ctx_out):
    kv, j = pl.program_id(1), pl.program_id(2)
    blk = SW_BLOCK

    def latent():
        kx, vx = kx_ref[...], vx_ref[...]
        k0, k1, v0, v1 = kc_ref[0:blk, :], kc_ref[blk:2 * blk, :], vc_ref[0:blk, :], vc_ref[blk:2 * blk, :]
        ns = 3 * blk + kx.shape[0]
        t = lax.broadcasted_iota(jnp.int32, (blk, ns), 0)
        s = lax.broadcasted_iota(jnp.int32, (blk, ns), 1)

        def window(has_prev, has_next):
            lo = jnp.maximum(t, jnp.where(has_prev, 0, blk))
            hi = jnp.minimum(t + 2 * blk, jnp.where(has_next, 3 * blk - 1, 2 * blk - 1))
            return ((s >= lo) & (s <= hi)) | (s >= 3 * blk)

        items = [((0, blk), q_ref[0:blk, :], jnp.concatenate([kp_ref[...], k0, k1, kx], axis=0),
                  jnp.concatenate([vp_ref[...], v0, v1, vx], axis=0), window(j > 0, True)),
                 ((blk, 2 * blk), q_ref[blk:2 * blk, :], jnp.concatenate([k0, k1, kn_ref[...], kx], axis=0),
                  jnp.concatenate([v0, v1, vn_ref[...], vx], axis=0), window(True, j < nstep - 1))]
        _attn_items(items, sink_ref, kv, o_ref)

    if not ctx_out:
        latent()
        return
    pl.when(j < nstep)(latent)

    @pl.when(j >= nstep)
    def _():
        _attn_items([((0, 2 * blk), q_ref[...], kx_ref[...], vx_ref[...], None)], sink_ref, kv, o_ref)


def _attention(q, kv, sink, geom, ctx_out):
    nb, nct_rows, seq = geom
    ctx_len = nct_rows // nb
    pair = 2 * SW_BLOCK
    assert ctx_len == pair and seq % pair == 0
    nblk = seq // SW_BLOCK
    nstep = seq // pair
    lat0 = nct_rows // SW_BLOCK
    w = SW_GROUP * SW_DH
    smem = pl.BlockSpec(memory_space=pltpu.SMEM)

    def qrow(b, j):
        return jnp.where(j < nstep, nct_rows // pair + b * nstep + j, b)

    def edge(off, colblock):
        return pl.BlockSpec((SW_BLOCK, w), lambda b, kv, j: (
            lat0 + b * nblk + jnp.clip(2 * jnp.minimum(j, nstep - 1) + off, 0, nblk - 1), colblock * SW_KV + kv))

    def center(colblock):
        return pl.BlockSpec((pair, w), lambda b, kv, j: (qrow(b, jnp.minimum(j, nstep - 1)), colblock * SW_KV + kv))

    def ctxkv(colblock):
        return pl.BlockSpec((ctx_len, w), lambda b, kv, j: (b, colblock * SW_KV + kv))

    if ctx_out:
        steps, out_rows = nstep + 1, q.shape[0]
        out_spec = pl.BlockSpec((pair, w), lambda b, kv, j: (qrow(b, j), kv))
    else:
        steps, out_rows = nstep, nb * seq
        out_spec = pl.BlockSpec((pair, w), lambda b, kv, j: (b * nstep + j, kv))
    return pl.pallas_call(
        functools.partial(_attn_kernel, nstep=nstep, ctx_out=ctx_out),
        out_shape=jax.ShapeDtypeStruct((out_rows, D), jnp.bfloat16),
        grid=(nb, SW_KV, steps),
        in_specs=[smem, pl.BlockSpec((pair, w), lambda b, kv, j: (qrow(b, j), kv)),
                  edge(-1, 0), center(0), edge(2, 0), ctxkv(0),
                  edge(-1, 1), center(1), edge(2, 1), ctxkv(1)],
        out_specs=out_spec,
        name="attention",
        compiler_params=_cparams(("arbitrary", "arbitrary", "arbitrary")),
    )(sink, q, kv, kv, kv, kv, kv, kv, kv, kv)


def _router_kernel(x_ref, sh_ref, sc_ref, r_ref, h_ref, info_ref):
    h = _modulate(x_ref[...], sh_ref[...], sc_ref[...])
    h_ref[...] = h
    logits = jnp.dot(h, r_ref[...], precision=lax.Precision.HIGHEST, preferred_element_type=jnp.float32)
    lane = lax.broadcasted_iota(jnp.int32, logits.shape, 1).astype(jnp.float32)
    logits = jnp.where(lane < N_EXPERTS, logits, -jnp.inf)
    l1 = jnp.max(logits, axis=-1, keepdims=True)
    i1 = jnp.min(jnp.where(logits == l1, lane, float(LANES)), axis=-1, keepdims=True)
    rest = jnp.where(lane == i1, -jnp.inf, logits)
    l2 = jnp.max(rest, axis=-1, keepdims=True)
    i2 = jnp.min(jnp.where(rest == l2, lane, float(LANES)), axis=-1, keepdims=True)
    w1 = 1.0 / (1.0 + jnp.exp(l2 - l1))
    info_ref[...] = jnp.where(lane == 0.0, i1, jnp.where(lane == 1.0, i2, jnp.where(lane == 2.0, w1, 1.0 - w1)))


def _router(x, mods, router_pad, *, geom, tm):
    nb, nct_rows, seq = geom
    t = x.shape[0]
    nct, lt = nct_rows // tm, seq // tm
    return pl.pallas_call(
        _router_kernel,
        out_shape=(jax.ShapeDtypeStruct((t, D), jnp.float32), jax.ShapeDtypeStruct((t, LANES), jnp.float32)),
        grid=(t // tm,),
        in_specs=[pl.BlockSpec((tm, D), lambda i: (i, 0)), _mod_spec(3, nct, lt, nb), _mod_spec(4, nct, lt, nb),
                  pl.BlockSpec((D, LANES), lambda i: (0, 0))],
        out_specs=(pl.BlockSpec((tm, D), lambda i: (i, 0)), pl.BlockSpec((tm, LANES), lambda i: (i, 0))),
        name="moe_router",
        compiler_params=_cparams(("arbitrary",)),
    )(x, mods, mods, router_pad)


FFN_SUB = 512


def _swiglu(h, wgu_ref, wd_ref):
    fdim = wd_ref.shape[0]
    bounds = list(range(0, fdim, FFN_SUB)) + [fdim]
    blocks = list(zip(bounds[:-1], bounds[1:]))
    up = lambda lo, hi: (jnp.dot(h, wgu_ref[:, lo:hi], preferred_element_type=jnp.float32),
                         jnp.dot(h, wgu_ref[:, fdim + lo:fdim + hi], preferred_element_type=jnp.float32))
    down = lambda gu, lo, hi: jnp.dot((gu[0] * _sigmoid(gu[0]) * gu[1]).astype(jnp.bfloat16), wd_ref[lo:hi, :],
                                      preferred_element_type=jnp.float32)
    gu = up(*blocks[0])
    acc = None
    for prev, cur in zip(blocks[:-1], blocks[1:]):
        gu_next = up(*cur)
        y = down(gu, *prev)
        acc = y if acc is None else acc + y
        gu = gu_next
    y = down(gu, *blocks[-1])
    return y if acc is None else acc + y


def _ffn_kernel(x_ref, sh_ref, sc_ref, gate_ref, wgu_ref, wd_ref, o_ref):
    x = x_ref[...]
    h = _modulate(x, sh_ref[...], sc_ref[...]).astype(jnp.bfloat16)
    o_ref[...] = x + gate_ref[...] * _swiglu(h, wgu_ref, wd_ref)


def _ffn(x, mods, wgu, wd, *, geom, tm):
    nb, nct_rows, seq = geom
    t = x.shape[0]
    nct, lt = nct_rows // tm, seq // tm
    once = pl.Buffered(1)
    return pl.pallas_call(
        _ffn_kernel,
        out_shape=jax.ShapeDtypeStruct((t, D), jnp.float32),
        grid=(t // tm,),
        in_specs=[pl.BlockSpec((tm, D), lambda i: (i, 0)),
                  _mod_spec(3, nct, lt, nb),
                  _mod_spec(4, nct, lt, nb),
                  _mod_spec(5, nct, lt, nb),
                  pl.BlockSpec(wgu.shape, lambda i: (0, 0), pipeline_mode=once),
                  pl.BlockSpec(wd.shape, lambda i: (0, 0), pipeline_mode=once)],
        out_specs=pl.BlockSpec((tm, D), lambda i: (i, 0)),
        name="dense_ffn",
        compiler_params=_cparams(("arbitrary",)),
    )(x, mods, mods, mods, wgu, wd)


def _moe_kernel(te_ref, nt_ref, rows0_ref, rows1_ref, rows2_ref, outrows_ref, h_hbm, wgu_ref, wd_ref, y_hbm,
                hbuf, hb_ref, obuf, gsem, ssem, *, tm):
    del te_ref
    i = pl.program_id(0)
    nt = nt_ref[0]
    g_slot, o_slot = i % 3, i % 2

    def gather(rows_ref, s):
        for r in range(tm):
            pltpu.make_async_copy(h_hbm.at[pl.ds(rows_ref[0, r], 1), :], hbuf.at[s, pl.ds(r, 1), :],
                                  gsem.at[s]).start()

    def scatter_prev():
        for r in range(tm):
            pltpu.make_async_copy(obuf.at[1 - o_slot, pl.ds(r, 1), :], y_hbm.at[pl.ds(outrows_ref[0, r], 1), :],
                                  ssem.at[1 - o_slot]).start()

    @pl.when(i == 0)
    def _():
        gather(rows0_ref, 0)
        gather(rows1_ref, 1)
        obuf[1] = jnp.zeros((tm, D), jnp.float32)

    @pl.when((i >= 1) & (i - 2 < nt))
    def _():
        pltpu.make_async_copy(obuf.at[o_slot], obuf.at[o_slot], ssem.at[o_slot]).wait()

    @pl.when(i < nt + 2)
    def _():
        pltpu.make_async_copy(hbuf.at[g_slot], hbuf.at[g_slot], gsem.at[g_slot]).wait()
        hb_ref[...] = hbuf[g_slot].astype(jnp.bfloat16)

    @pl.when(i < nt)
    def _():
        gather(rows2_ref, (i + 2) % 3)
        scatter_prev()
        obuf[o_slot] = _swiglu(hb_ref[...], wgu_ref, wd_ref)

    @pl.when(i == nt)
    def _():
        scatter_prev()


def _moe_experts(h, tile_expert, n_tiles, rows, outrows, wgu, wd, n_out, *, tm):
    nt = rows.shape[0]
    fdim = wd.shape[1]
    assert nt >= 3
    smem_rows = lambda off: pl.BlockSpec((None, 1, tm), lambda i, te, n: (jnp.minimum(i + off, nt - 1), 0, 0),
                                         memory_space=pltpu.SMEM)
    once = pl.Buffered(1)
    grid_spec = pltpu.PrefetchScalarGridSpec(
        num_scalar_prefetch=2,
        grid=(nt,),
        in_specs=[smem_rows(0), smem_rows(1), smem_rows(2),
                  pl.BlockSpec((None, 1, tm), lambda i, te, n: (i, 0, 0), memory_space=pltpu.SMEM),
                  pl.BlockSpec(memory_space=pl.ANY),
                  pl.BlockSpec((None, D, 2 * fdim), lambda i, te, n: (te[i], 0, 0), pipeline_mode=once),
                  pl.BlockSpec((None, fdim, D), lambda i, te, n: (te[i], 0, 0), pipeline_mode=once)],
        out_specs=pl.BlockSpec(memory_space=pl.ANY),
        scratch_shapes=[pltpu.VMEM((3, tm, D), jnp.float32), pltpu.VMEM((tm, D), jnp.bfloat16),
                        pltpu.VMEM((2, tm, D), jnp.float32), pltpu.SemaphoreType.DMA((3,)),
                        pltpu.SemaphoreType.DMA((2,))])
    return pl.pallas_call(
        functools.partial(_moe_kernel, tm=tm),
        out_shape=jax.ShapeDtypeStruct((n_out, D), jnp.float32),
        grid_spec=grid_spec,
        name="moe_experts",
        compiler_params=_cparams(("arbitrary",)),
    )(tile_expert, n_tiles, rows, rows, rows, outrows, h, wgu, wd)


def _combine_kernel(y0_ref, y1_ref, x_ref, gate_ref, info_ref, o_ref):
    info = info_ref[...]
    y = info[:, 2:3] * y0_ref[...] + info[:, 3:4] * y1_ref[...]
    o_ref[...] = x_ref[...] + gate_ref[...] * y


def _moe_combine(x, y2, info, mods, *, geom, tmc):
    nb, nct_rows, seq = geom
    t = x.shape[0]
    nct, lt = nct_rows // tmc, seq // tmc
    return pl.pallas_call(
        _combine_kernel,
        out_shape=jax.ShapeDtypeStruct((t, D), jnp.float32),
        grid=(t // tmc,),
        in_specs=[pl.BlockSpec((None, tmc, D), lambda i: (0, i, 0)),
                  pl.BlockSpec((None, tmc, D), lambda i: (1, i, 0)),
                  pl.BlockSpec((tmc, D), lambda i: (i, 0)),
                  _mod_spec(5, nct, lt, nb),
                  pl.BlockSpec((tmc, LANES), lambda i: (i, 0))],
        out_specs=pl.BlockSpec((tmc, D), lambda i: (i, 0)),
        name="moe_combine",
        compiler_params=_cparams(("arbitrary",)),
    )(y2, y2, x, mods, info)


def _moe(x, mods, router_pad, wgu, wd, *, geom, tm, tmc):
    tr = x.shape[0]
    h, info = _router(x, mods, router_pad, geom=geom, tm=tm)
    e = info[:, 0:2].astype(jnp.int32).reshape(-1)
    onehot = (e[:, None] == jnp.arange(N_EXPERTS, dtype=jnp.int32)[None, :]).astype(jnp.int32)
    csum = jnp.cumsum(onehot, axis=0)
    rank = jnp.sum(onehot * (csum - 1), axis=1)
    counts = csum[-1]
    padded = ((counts + tm - 1) // tm) * tm
    ends = jnp.cumsum(padded)
    dest = (ends - padded)[e] + rank
    nt = 2 * tr // tm + N_EXPERTS + 2
    tile_expert = jnp.minimum(jnp.sum(jnp.arange(nt, dtype=jnp.int32)[:, None] * tm >= ends[None, :], axis=1),
                              N_EXPERTS - 1).astype(jnp.int32)
    n_tiles = (ends[-1:] // tm).astype(jnp.int32)
    slot_of_row = jnp.full((nt * tm,), -1, jnp.int32).at[dest].set(jnp.arange(2 * tr, dtype=jnp.int32),
                                                                   unique_indices=True)
    tp = tr + tm // 2
    j = jnp.arange(nt * tm, dtype=jnp.int32) % tm
    spare = jnp.where(j < tm // 2, tr + j, tp + tr + j - tm // 2)
    out_row = (slot_of_row % 2) * tp + slot_of_row // 2
    rows = (jnp.maximum(slot_of_row, 0) // 2).reshape(nt, 1, tm)
    outrows = jnp.concatenate([spare[:tm], jnp.where(slot_of_row >= 0, out_row, spare)]).reshape(nt + 1, 1, tm)
    y = _moe_experts(h, tile_expert, n_tiles, rows, outrows, wgu, wd, 2 * tp, tm=tm)
    return _moe_combine(x, y.reshape(2, tp, D), info, mods, geom=geom, tmc=tmc)


def kernel(x, c, ctx, c_ctx, w_mod, b_mod, hg_w_in, hg_lb_logits, hg_norm_w, hg_w_out, sw_w_qkv, sw_q_norm,
           sw_k_norm, sw_sink, sw_w_out, ff_w_gate_up, ff_w_down, moe_router, moe_w_gate_up, moe_w_down):
    nb, seq, _ = x.shape
    ctx_len = ctx.shape[1]
    depth = w_mod.shape[0]
    nct_rows = nb * ctx_len
    geom = (nb, nct_rows, seq)
    tm = 512
    tmb = 1024 if nct_rows % 1024 == 0 and seq % 1024 == 0 else tm
    bf = jnp.bfloat16

    xs = jnp.concatenate([ctx.reshape(nct_rows, D), x.reshape(nb * seq, D)], axis=0)
    cpad = jnp.concatenate([c, c_ctx[None, :], jnp.zeros((8 - nb - 1, D), jnp.float32)], axis=0)
    mods_all = _modvecs(cpad, w_mod, b_mod)

    p_lb = jax.nn.softmax(hg_lb_logits.astype(jnp.float32), axis=0)
    lower_bounds = jnp.cumsum(p_lb, axis=0) - p_lb[:1]

    for i in range(depth):
        ctx_live = i < depth - 1
        mods = mods_all[i]
        j = i // 2
        if i % 2 == 0:
            p = _mod_matmul(xs, mods, hg_w_in[j].astype(bf), k_shift=0, geom=geom, tm=tm, tn=1024,
                            out_dtype=jnp.float32)
            of, ob = _hg_scan(p, lower_bounds[j, 0:1], lower_bounds[j, 1:2], geom)
            xs = _hg_out(of, ob, p, hg_norm_w[j][None, :], hg_w_out[j].astype(bf), xs, mods,
                         geom=geom, tm=tmb, tn=1024)
        else:
            q, kv = _qkv_proj(xs, mods, sw_w_qkv[j], sw_q_norm[j], sw_k_norm[j], geom=geom, tm=tmb)
            o = _attention(q, kv, sw_sink[j], geom, ctx_live)
            xs = _res_matmul(o, sw_w_out[j].astype(bf), xs, mods, geom=geom, tm=tmb, tn=1024)
            if not ctx_live:
                geom = (nb, 0, seq)
        if i % 2 == 0:
            xs = _ffn(xs, mods, ff_w_gate_up[j].astype(bf), ff_w_down[j].astype(bf), geom=geom, tm=tm)
        else:
            rpad = jnp.pad(moe_router[j], ((0, 0), (0, LANES - N_EXPERTS)))
            xs = _moe(xs, mods, rpad, moe_w_gate_up[j].astype(bf), moe_w_down[j].astype(bf),
                      geom=geom, tm=tm, tmc=tm)
    return xs[xs.shape[0] - nb * seq:].reshape(nb, seq, D)
```

```python
import functools

import numpy as np
import jax
import jax.numpy as jnp
from jax import lax
from jax.experimental import pallas as pl
from jax.experimental.pallas import tpu as pltpu

D = 1024
EPS = 1e-6
NEG_BIG = -1e30
LOG2E = 1.4426950408889634
GRID_W = 64
ROPE_THETA = 10000.0

HG_HEADS = 8
HG_DK = 128
HG_FDIM = HG_HEADS * HG_DK
HG_CHUNK = 128
HG_LEVELS = 7
HG_ROWS = 256
HG_HB = 4

SW_HEADS = 16
SW_KV = 4
SW_GROUP = 4
SW_DH = 64
SW_BLOCK = 128

N_EXPERTS = 8
LANES = 128

VMEM_LIMIT = 56 * 1024 * 1024


def _cparams(sem):
    return pltpu.CompilerParams(dimension_semantics=sem, vmem_limit_bytes=VMEM_LIMIT)


def _sigmoid(x):
    return 1.0 / (1.0 + jnp.exp2(x * -LOG2E))


def _mod_row(i, nct, lt, nb):
    return jnp.where(i < nct, nb, (i - nct) // lt)


def _mod_spec(k, nct, lt, nb):
    return pl.BlockSpec((None, 1, D), lambda i, *_: (_mod_row(i, nct, lt, nb) * 6 + k, 0, 0))


def _modulate(x, shift, scale):
    ms = jnp.mean(x * x, axis=-1, keepdims=True)
    return (x * lax.rsqrt(ms + EPS)) * (1.0 + scale) + shift


def _modvec_kernel(c_ref, w_ref, b_ref, o_ref):
    c = c_ref[...]
    s = c * _sigmoid(c)
    o_ref[...] = jnp.dot(s, w_ref[...], precision=lax.Precision.HIGHEST,
                         preferred_element_type=jnp.float32) + b_ref[...]


def _modvecs(cpad, w_mod, b_mod):
    depth = w_mod.shape[0]
    tn = 1024
    out = pl.pallas_call(
        _modvec_kernel,
        out_shape=jax.ShapeDtypeStruct((depth, 8, 6 * D), jnp.float32),
        grid=(depth, 6 * D // tn),
        in_specs=[pl.BlockSpec((8, D), lambda l, j: (0, 0)),
                  pl.BlockSpec((None, D, tn), lambda l, j: (l, 0, j)),
                  pl.BlockSpec((None, 1, tn), lambda l, j: (l, 0, j))],
        out_specs=pl.BlockSpec((None, 8, tn), lambda l, j: (l, 0, j)),
        name="adaln_vectors",
        compiler_params=_cparams(("arbitrary", "arbitrary")),
    )(cpad, w_mod, b_mod.reshape(depth, 1, 6 * D))
    return out.reshape(depth, 8 * 6, 1, D)


def _modmm_kernel(x_ref, sh_ref, sc_ref, w_ref, o_ref, *, tn):
    h = _modulate(x_ref[...], sh_ref[...], sc_ref[...]).astype(jnp.bfloat16)
    for n0 in range(0, w_ref.shape[1], tn):
        o_ref[:, n0:n0 + tn] = jnp.dot(h, w_ref[:, n0:n0 + tn], preferred_element_type=jnp.float32).astype(o_ref.dtype)


def _mod_matmul(x, mods, w, layer, *, k_shift, geom, tm, tn, out_dtype):
    nb, nct_rows, seq = geom
    t, n = x.shape[0], w.shape[2]
    nct, lt = nct_rows // tm, seq // tm
    return pl.pallas_call(
        functools.partial(_modmm_kernel, tn=tn),
        out_shape=jax.ShapeDtypeStruct((t, n), out_dtype),
        grid=(t // tm,),
        in_specs=[pl.BlockSpec((tm, D), lambda i: (i, 0)),
                  _mod_spec(k_shift, nct, lt, nb),
                  _mod_spec(k_shift + 1, nct, lt, nb),
                  pl.BlockSpec((None,) + w.shape[1:], lambda i: (layer, 0, 0), pipeline_mode=pl.Buffered(1))],
        out_specs=pl.BlockSpec((tm, n), lambda i: (i, 0)),
        name="hg_in_proj",
        compiler_params=_cparams(("arbitrary",)),
    )(x, mods, mods, w)


def _hg_tables(fwd):
    c = HG_CHUNK
    t = np.arange(c)[:, None]
    u = np.arange(c)[None, :]
    reach = (u <= t) if fwd else (u >= t)
    x = t ^ u
    lvl = np.where(x > 0, np.floor(np.log2(np.maximum(x, 1))), HG_LEVELS)
    lvl = np.where(reach, lvl, -1).astype(np.int32)
    tri = np.concatenate([reach.astype(np.float32)] * 3, axis=1)
    return tri, lvl


def _hg_gates(q_raw, z, lb, tri_ref, a_ref):
    bf = jnp.bfloat16
    q = q_raw * _sigmoid(q_raw)
    sig = _sigmoid(z)
    fc = jnp.maximum(lb + (1.0 - lb) * sig, 1e-30)
    lf = jnp.log(fc) * LOG2E
    k = (1.0 - lb) * (1.0 - sig)
    hi = lf.astype(bf)
    rest = lf - hi.astype(jnp.float32)
    mid = rest.astype(bf)
    lo = (rest - mid.astype(jnp.float32)).astype(bf)
    a = jnp.dot(tri_ref[...], jnp.concatenate([hi, mid, lo], axis=0), preferred_element_type=jnp.float32)
    a_ref[...] = a
    return q, k, fc, a


def _hg_scores(q, k, fc, a, a_ref, lvl, fwd):
    c = HG_CHUNK
    bf = jnp.bfloat16

    def rows_of(row, n):
        return jnp.broadcast_to(a_ref[row:row + 1, :], (n, HG_DK))

    rowi = lax.broadcasted_iota(jnp.int32, (c, HG_DK), 0)
    nt = (((1,), (1,)), ((), ()))
    zeros8 = jnp.zeros((8, HG_DK), jnp.float32)

    def level_operands(l):
        w = 1 << l
        near = w - 1 if fwd else w
        if w < 8:
            upper = ((rowi >> l) & 1) == 1
            q_side = upper if fwd else jnp.logical_not(upper)
            if l == 0:
                qe, ke = q * fc, k
            else:
                if l == 1:
                    first = jnp.concatenate([rows_of(8 * g + near, 8) for g in range(c // 8)], axis=0)
                    second = jnp.concatenate([rows_of(8 * g + 4 + near, 8) for g in range(c // 8)], axis=0)
                    bnd = jnp.where((rowi & 4) == 0, first, second)
                else:
                    bnd = jnp.concatenate([rows_of(8 * g + near, 8) for g in range(c // 8)], axis=0)
                e = jnp.exp2(-jnp.abs(a - bnd))
                qe, ke = q * e, k * e
            return jnp.where(q_side, qe, 0.0).astype(bf), jnp.where(q_side, 0.0, ke).astype(bf)
        qp, kp = [], []
        for j in range(c // w):
            rs = slice(j * w, (j + 1) * w)
            bnd = rows_of((j // 2) * 2 * w + near, w)
            zero = jnp.concatenate([zeros8] * (w // 8), axis=0)
            if (j % 2 == 1) == fwd:
                qp.append(q[rs] * jnp.exp2(a[rs] - bnd))
                kp.append(zero)
            else:
                qp.append(zero)
                kp.append(k[rs] * jnp.exp2(bnd - a[rs]))
        return jnp.concatenate(qp, axis=0).astype(bf), jnp.concatenate(kp, axis=0).astype(bf)

    scores = jnp.where(lvl == HG_LEVELS,
                       lax.dot_general(q.astype(bf), k.astype(bf), nt, preferred_element_type=jnp.float32), 0.0)
    for l in range(HG_LEVELS):
        qm, km = level_operands(l)
        sc = lax.dot_general(qm, km, nt, preferred_element_type=jnp.float32)
        scores = jnp.where(lvl == l, sc, scores)
    return scores.astype(bf)


def _hg_finish(q, k, v, a, scores, st_ref, fwd):
    c = HG_CHUNK
    bf = jnp.bfloat16
    nt = (((1,), (1,)), ((), ()))
    last = c - 1 if fwd else 0
    e_in = jnp.exp2(a)
    e_out = jnp.exp2(a[last:last + 1, :] - a)
    st = st_ref[...]
    o = (jnp.dot(scores, v.astype(bf), preferred_element_type=jnp.float32)
         + lax.dot_general((q * e_in).astype(bf), st.astype(bf), nt, preferred_element_type=jnp.float32))
    st_ref[...] = (e_in[last:last + 1, :] * st
                   + jnp.dot(v.T.astype(bf), (k * e_out).astype(bf), preferred_element_type=jnp.float32))
    return o


def _hg_scan_kernel(qf_ref, vf_ref, zf_ref, qb_ref, vb_ref, zb_ref, lbf_ref, lbb_ref, trif_ref, trib_ref,
                    lvlf_ref, lvlb_ref, of_ref, ob_ref, sf_ref, sb_ref, a_ref):
    @pl.when(pl.program_id(2) == 0)
    def _():
        sf_ref[...] = jnp.zeros_like(sf_ref)
        sb_ref[...] = jnp.zeros_like(sb_ref)

    c = HG_CHUNK
    nch = HG_ROWS // c
    lvlf, lvlb = lvlf_ref[...], lvlb_ref[...]
    units = []
    for hh in range(HG_HB):
        cs = slice(hh * HG_DK, (hh + 1) * HG_DK)
        for step, ci in enumerate(range(nch)):
            units.append(dict(step=step, rows=slice(ci * c, (ci + 1) * c), cols=cs, fwd=True, q=qf_ref, v=vf_ref,
                              z=zf_ref, lb=lbf_ref, tri=trif_ref, lvl=lvlf, st=sf_ref.at[hh], out=of_ref))
        for step, ci in enumerate(reversed(range(nch))):
            units.append(dict(step=step, rows=slice(ci * c, (ci + 1) * c), cols=cs, fwd=False, q=qb_ref, v=vb_ref,
                              z=zb_ref, lb=lbb_ref, tri=trib_ref, lvl=lvlb, st=sb_ref.at[hh], out=ob_ref))
    for n, u in enumerate(units):
        u["a_ref"] = a_ref.at[n]
        u["q"], u["k"], u["fc"], u["a"] = _hg_gates(u["q"][u["rows"], u["cols"]], u["z"][u["rows"], u["cols"]],
                                                    u["lb"][:, u["cols"]], u["tri"], u["a_ref"])
    for u in units:
        u["scores"] = _hg_scores(u["q"], u["k"], u["fc"], u["a"], u["a_ref"], u["lvl"], u["fwd"])
    for step in range(nch):
        for u in units:
            if u["step"] == step:
                o = _hg_finish(u["q"], u["k"], u["v"][u["rows"], u["cols"]], u["a"], u["scores"], u["st"], u["fwd"])
                u["out"][u["rows"], u["cols"]] = o.astype(u["out"].dtype)


def _hg_scan(p, lbf, lbb, geom):
    nb, nct_rows, seq = geom
    t = p.shape[0]
    r = HG_ROWS
    cb, lb_ = (nct_rows // nb) // r, seq // r
    nsteps = cb + lb_
    lat0 = nct_rows // r

    def fblk(b, s):
        return jnp.where(s < cb, b * cb + s, lat0 + b * lb_ + (s - cb))

    def bblk(b, s):
        return jnp.where(s < cb, b * cb + (cb - 1 - s), lat0 + b * lb_ + (lb_ - 1 - (s - cb)))

    wcol = HG_HB * HG_DK
    ngrp = HG_HEADS // HG_HB
    units = 2 * HG_HB * (r // HG_CHUNK)

    def spec(blk, colblock):
        return pl.BlockSpec((r, wcol), lambda b, h, s: (blk(b, s), colblock * ngrp + h))

    lbspec = pl.BlockSpec((1, wcol), lambda b, h, s: (0, h))
    const = lambda shape: pl.BlockSpec(shape, lambda b, h, s: (0, 0))
    trif, lvlf = _hg_tables(True)
    trib, lvlb = _hg_tables(False)
    return pl.pallas_call(
        _hg_scan_kernel,
        out_shape=(jax.ShapeDtypeStruct((t, D), jnp.bfloat16), jax.ShapeDtypeStruct((t, D), jnp.bfloat16)),
        grid=(nb, ngrp, nsteps),
        in_specs=[spec(fblk, 0), spec(fblk, 1), spec(fblk, 2),
                  spec(bblk, 0), spec(bblk, 1), spec(bblk, 3),
                  lbspec, lbspec, const(trif.shape), const(trib.shape), const(lvlf.shape), const(lvlb.shape)],
        out_specs=(pl.BlockSpec((r, wcol), lambda b, h, s: (fblk(b, s), h)),
                   pl.BlockSpec((r, wcol), lambda b, h, s: (bblk(b, s), h))),
        scratch_shapes=[pltpu.VMEM((HG_HB, HG_DK, HG_DK), jnp.float32),
                        pltpu.VMEM((HG_HB, HG_DK, HG_DK), jnp.float32),
                        pltpu.VMEM((units, HG_CHUNK, HG_DK), jnp.float32)],
        name="hg_scan",
        compiler_params=_cparams(("arbitrary", "arbitrary", "arbitrary")),
    )(p, p, p, p, p, p, lbf, lbb, jnp.asarray(trif, jnp.bfloat16), jnp.asarray(trib, jnp.bfloat16),
      jnp.asarray(lvlf), jnp.asarray(lvlb))


def _hg_out_kernel(of_ref, ob_ref, g_ref, nw_ref, w_ref, x_ref, gate_ref, o_ref, h_ref):
    @pl.when(pl.program_id(1) == 0)
    def _():
        nw = nw_ref[...]
        for h in range(HG_HEADS):
            cs = slice(h * HG_DK, (h + 1) * HG_DK)
            o = of_ref[:, cs].astype(jnp.float32) + ob_ref[:, cs].astype(jnp.float32)
            y = o * lax.rsqrt(jnp.mean(o * o, axis=-1, keepdims=True) + EPS) * nw
            g = g_ref[:, cs]
            h_ref[:, cs] = (y * (g * _sigmoid(g))).astype(jnp.bfloat16)

    acc = jnp.dot(h_ref[...], w_ref[...], preferred_element_type=jnp.float32)
    o_ref[...] = x_ref[...] + gate_ref[...] * acc


def _hg_out(of, ob, p, norm_w, w, x, mods, *, geom, tm, tn):
    nb, nct_rows, seq = geom
    t = x.shape[0]
    nct, lt = nct_rows // tm, seq // tm
    gate = pl.BlockSpec((None, 1, tn), lambda i, j: (_mod_row(i, nct, lt, nb) * 6 + 2, 0, j))
    return pl.pallas_call(
        _hg_out_kernel,
        out_shape=jax.ShapeDtypeStruct((t, D), jnp.float32),
        grid=(t // tm, D // tn),
        in_specs=[pl.BlockSpec((tm, D), lambda i, j: (i, 0)),
                  pl.BlockSpec((tm, D), lambda i, j: (i, 0)),
                  pl.BlockSpec((tm, D), lambda i, j: (i, 4)),
                  pl.BlockSpec((1, HG_DK), lambda i, j: (0, 0)),
                  pl.BlockSpec((D, tn), lambda i, j: (0, j)),
                  pl.BlockSpec((tm, tn), lambda i, j: (i, j)),
                  gate],
        out_specs=pl.BlockSpec((tm, tn), lambda i, j: (i, j)),
        scratch_shapes=[pltpu.VMEM((tm, D), jnp.bfloat16)],
        name="hg_out_proj",
        compiler_params=_cparams(("arbitrary", "arbitrary")),
    )(of, ob, p, norm_w, w, x, mods)


def _resmm_kernel(a_ref, w_ref, x_ref, gate_ref, o_ref):
    acc = jnp.dot(a_ref[...], w_ref[...], preferred_element_type=jnp.float32)
    o_ref[...] = x_ref[...] + gate_ref[...] * acc


def _res_matmul(a, w, x, mods, *, geom, tm, tn):
    nb, nct_rows, seq = geom
    rows = a.shape[0]
    t0 = (x.shape[0] - rows) // tm
    nct, lt = nct_rows // tm, seq // tm
    gate = pl.BlockSpec((None, 1, tn), lambda i, j: (_mod_row(i + t0, nct, lt, nb) * 6 + 2, 0, j))
    return pl.pallas_call(
        _resmm_kernel,
        out_shape=jax.ShapeDtypeStruct((rows, D), jnp.float32),
        grid=(rows // tm, D // tn),
        in_specs=[pl.BlockSpec((tm, a.shape[1]), lambda i, j: (i, 0)),
                  pl.BlockSpec((a.shape[1], tn), lambda i, j: (0, j)),
                  pl.BlockSpec((tm, tn), lambda i, j: (i + t0, j)),
                  gate],
        out_specs=pl.BlockSpec((tm, tn), lambda i, j: (i, j)),
        name="attn_out_proj",
        compiler_params=_cparams(("arbitrary", "arbitrary")),
    )(a, w, x, mods)


def _rope_tables(seq, tm):
    rows = seq // GRID_W
    row = np.repeat(np.arange(rows, dtype=np.float32), GRID_W)
    col = np.tile(np.arange(GRID_W, dtype=np.float32), rows)
    nf = SW_DH // 4
    inv = (ROPE_THETA ** (-np.arange(nf, dtype=np.float32) / nf)).astype(np.float32)
    ang_r = row[:, None] * inv
    ang_c = col[:, None] * inv
    cos = np.concatenate([np.cos(ang_r), np.cos(ang_r), np.cos(ang_c), np.cos(ang_c)], axis=1)
    sin = np.concatenate([-np.sin(ang_r), np.sin(ang_r), -np.sin(ang_c), np.sin(ang_c)], axis=1)
    cos = np.concatenate([np.tile(cos, (1, 2)), np.ones((tm, LANES), np.float32)], axis=0)
    sin = np.concatenate([np.tile(sin, (1, 2)), np.zeros((tm, LANES), np.float32)], axis=0)
    return jnp.asarray(cos, jnp.float32), jnp.asarray(sin, jnp.float32)


def _norm_rope(y, ones_ref, nw, cos, sin, scale):
    ss = jnp.dot((y * y).astype(jnp.bfloat16), ones_ref[...], preferred_element_type=jnp.float32)
    y = y * lax.rsqrt(ss * (1.0 / SW_DH) + EPS) * nw
    first = (lax.broadcasted_iota(jnp.int32, (y.shape[0], LANES), 1) & 16) == 0
    pieces = []
    for g in range(y.shape[1] // LANES):
        yg = y[:, g * LANES:(g + 1) * LANES]
        partner = jnp.where(first, pltpu.roll(yg, LANES - 16, 1), pltpu.roll(yg, 16, 1))
        pieces.append((yg * cos + partner * sin) * scale)
    return pieces


def _q_kernel(x_ref, sh_ref, sc_ref, w_ref, nw_ref, ones_ref, cos_ref, sin_ref, o_ref):
    h = _modulate(x_ref[...], sh_ref[...], sc_ref[...]).astype(jnp.bfloat16)
    acc = jnp.dot(h, w_ref[...], preferred_element_type=jnp.float32)
    pieces = _norm_rope(acc, ones_ref, nw_ref[...], cos_ref[...], sin_ref[...], SW_DH ** -0.5 * LOG2E)
    for g, piece in enumerate(pieces):
        o_ref[:, g * LANES:(g + 1) * LANES] = piece.astype(o_ref.dtype)


def _kv_kernel(x_ref, sh_ref, sc_ref, w_ref, nw_ref, ones_ref, cos_ref, sin_ref, o_ref):
    h = _modulate(x_ref[...], sh_ref[...], sc_ref[...]).astype(jnp.bfloat16)
    acc = jnp.dot(h, w_ref[...], preferred_element_type=jnp.float32)
    nkv = SW_KV * SW_DH
    k_pieces = _norm_rope(acc[:, :nkv], ones_ref, nw_ref[...], cos_ref[...], sin_ref[...], 1.0)
    v_pieces = [acc[:, nkv + g * LANES:nkv + (g + 1) * LANES] for g in range(nkv // LANES)]
    low = lax.broadcasted_iota(jnp.int32, (acc.shape[0], LANES), 1) < SW_DH
    for base, pieces in ((0, k_pieces), (D, v_pieces)):
        for c, piece in enumerate(pieces):
            swapped = pltpu.roll(piece, SW_DH, 1)
            for half, rep in enumerate((jnp.where(low, piece, swapped), jnp.where(low, swapped, piece))):
                rep = rep.astype(o_ref.dtype)
                col = base + (2 * c + half) * SW_GROUP * SW_DH
                o_ref[:, col:col + LANES] = rep
                o_ref[:, col + LANES:col + 2 * LANES] = rep


def _qkv_proj(x, mods, w_qkv, q_norm, k_norm, *, geom, tm):
    nb, nct_rows, seq = geom
    t = x.shape[0]
    nct, lt = nct_rows // tm, seq // tm
    nkv = SW_KV * SW_DH
    cos, sin = _rope_tables(seq, tm)
    ones = lambda n: jnp.asarray(np.kron(np.eye(n // SW_DH, dtype=np.float32),
                                         np.ones((SW_DH, SW_DH), np.float32)), jnp.bfloat16)
    tab = lambda i: (jnp.where(i < nct, lt, (i - nct) % lt), 0)
    once = pl.Buffered(1)

    def call(body, w, nw, n_norm, n_out, name):
        return pl.pallas_call(
            body,
            out_shape=jax.ShapeDtypeStruct((t, n_out), jnp.bfloat16),
            grid=(t // tm,),
            in_specs=[pl.BlockSpec((tm, D), lambda i: (i, 0)),
                      _mod_spec(0, nct, lt, nb),
                      _mod_spec(1, nct, lt, nb),
                      pl.BlockSpec(w.shape, lambda i: (0, 0), pipeline_mode=once),
                      pl.BlockSpec((1, n_norm), lambda i: (0, 0)),
                      pl.BlockSpec((n_norm, n_norm), lambda i: (0, 0), pipeline_mode=once),
                      pl.BlockSpec((tm, LANES), tab),
                      pl.BlockSpec((tm, LANES), tab)],
            out_specs=pl.BlockSpec((tm, n_out), lambda i: (i, 0)),
            name=name,
            compiler_params=_cparams(("arbitrary",)),
        )(x, mods, mods, w, nw, ones(n_norm), cos, sin)

    bf = jnp.bfloat16
    q = call(_q_kernel, w_qkv[:, :D].astype(bf), jnp.tile(q_norm, SW_HEADS)[None, :], D, D, "q_proj")
    kv = call(_kv_kernel, w_qkv[:, D:].astype(bf), jnp.tile(k_norm, SW_KV)[None, :], nkv, 2 * D, "kv_proj")
    return q, kv


def _attn_items(items, sink_ref, kv, o_ref):
    nt = (((1,), (1,)), ((), ()))
    work = [(it, g) for it in items for g in range(SW_GROUP)]
    scores = []
    for (_, q, kcat, _, _), g in work:
        grp = lax.broadcasted_iota(jnp.int32, q.shape, 1) // SW_DH
        scores.append(lax.dot_general(jnp.where(grp == g, q, jnp.zeros_like(q)), kcat, nt,
                                      preferred_element_type=jnp.float32))
    probs, denoms = [], []
    for ((_, _, _, _, valid), g), s in zip(work, scores):
        sink = sink_ref[kv * SW_GROUP + g] * LOG2E
        if valid is not None:
            s = jnp.where(valid, s, NEG_BIG)
        m = jnp.maximum(jnp.max(s, axis=-1, keepdims=True), sink)
        p = jnp.exp2(s - m)
        denoms.append(jnp.sum(p, axis=-1, keepdims=True) + jnp.exp2(sink - m))
        probs.append(p.astype(jnp.bfloat16))
    outs = {}
    for n, ((rows, q, _, vcat, _), g) in enumerate(work):
        grp = lax.broadcasted_iota(jnp.int32, q.shape, 1) // SW_DH
        og = jnp.dot(probs[n], vcat, preferred_element_type=jnp.float32) / denoms[n]
        outs[rows] = og if g == 0 else jnp.where(grp == g, og, outs[rows])
    for rows, out in outs.items():
        o_ref[rows[0]:rows[1], :] = out.astype(o_ref.dtype)


def _attn_kernel(sink_ref, q_ref, kp_ref, kc_ref, kn_ref, kx_ref, vp_ref, vc_ref, vn_ref, vx_ref, o_ref,
                 *, nstep, ctx_out):
    kv, j = pl.program_id(1), pl.program_id(2)
    blk = SW_BLOCK

    def latent():
        kx, vx = kx_ref[...], vx_ref[...]
        k0, k1, v0, v1 = kc_ref[0:blk, :], kc_ref[blk:2 * blk, :], vc_ref[0:blk, :], vc_ref[blk:2 * blk, :]
        ns = 3 * blk + kx.shape[0]
        t = lax.broadcasted_iota(jnp.int32, (blk, ns), 0)
        s = lax.broadcasted_iota(jnp.int32, (blk, ns), 1)

        def window(has_prev, has_next):
            lo = jnp.maximum(t, jnp.where(has_prev, 0, blk))
            hi = jnp.minimum(t + 2 * blk, jnp.where(has_next, 3 * blk - 1, 2 * blk - 1))
            return ((s >= lo) & (s <= hi)) | (s >= 3 * blk)

        items = [((0, blk), q_ref[0:blk, :], jnp.concatenate([kp_ref[...], k0, k1, kx], axis=0),
                  jnp.concatenate([vp_ref[...], v0, v1, vx], axis=0), window(j > 0, True)),
                 ((blk, 2 * blk), q_ref[blk:2 * blk, :], jnp.concatenate([k0, k1, kn_ref[...], kx], axis=0),
                  jnp.concatenate([v0, v1, vn_ref[...], vx], axis=0), window(True, j < nstep - 1))]
        _attn_items(items, sink_ref, kv, o_ref)

    if not ctx_out:
        latent()
        return
    pl.when(j < nstep)(latent)

    @pl.when(j >= nstep)
    def _():
        _attn_items([((0, 2 * blk), q_ref[...], kx_ref[...], vx_ref[...], None)], sink_ref, kv, o_ref)


def _attention(q, kv, sink, geom, ctx_out):
    nb, nct_rows, seq = geom
    ctx_len = nct_rows // nb
    pair = 2 * SW_BLOCK
    assert ctx_len == pair and seq % pair == 0
    nblk = seq // SW_BLOCK
    nstep = seq // pair
    lat0 = nct_rows // SW_BLOCK
    w = SW_GROUP * SW_DH
    smem = pl.BlockSpec(memory_space=pltpu.SMEM)

    def qrow(b, j):
        return jnp.where(j < nstep, nct_rows // pair + b * nstep + j, b)

    def edge(off, colblock):
        return pl.BlockSpec((SW_BLOCK, w), lambda b, kv, j: (
            lat0 + b * nblk + jnp.clip(2 * jnp.minimum(j, nstep - 1) + off, 0, nblk - 1), colblock * SW_KV + kv))

    def center(colblock):
        return pl.BlockSpec((pair, w), lambda b, kv, j: (qrow(b, jnp.minimum(j, nstep - 1)), colblock * SW_KV + kv))

    def ctxkv(colblock):
        return pl.BlockSpec((ctx_len, w), lambda b, kv, j: (b, colblock * SW_KV + kv))

    if ctx_out:
        steps, out_rows = nstep + 1, q.shape[0]
        out_spec = pl.BlockSpec((pair, w), lambda b, kv, j: (qrow(b, j), kv))
    else:
        steps, out_rows = nstep, nb * seq
        out_spec = pl.BlockSpec((pair, w), lambda b, kv, j: (b * nstep + j, kv))
    return pl.pallas_call(
        functools.partial(_attn_kernel, nstep=nstep, ctx_out=ctx_out),
        out_shape=jax.ShapeDtypeStruct((out_rows, D), jnp.bfloat16),
        grid=(nb, SW_KV, steps),
        in_specs=[smem, pl.BlockSpec((pair, w), lambda b, kv, j: (qrow(b, j), kv)),
                  edge(-1, 0), center(0), edge(2, 0), ctxkv(0),
                  edge(-1, 1), center(1), edge(2, 1), ctxkv(1)],
        out_specs=out_spec,
        name="attention",
        compiler_params=_cparams(("arbitrary", "arbitrary", "arbitrary")),
    )(sink, q, kv, kv, kv, kv, kv, kv, kv, kv)


def _router_kernel(x_ref, sh_ref, sc_ref, r_ref, h_ref, info_ref):
    h = _modulate(x_ref[...], sh_ref[...], sc_ref[...])
    h_ref[...] = h
    logits = jnp.dot(h, r_ref[...], precision=lax.Precision.HIGHEST, preferred_element_type=jnp.float32)
    lane = lax.broadcasted_iota(jnp.int32, logits.shape, 1).astype(jnp.float32)
    logits = jnp.where(lane < N_EXPERTS, logits, -jnp.inf)
    l1 = jnp.max(logits, axis=-1, keepdims=True)
    i1 = jnp.min(jnp.where(logits == l1, lane, float(LANES)), axis=-1, keepdims=True)
    rest = jnp.where(lane == i1, -jnp.inf, logits)
    l2 = jnp.max(rest, axis=-1, keepdims=True)
    i2 = jnp.min(jnp.where(rest == l2, lane, float(LANES)), axis=-1, keepdims=True)
    w1 = 1.0 / (1.0 + jnp.exp(l2 - l1))
    info_ref[...] = jnp.where(lane == 0.0, i1, jnp.where(lane == 1.0, i2, jnp.where(lane == 2.0, w1, 1.0 - w1)))


def _router(x, mods, router_pad, *, geom, tm):
    nb, nct_rows, seq = geom
    t = x.shape[0]
    nct, lt = nct_rows // tm, seq // tm
    return pl.pallas_call(
        _router_kernel,
        out_shape=(jax.ShapeDtypeStruct((t, D), jnp.float32), jax.ShapeDtypeStruct((t, LANES), jnp.float32)),
        grid=(t // tm,),
        in_specs=[pl.BlockSpec((tm, D), lambda i: (i, 0)), _mod_spec(3, nct, lt, nb), _mod_spec(4, nct, lt, nb),
                  pl.BlockSpec((D, LANES), lambda i: (0, 0))],
        out_specs=(pl.BlockSpec((tm, D), lambda i: (i, 0)), pl.BlockSpec((tm, LANES), lambda i: (i, 0))),
        name="moe_router",
        compiler_params=_cparams(("arbitrary",)),
    )(x, mods, mods, router_pad)


FFN_SUB = 512


def _swiglu(h, wgu_ref, wd_ref):
    fdim = wd_ref.shape[0]
    bounds = list(range(0, fdim, FFN_SUB)) + [fdim]
    blocks = list(zip(bounds[:-1], bounds[1:]))
    up = lambda lo, hi: (jnp.dot(h, wgu_ref[:, lo:hi], preferred_element_type=jnp.float32),
                         jnp.dot(h, wgu_ref[:, fdim + lo:fdim + hi], preferred_element_type=jnp.float32))
    down = lambda gu, lo, hi: jnp.dot((gu[0] * _sigmoid(gu[0]) * gu[1]).astype(jnp.bfloat16), wd_ref[lo:hi, :],
                                      preferred_element_type=jnp.float32)
    gu = up(*blocks[0])
    acc = None
    for prev, cur in zip(blocks[:-1], blocks[1:]):
        gu_next = up(*cur)
        y = down(gu, *prev)
        acc = y if acc is None else acc + y
        gu = gu_next
    y = down(gu, *blocks[-1])
    return y if acc is None else acc + y


def _ffn_kernel(x_ref, sh_ref, sc_ref, gate_ref, wgu_ref, wd_ref, o_ref):
    x = x_ref[...]
    h = _modulate(x, sh_ref[...], sc_ref[...]).astype(jnp.bfloat16)
    o_ref[...] = x + gate_ref[...] * _swiglu(h, wgu_ref, wd_ref)


def _ffn(x, mods, wgu, wd, layer, *, geom, tm):
    nb, nct_rows, seq = geom
    t = x.shape[0]
    nct, lt = nct_rows // tm, seq // tm
    once = pl.Buffered(1)
    return pl.pallas_call(
        _ffn_kernel,
        out_shape=jax.ShapeDtypeStruct((t, D), jnp.float32),
        grid=(t // tm,),
        in_specs=[pl.BlockSpec((tm, D), lambda i: (i, 0)),
                  _mod_spec(3, nct, lt, nb),
                  _mod_spec(4, nct, lt, nb),
                  _mod_spec(5, nct, lt, nb),
                  pl.BlockSpec((None,) + wgu.shape[1:], lambda i: (layer, 0, 0), pipeline_mode=once),
                  pl.BlockSpec((None,) + wd.shape[1:], lambda i: (layer, 0, 0), pipeline_mode=once)],
        out_specs=pl.BlockSpec((tm, D), lambda i: (i, 0)),
        name="dense_ffn",
        compiler_params=_cparams(("arbitrary",)),
    )(x, mods, mods, mods, wgu, wd)


def _moe_kernel(te_ref, nt_ref, rows0_ref, rows1_ref, rows2_ref, outrows_ref, h_hbm, wgu_ref, wd_ref, y_hbm,
                hbuf, hb_ref, obuf, gsem, ssem, *, tm):
    del te_ref
    i = pl.program_id(0)
    nt = nt_ref[0]
    g_slot, o_slot = i % 3, i % 2

    def gather(rows_ref, s):
        for r in range(tm):
            pltpu.make_async_copy(h_hbm.at[pl.ds(rows_ref[0, r], 1), :], hbuf.at[s, pl.ds(r, 1), :],
                                  gsem.at[s]).start()

    def scatter_prev():
        for r in range(tm):
            pltpu.make_async_copy(obuf.at[1 - o_slot, pl.ds(r, 1), :], y_hbm.at[pl.ds(outrows_ref[0, r], 1), :],
                                  ssem.at[1 - o_slot]).start()

    @pl.when(i == 0)
    def _():
        gather(rows0_ref, 0)
        gather(rows1_ref, 1)
        obuf[1] = jnp.zeros((tm, D), jnp.float32)

    @pl.when((i >= 1) & (i - 2 < nt))
    def _():
        pltpu.make_async_copy(obuf.at[o_slot], obuf.at[o_slot], ssem.at[o_slot]).wait()

    @pl.when(i < nt + 2)
    def _():
        pltpu.make_async_copy(hbuf.at[g_slot], hbuf.at[g_slot], gsem.at[g_slot]).wait()
        hb_ref[...] = hbuf[g_slot].astype(jnp.bfloat16)

    @pl.when(i < nt)
    def _():
        gather(rows2_ref, (i + 2) % 3)
        scatter_prev()
        obuf[o_slot] = _swiglu(hb_ref[...], wgu_ref, wd_ref)

    @pl.when(i == nt)
    def _():
        scatter_prev()


def _moe_experts(h, tile_expert, n_tiles, rows, outrows, wgu, wd, layer, n_out, *, tm):
    nt = rows.shape[0]
    fdim = wd.shape[2]
    assert nt >= 3
    smem_rows = lambda off: pl.BlockSpec((None, 1, tm), lambda i, te, n: (jnp.minimum(i + off, nt - 1), 0, 0),
                                         memory_space=pltpu.SMEM)
    once = pl.Buffered(1)
    grid_spec = pltpu.PrefetchScalarGridSpec(
        num_scalar_prefetch=2,
        grid=(nt,),
        in_specs=[smem_rows(0), smem_rows(1), smem_rows(2),
                  pl.BlockSpec((None, 1, tm), lambda i, te, n: (i, 0, 0), memory_space=pltpu.SMEM),
                  pl.BlockSpec(memory_space=pl.ANY),
                  pl.BlockSpec((None, None, D, 2 * fdim), lambda i, te, n: (layer, te[i], 0, 0), pipeline_mode=once),
                  pl.BlockSpec((None, None, fdim, D), lambda i, te, n: (layer, te[i], 0, 0))],
        out_specs=pl.BlockSpec(memory_space=pl.ANY),
        scratch_shapes=[pltpu.VMEM((3, tm, D), jnp.float32), pltpu.VMEM((tm, D), jnp.bfloat16),
                        pltpu.VMEM((2, tm, D), jnp.float32), pltpu.SemaphoreType.DMA((3,)),
                        pltpu.SemaphoreType.DMA((2,))])
    return pl.pallas_call(
        functools.partial(_moe_kernel, tm=tm),
        out_shape=jax.ShapeDtypeStruct((n_out, D), jnp.float32),
        grid_spec=grid_spec,
        name="moe_experts",
        compiler_params=_cparams(("arbitrary",)),
    )(tile_expert, n_tiles, rows, rows, rows, outrows, h, wgu, wd)


def _combine_kernel(y0_ref, y1_ref, x_ref, gate_ref, info_ref, o_ref):
    info = info_ref[...]
    y = info[:, 2:3] * y0_ref[...] + info[:, 3:4] * y1_ref[...]
    o_ref[...] = x_ref[...] + gate_ref[...] * y


def _moe_combine(x, y2, info, mods, *, geom, tmc):
    nb, nct_rows, seq = geom
    t = x.shape[0]
    nct, lt = nct_rows // tmc, seq // tmc
    return pl.pallas_call(
        _combine_kernel,
        out_shape=jax.ShapeDtypeStruct((t, D), jnp.float32),
        grid=(t // tmc,),
        in_specs=[pl.BlockSpec((None, tmc, D), lambda i: (0, i, 0)),
                  pl.BlockSpec((None, tmc, D), lambda i: (1, i, 0)),
                  pl.BlockSpec((tmc, D), lambda i: (i, 0)),
                  _mod_spec(5, nct, lt, nb),
                  pl.BlockSpec((tmc, LANES), lambda i: (i, 0))],
        out_specs=pl.BlockSpec((tmc, D), lambda i: (i, 0)),
        name="moe_combine",
        compiler_params=_cparams(("arbitrary",)),
    )(y2, y2, x, mods, info)


def _moe(x, mods, router_pad, wgu, wd, layer, *, geom, tm, tmc):
    tr = x.shape[0]
    h, info = _router(x, mods, router_pad, geom=geom, tm=tm)
    e = info[:, 0:2].astype(jnp.int32).reshape(-1)
    onehot = (e[:, None] == jnp.arange(N_EXPERTS, dtype=jnp.int32)[None, :]).astype(jnp.int32)
    csum = jnp.cumsum(onehot, axis=0)
    rank = jnp.sum(onehot * (csum - 1), axis=1)
    counts = csum[-1]
    padded = ((counts + tm - 1) // tm) * tm
    ends = jnp.cumsum(padded)
    dest = (ends - padded)[e] + rank
    nt = 2 * tr // tm + N_EXPERTS + 2
    tile_expert = jnp.minimum(jnp.sum(jnp.arange(nt, dtype=jnp.int32)[:, None] * tm >= ends[None, :], axis=1),
                              N_EXPERTS - 1).astype(jnp.int32)
    n_tiles = (ends[-1:] // tm).astype(jnp.int32)
    slot_of_row = jnp.full((nt * tm,), -1, jnp.int32).at[dest].set(jnp.arange(2 * tr, dtype=jnp.int32),
                                                                   unique_indices=True)
    tp = tr + tm // 2
    j = jnp.arange(nt * tm, dtype=jnp.int32) % tm
    spare = jnp.where(j < tm // 2, tr + j, tp + tr + j - tm // 2)
    out_row = (slot_of_row % 2) * tp + slot_of_row // 2
    rows = (jnp.maximum(slot_of_row, 0) // 2).reshape(nt, 1, tm)
    outrows = jnp.concatenate([spare[:tm], jnp.where(slot_of_row >= 0, out_row, spare)]).reshape(nt + 1, 1, tm)
    y = _moe_experts(h, tile_expert, n_tiles, rows, outrows, wgu, wd, layer, 2 * tp, tm=tm)
    return _moe_combine(x, y.reshape(2, tp, D), info, mods, geom=geom, tmc=tmc)


def kernel(x, c, ctx, c_ctx, w_mod, b_mod, hg_w_in, hg_lb_logits, hg_norm_w, hg_w_out, sw_w_qkv, sw_q_norm,
           sw_k_norm, sw_sink, sw_w_out, ff_w_gate_up, ff_w_down, moe_router, moe_w_gate_up, moe_w_down):
    nb, seq, _ = x.shape
    ctx_len = ctx.shape[1]
    depth = w_mod.shape[0]
    nct_rows = nb * ctx_len
    geom = (nb, nct_rows, seq)
    tm = 512
    tmb = 1024 if nct_rows % 1024 == 0 and seq % 1024 == 0 else tm
    bf = jnp.bfloat16

    xs = jnp.concatenate([ctx.reshape(nct_rows, D), x.reshape(nb * seq, D)], axis=0)
    cpad = jnp.concatenate([c, c_ctx[None, :], jnp.zeros((8 - nb - 1, D), jnp.float32)], axis=0)
    mods_all = _modvecs(cpad, w_mod, b_mod)

    p_lb = jax.nn.softmax(hg_lb_logits.astype(jnp.float32), axis=0)
    lower_bounds = jnp.cumsum(p_lb, axis=0) - p_lb[:1]
    hg_w_in_bf, ff_wgu_bf, ff_wd_bf = hg_w_in.astype(bf), ff_w_gate_up.astype(bf), ff_w_down.astype(bf)
    moe_wgu_bf, moe_wd_bf = moe_w_gate_up.astype(bf), moe_w_down.astype(bf)

    for i in range(depth):
        ctx_live = i < depth - 1
        mods = mods_all[i]
        j = i // 2
        if i % 2 == 0:
            p = _mod_matmul(xs, mods, hg_w_in_bf, j, k_shift=0, geom=geom, tm=tm, tn=1024, out_dtype=jnp.float32)
            of, ob = _hg_scan(p, lower_bounds[j, 0:1], lower_bounds[j, 1:2], geom)
            xs = _hg_out(of, ob, p, hg_norm_w[j][None, :], hg_w_out[j].astype(bf), xs, mods,
                         geom=geom, tm=tmb, tn=1024)
        else:
            q, kv = _qkv_proj(xs, mods, sw_w_qkv[j], sw_q_norm[j], sw_k_norm[j], geom=geom, tm=tmb)
            o = _attention(q, kv, sw_sink[j], geom, ctx_live)
            xs = _res_matmul(o, sw_w_out[j].astype(bf), xs, mods, geom=geom, tm=tmb, tn=1024)
            if not ctx_live:
                geom = (nb, 0, seq)
        if i % 2 == 0:
            xs = _ffn(xs, mods, ff_wgu_bf, ff_wd_bf, j, geom=geom, tm=tm)
        else:
            rpad = jnp.pad(moe_router[j], ((0, 0), (0, LANES - N_EXPERTS)))
            xs = _moe(xs, mods, rpad, moe_wgu_bf, moe_wd_bf, j, geom=geom, tm=tm, tmc=tm)
    return xs[xs.shape[0] - nb * seq:].reshape(nb, seq, D)
```

```python
import functools

import numpy as np
import jax
import jax.numpy as jnp
from jax import lax
from jax.experimental import pallas as pl
from jax.experimental.pallas import tpu as pltpu

D = 1024
EPS = 1e-6
NEG_BIG = -1e30
LOG2E = 1.4426950408889634
GRID_W = 64
ROPE_THETA = 10000.0

HG_HEADS = 8
HG_DK = 128
HG_FDIM = HG_HEADS * HG_DK
HG_CHUNK = 128
HG_LEVELS = 7
HG_ROWS = 256
HG_HB = 4

SW_HEADS = 16
SW_KV = 4
SW_GROUP = 4
SW_DH = 64
SW_BLOCK = 128

N_EXPERTS = 8
LANES = 128

VMEM_LIMIT = 56 * 1024 * 1024


def _cparams(sem):
    return pltpu.CompilerParams(dimension_semantics=sem, vmem_limit_bytes=VMEM_LIMIT)


def _sigmoid(x):
    return 1.0 / (1.0 + jnp.exp2(x * -LOG2E))


def _mod_row(i, nct, lt, nb):
    return jnp.where(i < nct, nb, (i - nct) // lt)


def _mod_spec(k, nct, lt, nb):
    return pl.BlockSpec((None, 1, D), lambda i, *_: (_mod_row(i, nct, lt, nb) * 6 + k, 0, 0))


def _modulate(x, shift, scale):
    ms = jnp.mean(x * x, axis=-1, keepdims=True)
    return (x * lax.rsqrt(ms + EPS)) * (1.0 + scale) + shift


def _modvec_kernel(c_ref, w_ref, b_ref, o_ref):
    c = c_ref[...]
    s = c * _sigmoid(c)
    o_ref[...] = jnp.dot(s, w_ref[...], precision=lax.Precision.HIGHEST,
                         preferred_element_type=jnp.float32) + b_ref[...]


def _modvecs(cpad, w_mod, b_mod):
    depth = w_mod.shape[0]
    tn = 1024
    out = pl.pallas_call(
        _modvec_kernel,
        out_shape=jax.ShapeDtypeStruct((depth, 8, 6 * D), jnp.float32),
        grid=(depth, 6 * D // tn),
        in_specs=[pl.BlockSpec((8, D), lambda l, j: (0, 0)),
                  pl.BlockSpec((None, D, tn), lambda l, j: (l, 0, j)),
                  pl.BlockSpec((None, 1, tn), lambda l, j: (l, 0, j))],
        out_specs=pl.BlockSpec((None, 8, tn), lambda l, j: (l, 0, j)),
        name="adaln_vectors",
        compiler_params=_cparams(("arbitrary", "arbitrary")),
    )(cpad, w_mod, b_mod.reshape(depth, 1, 6 * D))
    return out.reshape(depth, 8 * 6, 1, D)


def _modmm_kernel(x_ref, sh_ref, sc_ref, w_ref, o_ref, *, tn):
    h = _modulate(x_ref[...], sh_ref[...], sc_ref[...]).astype(jnp.bfloat16)
    for n0 in range(0, w_ref.shape[1], tn):
        o_ref[:, n0:n0 + tn] = jnp.dot(h, w_ref[:, n0:n0 + tn], preferred_element_type=jnp.float32).astype(o_ref.dtype)


def _mod_matmul(x, mods, w, layer, *, k_shift, geom, tm, tn, out_dtype):
    nb, nct_rows, seq = geom
    t, n = x.shape[0], w.shape[2]
    nct, lt = nct_rows // tm, seq // tm
    return pl.pallas_call(
        functools.partial(_modmm_kernel, tn=tn),
        out_shape=jax.ShapeDtypeStruct((t, n), out_dtype),
        grid=(t // tm,),
        in_specs=[pl.BlockSpec((tm, D), lambda i: (i, 0)),
                  _mod_spec(k_shift, nct, lt, nb),
                  _mod_spec(k_shift + 1, nct, lt, nb),
                  pl.BlockSpec((None,) + w.shape[1:], lambda i: (layer, 0, 0), pipeline_mode=pl.Buffered(1))],
        out_specs=pl.BlockSpec((tm, n), lambda i: (i, 0)),
        name="hg_in_proj",
        compiler_params=_cparams(("arbitrary",)),
    )(x, mods, mods, w)


def _hg_tables(fwd):
    c = HG_CHUNK
    t = np.arange(c)[:, None]
    u = np.arange(c)[None, :]
    reach = (u <= t) if fwd else (u >= t)
    x = t ^ u
    lvl = np.where(x > 0, np.floor(np.log2(np.maximum(x, 1))), HG_LEVELS)
    lvl = np.where(reach, lvl, -1).astype(np.int32)
    tri = np.concatenate([reach.astype(np.float32)] * 3, axis=1)
    return tri, lvl


def _hg_gates(q_raw, z, lb, tri_ref, a_ref):
    bf = jnp.bfloat16
    q = q_raw * _sigmoid(q_raw)
    sig = _sigmoid(z)
    fc = jnp.maximum(lb + (1.0 - lb) * sig, 1e-30)
    lf = jnp.log(fc) * LOG2E
    k = (1.0 - lb) * (1.0 - sig)
    hi = lf.astype(bf)
    rest = lf - hi.astype(jnp.float32)
    mid = rest.astype(bf)
    lo = (rest - mid.astype(jnp.float32)).astype(bf)
    a = jnp.dot(tri_ref[...], jnp.concatenate([hi, mid, lo], axis=0), preferred_element_type=jnp.float32)
    a_ref[...] = a
    return q, k, fc, a


def _hg_scores(q, k, fc, a, a_ref, lvl, fwd):
    c = HG_CHUNK
    bf = jnp.bfloat16

    def rows_of(row, n):
        return jnp.broadcast_to(a_ref[row:row + 1, :], (n, HG_DK))

    rowi = lax.broadcasted_iota(jnp.int32, (c, HG_DK), 0)
    nt = (((1,), (1,)), ((), ()))
    zeros8 = jnp.zeros((8, HG_DK), jnp.float32)

    def level_operands(l):
        w = 1 << l
        near = w - 1 if fwd else w
        if w < 8:
            upper = ((rowi >> l) & 1) == 1
            q_side = upper if fwd else jnp.logical_not(upper)
            if l == 0:
                qe, ke = q * fc, k
            else:
                if l == 1:
                    first = jnp.concatenate([rows_of(8 * g + near, 8) for g in range(c // 8)], axis=0)
                    second = jnp.concatenate([rows_of(8 * g + 4 + near, 8) for g in range(c // 8)], axis=0)
                    bnd = jnp.where((rowi & 4) == 0, first, second)
                else:
                    bnd = jnp.concatenate([rows_of(8 * g + near, 8) for g in range(c // 8)], axis=0)
                e = jnp.exp2(-jnp.abs(a - bnd))
                qe, ke = q * e, k * e
            return jnp.where(q_side, qe, 0.0).astype(bf), jnp.where(q_side, 0.0, ke).astype(bf)
        qp, kp = [], []
        for j in range(c // w):
            rs = slice(j * w, (j + 1) * w)
            bnd = rows_of((j // 2) * 2 * w + near, w)
            zero = jnp.concatenate([zeros8] * (w // 8), axis=0)
            if (j % 2 == 1) == fwd:
                qp.append(q[rs] * jnp.exp2(a[rs] - bnd))
                kp.append(zero)
            else:
                qp.append(zero)
                kp.append(k[rs] * jnp.exp2(bnd - a[rs]))
        return jnp.concatenate(qp, axis=0).astype(bf), jnp.concatenate(kp, axis=0).astype(bf)

    scores = jnp.where(lvl == HG_LEVELS,
                       lax.dot_general(q.astype(bf), k.astype(bf), nt, preferred_element_type=jnp.float32), 0.0)
    for l in range(HG_LEVELS):
        qm, km = level_operands(l)
        sc = lax.dot_general(qm, km, nt, preferred_element_type=jnp.float32)
        scores = jnp.where(lvl == l, sc, scores)
    return scores.astype(bf)


def _hg_finish(q, k, v, a, scores, st_ref, fwd):
    c = HG_CHUNK
    bf = jnp.bfloat16
    nt = (((1,), (1,)), ((), ()))
    last = c - 1 if fwd else 0
    e_in = jnp.exp2(a)
    e_out = jnp.exp2(a[last:last + 1, :] - a)
    st = st_ref[...]
    o = (jnp.dot(scores, v.astype(bf), preferred_element_type=jnp.float32)
         + lax.dot_general((q * e_in).astype(bf), st.astype(bf), nt, preferred_element_type=jnp.float32))
    st_ref[...] = (e_in[last:last + 1, :] * st
                   + jnp.dot(v.T.astype(bf), (k * e_out).astype(bf), preferred_element_type=jnp.float32))
    return o


def _hg_scan_kernel(qf_ref, vf_ref, zf_ref, qb_ref, vb_ref, zb_ref, lbf_ref, lbb_ref, trif_ref, trib_ref,
                    lvlf_ref, lvlb_ref, of_ref, ob_ref, sf_ref, sb_ref, a_ref):
    @pl.when(pl.program_id(2) == 0)
    def _():
        sf_ref[...] = jnp.zeros_like(sf_ref)
        sb_ref[...] = jnp.zeros_like(sb_ref)

    c = HG_CHUNK
    nch = HG_ROWS // c
    lvlf, lvlb = lvlf_ref[...], lvlb_ref[...]
    units = []
    for hh in range(HG_HB):
        cs = slice(hh * HG_DK, (hh + 1) * HG_DK)
        for step, ci in enumerate(range(nch)):
            units.append(dict(step=step, rows=slice(ci * c, (ci + 1) * c), cols=cs, fwd=True, q=qf_ref, v=vf_ref,
                              z=zf_ref, lb=lbf_ref, tri=trif_ref, lvl=lvlf, st=sf_ref.at[hh], out=of_ref))
        for step, ci in enumerate(reversed(range(nch))):
            units.append(dict(step=step, rows=slice(ci * c, (ci + 1) * c), cols=cs, fwd=False, q=qb_ref, v=vb_ref,
                              z=zb_ref, lb=lbb_ref, tri=trib_ref, lvl=lvlb, st=sb_ref.at[hh], out=ob_ref))
    for n, u in enumerate(units):
        u["a_ref"] = a_ref.at[n]
        u["q"], u["k"], u["fc"], u["a"] = _hg_gates(u["q"][u["rows"], u["cols"]], u["z"][u["rows"], u["cols"]],
                                                    u["lb"][:, u["cols"]], u["tri"], u["a_ref"])
    for u in units:
        u["scores"] = _hg_scores(u["q"], u["k"], u["fc"], u["a"], u["a_ref"], u["lvl"], u["fwd"])
    for step in range(nch):
        for u in units:
            if u["step"] == step:
                o = _hg_finish(u["q"], u["k"], u["v"][u["rows"], u["cols"]], u["a"], u["scores"], u["st"], u["fwd"])
                u["out"][u["rows"], u["cols"]] = o.astype(u["out"].dtype)


def _hg_scan(p, lbf, lbb, geom):
    nb, nct_rows, seq = geom
    t = p.shape[0]
    r = HG_ROWS
    cb, lb_ = (nct_rows // nb) // r, seq // r
    nsteps = cb + lb_
    lat0 = nct_rows // r

    def fblk(b, s):
        return jnp.where(s < cb, b * cb + s, lat0 + b * lb_ + (s - cb))

    def bblk(b, s):
        return jnp.where(s < cb, b * cb + (cb - 1 - s), lat0 + b * lb_ + (lb_ - 1 - (s - cb)))

    wcol = HG_HB * HG_DK
    ngrp = HG_HEADS // HG_HB
    units = 2 * HG_HB * (r // HG_CHUNK)

    def spec(blk, colblock):
        return pl.BlockSpec((r, wcol), lambda b, h, s: (blk(b, s), colblock * ngrp + h))

    lbspec = pl.BlockSpec((1, wcol), lambda b, h, s: (0, h))
    const = lambda shape: pl.BlockSpec(shape, lambda b, h, s: (0, 0))
    trif, lvlf = _hg_tables(True)
    trib, lvlb = _hg_tables(False)
    return pl.pallas_call(
        _hg_scan_kernel,
        out_shape=(jax.ShapeDtypeStruct((t, D), jnp.bfloat16), jax.ShapeDtypeStruct((t, D), jnp.bfloat16)),
        grid=(nb, ngrp, nsteps),
        in_specs=[spec(fblk, 0), spec(fblk, 1), spec(fblk, 2),
                  spec(bblk, 0), spec(bblk, 1), spec(bblk, 3),
                  lbspec, lbspec, const(trif.shape), const(trib.shape), const(lvlf.shape), const(lvlb.shape)],
        out_specs=(pl.BlockSpec((r, wcol), lambda b, h, s: (fblk(b, s), h)),
                   pl.BlockSpec((r, wcol), lambda b, h, s: (bblk(b, s), h))),
        scratch_shapes=[pltpu.VMEM((HG_HB, HG_DK, HG_DK), jnp.float32),
                        pltpu.VMEM((HG_HB, HG_DK, HG_DK), jnp.float32),
                        pltpu.VMEM((units, HG_CHUNK, HG_DK), jnp.float32)],
        name="hg_scan",
        compiler_params=_cparams(("arbitrary", "arbitrary", "arbitrary")),
    )(p, p, p, p, p, p, lbf, lbb, jnp.asarray(trif, jnp.bfloat16), jnp.asarray(trib, jnp.bfloat16),
      jnp.asarray(lvlf), jnp.asarray(lvlb))


def _hg_out_kernel(of_ref, ob_ref, g_ref, nw_ref, w_ref, x_ref, gate_ref, o_ref, h_ref):
    @pl.when(pl.program_id(1) == 0)
    def _():
        nw = nw_ref[...]
        for h in range(HG_HEADS):
            cs = slice(h * HG_DK, (h + 1) * HG_DK)
            o = of_ref[:, cs].astype(jnp.float32) + ob_ref[:, cs].astype(jnp.float32)
            y = o * lax.rsqrt(jnp.mean(o * o, axis=-1, keepdims=True) + EPS) * nw
            g = g_ref[:, cs]
            h_ref[:, cs] = (y * (g * _sigmoid(g))).astype(jnp.bfloat16)

    acc = jnp.dot(h_ref[...], w_ref[...], preferred_element_type=jnp.float32)
    o_ref[...] = x_ref[...] + gate_ref[...] * acc


def _hg_out(of, ob, p, norm_w, w, x, mods, *, geom, tm, tn):
    nb, nct_rows, seq = geom
    t = x.shape[0]
    nct, lt = nct_rows // tm, seq // tm
    gate = pl.BlockSpec((None, 1, tn), lambda i, j: (_mod_row(i, nct, lt, nb) * 6 + 2, 0, j))
    return pl.pallas_call(
        _hg_out_kernel,
        out_shape=jax.ShapeDtypeStruct((t, D), jnp.float32),
        grid=(t // tm, D // tn),
        in_specs=[pl.BlockSpec((tm, D), lambda i, j: (i, 0)),
                  pl.BlockSpec((tm, D), lambda i, j: (i, 0)),
                  pl.BlockSpec((tm, D), lambda i, j: (i, 4)),
                  pl.BlockSpec((1, HG_DK), lambda i, j: (0, 0)),
                  pl.BlockSpec((D, tn), lambda i, j: (0, j)),
                  pl.BlockSpec((tm, tn), lambda i, j: (i, j)),
                  gate],
        out_specs=pl.BlockSpec((tm, tn), lambda i, j: (i, j)),
        scratch_shapes=[pltpu.VMEM((tm, D), jnp.bfloat16)],
        name="hg_out_proj",
        compiler_params=_cparams(("arbitrary", "arbitrary")),
    )(of, ob, p, norm_w, w, x, mods)


def _resmm_kernel(a_ref, w_ref, x_ref, gate_ref, o_ref):
    acc = jnp.dot(a_ref[...], w_ref[...], preferred_element_type=jnp.float32)
    o_ref[...] = x_ref[...] + gate_ref[...] * acc


def _res_matmul(a, w, x, mods, *, geom, tm, tn):
    nb, nct_rows, seq = geom
    rows = a.shape[0]
    t0 = (x.shape[0] - rows) // tm
    nct, lt = nct_rows // tm, seq // tm
    gate = pl.BlockSpec((None, 1, tn), lambda i, j: (_mod_row(i + t0, nct, lt, nb) * 6 + 2, 0, j))
    return pl.pallas_call(
        _resmm_kernel,
        out_shape=jax.ShapeDtypeStruct((rows, D), jnp.float32),
        grid=(rows // tm, D // tn),
        in_specs=[pl.BlockSpec((tm, a.shape[1]), lambda i, j: (i, 0)),
                  pl.BlockSpec((a.shape[1], tn), lambda i, j: (0, j)),
                  pl.BlockSpec((tm, tn), lambda i, j: (i + t0, j)),
                  gate],
        out_specs=pl.BlockSpec((tm, tn), lambda i, j: (i, j)),
        name="attn_out_proj",
        compiler_params=_cparams(("arbitrary", "arbitrary")),
    )(a, w, x, mods)


def _rope_tables(seq, tm):
    rows = seq // GRID_W
    row = np.repeat(np.arange(rows, dtype=np.float32), GRID_W)
    col = np.tile(np.arange(GRID_W, dtype=np.float32), rows)
    nf = SW_DH // 4
    inv = (ROPE_THETA ** (-np.arange(nf, dtype=np.float32) / nf)).astype(np.float32)
    ang_r = row[:, None] * inv
    ang_c = col[:, None] * inv
    cos = np.concatenate([np.cos(ang_r), np.cos(ang_r), np.cos(ang_c), np.cos(ang_c)], axis=1)
    sin = np.concatenate([-np.sin(ang_r), np.sin(ang_r), -np.sin(ang_c), np.sin(ang_c)], axis=1)
    cos = np.concatenate([np.tile(cos, (1, 2)), np.ones((tm, LANES), np.float32)], axis=0)
    sin = np.concatenate([np.tile(sin, (1, 2)), np.zeros((tm, LANES), np.float32)], axis=0)
    return jnp.asarray(cos, jnp.float32), jnp.asarray(sin, jnp.float32)


def _norm_rope(y, ones_ref, nw, cos, sin, scale):
    ss = jnp.dot((y * y).astype(jnp.bfloat16), ones_ref[...], preferred_element_type=jnp.float32)
    y = y * lax.rsqrt(ss * (1.0 / SW_DH) + EPS) * nw
    first = (lax.broadcasted_iota(jnp.int32, (y.shape[0], LANES), 1) & 16) == 0
    pieces = []
    for g in range(y.shape[1] // LANES):
        yg = y[:, g * LANES:(g + 1) * LANES]
        partner = jnp.where(first, pltpu.roll(yg, LANES - 16, 1), pltpu.roll(yg, 16, 1))
        pieces.append((yg * cos + partner * sin) * scale)
    return pieces


def _q_kernel(x_ref, sh_ref, sc_ref, w_ref, nw_ref, ones_ref, cos_ref, sin_ref, o_ref):
    h = _modulate(x_ref[...], sh_ref[...], sc_ref[...]).astype(jnp.bfloat16)
    acc = jnp.dot(h, w_ref[...], preferred_element_type=jnp.float32)
    pieces = _norm_rope(acc, ones_ref, nw_ref[...], cos_ref[...], sin_ref[...], SW_DH ** -0.5 * LOG2E)
    for g, piece in enumerate(pieces):
        o_ref[:, g * LANES:(g + 1) * LANES] = piece.astype(o_ref.dtype)


def _kv_kernel(x_ref, sh_ref, sc_ref, w_ref, nw_ref, ones_ref, cos_ref, sin_ref, o_ref):
    h = _modulate(x_ref[...], sh_ref[...], sc_ref[...]).astype(jnp.bfloat16)
    acc = jnp.dot(h, w_ref[...], preferred_element_type=jnp.float32)
    nkv = SW_KV * SW_DH
    k_pieces = _norm_rope(acc[:, :nkv], ones_ref, nw_ref[...], cos_ref[...], sin_ref[...], 1.0)
    v_pieces = [acc[:, nkv + g * LANES:nkv + (g + 1) * LANES] for g in range(nkv // LANES)]
    low = lax.broadcasted_iota(jnp.int32, (acc.shape[0], LANES), 1) < SW_DH
    for base, pieces in ((0, k_pieces), (D, v_pieces)):
        for c, piece in enumerate(pieces):
            swapped = pltpu.roll(piece, SW_DH, 1)
            for half, rep in enumerate((jnp.where(low, piece, swapped), jnp.where(low, swapped, piece))):
                rep = rep.astype(o_ref.dtype)
                col = base + (2 * c + half) * SW_GROUP * SW_DH
                o_ref[:, col:col + LANES] = rep
                o_ref[:, col + LANES:col + 2 * LANES] = rep


def _qkv_proj(x, mods, w_qkv, q_norm, k_norm, *, geom, tm):
    nb, nct_rows, seq = geom
    t = x.shape[0]
    nct, lt = nct_rows // tm, seq // tm
    nkv = SW_KV * SW_DH
    cos, sin = _rope_tables(seq, tm)
    ones = lambda n: jnp.asarray(np.kron(np.eye(n // SW_DH, dtype=np.float32),
                                         np.ones((SW_DH, SW_DH), np.float32)), jnp.bfloat16)
    tab = lambda i: (jnp.where(i < nct, lt, (i - nct) % lt), 0)
    once = pl.Buffered(1)

    def call(body, w, nw, n_norm, n_out, name):
        return pl.pallas_call(
            body,
            out_shape=jax.ShapeDtypeStruct((t, n_out), jnp.bfloat16),
            grid=(t // tm,),
            in_specs=[pl.BlockSpec((tm, D), lambda i: (i, 0)),
                      _mod_spec(0, nct, lt, nb),
                      _mod_spec(1, nct, lt, nb),
                      pl.BlockSpec(w.shape, lambda i: (0, 0), pipeline_mode=once),
                      pl.BlockSpec((1, n_norm), lambda i: (0, 0)),
                      pl.BlockSpec((n_norm, n_norm), lambda i: (0, 0), pipeline_mode=once),
                      pl.BlockSpec((tm, LANES), tab),
                      pl.BlockSpec((tm, LANES), tab)],
            out_specs=pl.BlockSpec((tm, n_out), lambda i: (i, 0)),
            name=name,
            compiler_params=_cparams(("arbitrary",)),
        )(x, mods, mods, w, nw, ones(n_norm), cos, sin)

    bf = jnp.bfloat16
    q = call(_q_kernel, w_qkv[:, :D].astype(bf), jnp.tile(q_norm, SW_HEADS)[None, :], D, D, "q_proj")
    kv = call(_kv_kernel, w_qkv[:, D:].astype(bf), jnp.tile(k_norm, SW_KV)[None, :], nkv, 2 * D, "kv_proj")
    return q, kv


def _attn_items(items, sink_ref, kv, o_ref):
    nt = (((1,), (1,)), ((), ()))
    work = [(it, g) for it in items for g in range(SW_GROUP)]
    scores = []
    for (_, q, kcat, _, _), g in work:
        grp = lax.broadcasted_iota(jnp.int32, q.shape, 1) // SW_DH
        scores.append(lax.dot_general(jnp.where(grp == g, q, jnp.zeros_like(q)), kcat, nt,
                                      preferred_element_type=jnp.float32))
    probs, denoms = [], []
    for ((_, _, _, _, valid), g), s in zip(work, scores):
        sink = sink_ref[kv * SW_GROUP + g] * LOG2E
        if valid is not None:
            s = jnp.where(valid, s, NEG_BIG)
        m = jnp.maximum(jnp.max(s, axis=-1, keepdims=True), sink)
        p = jnp.exp2(s - m)
        denoms.append(jnp.sum(p, axis=-1, keepdims=True) + jnp.exp2(sink - m))
        probs.append(p.astype(jnp.bfloat16))
    outs = {}
    for n, ((rows, q, _, vcat, _), g) in enumerate(work):
        grp = lax.broadcasted_iota(jnp.int32, q.shape, 1) // SW_DH
        og = jnp.dot(probs[n], vcat, preferred_element_type=jnp.float32) / denoms[n]
        outs[rows] = og if g == 0 else jnp.where(grp == g, og, outs[rows])
    for rows, out in outs.items():
        o_ref[rows[0]:rows[1], :] = out.astype(o_ref.dtype)


def _attn_kernel(sink_ref, q_ref, kp_ref, kc_ref, kn_ref, kx_ref, vp_ref, vc_ref, vn_ref, vx_ref, o_ref,
                 *, nstep, ctx_out):
    kv, j = pl.program_id(1), pl.program_id(2)
    blk = SW_BLOCK

    def latent():
        kx, vx = kx_ref[...], vx_ref[...]
        k0, k1, v0, v1 = kc_ref[0:blk, :], kc_ref[blk:2 * blk, :], vc_ref[0:blk, :], vc_ref[blk:2 * blk, :]
        ns = 3 * blk + kx.shape[0]
        t = lax.broadcasted_iota(jnp.int32, (blk, ns), 0)
        s = lax.broadcasted_iota(jnp.int32, (blk, ns), 1)

        def window(has_prev, has_next):
            lo = jnp.maximum(t, jnp.where(has_prev, 0, blk))
            hi = jnp.minimum(t + 2 * blk, jnp.where(has_next, 3 * blk - 1, 2 * blk - 1))
            return ((s >= lo) & (s <= hi)) | (s >= 3 * blk)

        items = [((0, blk), q_ref[0:blk, :], jnp.concatenate([kp_ref[...], k0, k1, kx], axis=0),
                  jnp.concatenate([vp_ref[...], v0, v1, vx], axis=0), window(j > 0, True)),
                 ((blk, 2 * blk), q_ref[blk:2 * blk, :], jnp.concatenate([k0, k1, kn_ref[...], kx], axis=0),
                  jnp.concatenate([v0, v1, vn_ref[...], vx], axis=0), window(True, j < nstep - 1))]
        _attn_items(items, sink_ref, kv, o_ref)

    if not ctx_out:
        latent()
        return
    pl.when(j < nstep)(latent)

    @pl.when(j >= nstep)
    def _():
        _attn_items([((0, 2 * blk), q_ref[...], kx_ref[...], vx_ref[...], None)], sink_ref, kv, o_ref)


def _attention(q, kv, sink, geom, ctx_out):
    nb, nct_rows, seq = geom
    ctx_len = nct_rows // nb
    pair = 2 * SW_BLOCK
    assert ctx_len == pair and seq % pair == 0
    nblk = seq // SW_BLOCK
    nstep = seq // pair
    lat0 = nct_rows // SW_BLOCK
    w = SW_GROUP * SW_DH
    smem = pl.BlockSpec(memory_space=pltpu.SMEM)

    def qrow(b, j):
        return jnp.where(j < nstep, nct_rows // pair + b * nstep + j, b)

    def edge(off, colblock):
        return pl.BlockSpec((SW_BLOCK, w), lambda b, kv, j: (
            lat0 + b * nblk + jnp.clip(2 * jnp.minimum(j, nstep - 1) + off, 0, nblk - 1), colblock * SW_KV + kv))

    def center(colblock):
        return pl.BlockSpec((pair, w), lambda b, kv, j: (qrow(b, jnp.minimum(j, nstep - 1)), colblock * SW_KV + kv))

    def ctxkv(colblock):
        return pl.BlockSpec((ctx_len, w), lambda b, kv, j: (b, colblock * SW_KV + kv))

    if ctx_out:
        steps, out_rows = nstep + 1, q.shape[0]
        out_spec = pl.BlockSpec((pair, w), lambda b, kv, j: (qrow(b, j), kv))
    else:
        steps, out_rows = nstep, nb * seq
        out_spec = pl.BlockSpec((pair, w), lambda b, kv, j: (b * nstep + j, kv))
    return pl.pallas_call(
        functools.partial(_attn_kernel, nstep=nstep, ctx_out=ctx_out),
        out_shape=jax.ShapeDtypeStruct((out_rows, D), jnp.bfloat16),
        grid=(nb, SW_KV, steps),
        in_specs=[smem, pl.BlockSpec((pair, w), lambda b, kv, j: (qrow(b, j), kv)),
                  edge(-1, 0), center(0), edge(2, 0), ctxkv(0),
                  edge(-1, 1), center(1), edge(2, 1), ctxkv(1)],
        out_specs=out_spec,
        name="attention",
        compiler_params=_cparams(("arbitrary", "arbitrary", "arbitrary")),
    )(sink, q, kv, kv, kv, kv, kv, kv, kv, kv)


def _router_kernel(x_ref, sh_ref, sc_ref, r_ref, h_ref, info_ref):
    h = _modulate(x_ref[...], sh_ref[...], sc_ref[...])
    h_ref[...] = h
    logits = jnp.dot(h, r_ref[...], precision=lax.Precision.HIGHEST, preferred_element_type=jnp.float32)
    lane = lax.broadcasted_iota(jnp.int32, logits.shape, 1).astype(jnp.float32)
    logits = jnp.where(lane < N_EXPERTS, logits, -jnp.inf)
    l1 = jnp.max(logits, axis=-1, keepdims=True)
    i1 = jnp.min(jnp.where(logits == l1, lane, float(LANES)), axis=-1, keepdims=True)
    rest = jnp.where(lane == i1, -jnp.inf, logits)
    l2 = jnp.max(rest, axis=-1, keepdims=True)
    i2 = jnp.min(jnp.where(rest == l2, lane, float(LANES)), axis=-1, keepdims=True)
    w1 = 1.0 / (1.0 + jnp.exp(l2 - l1))
    info_ref[...] = jnp.where(lane == 0.0, i1, jnp.where(lane == 1.0, i2, jnp.where(lane == 2.0, w1, 1.0 - w1)))


def _router(x, mods, router_pad, *, geom, tm):
    nb, nct_rows, seq = geom
    t = x.shape[0]
    nct, lt = nct_rows // tm, seq // tm
    return pl.pallas_call(
        _router_kernel,
        out_shape=(jax.ShapeDtypeStruct((t, D), jnp.float32), jax.ShapeDtypeStruct((t, LANES), jnp.float32)),
        grid=(t // tm,),
        in_specs=[pl.BlockSpec((tm, D), lambda i: (i, 0)), _mod_spec(3, nct, lt, nb), _mod_spec(4, nct, lt, nb),
                  pl.BlockSpec((D, LANES), lambda i: (0, 0))],
        out_specs=(pl.BlockSpec((tm, D), lambda i: (i, 0)), pl.BlockSpec((tm, LANES), lambda i: (i, 0))),
        name="moe_router",
        compiler_params=_cparams(("arbitrary",)),
    )(x, mods, mods, router_pad)


FFN_SUB = 512


def _swiglu(h, wgu_ref, wd_ref):
    fdim = wd_ref.shape[0]
    bounds = list(range(0, fdim, FFN_SUB)) + [fdim]
    blocks = list(zip(bounds[:-1], bounds[1:]))
    up = lambda lo, hi: (jnp.dot(h, wgu_ref[:, lo:hi], preferred_element_type=jnp.float32),
                         jnp.dot(h, wgu_ref[:, fdim + lo:fdim + hi], preferred_element_type=jnp.float32))
    down = lambda gu, lo, hi: jnp.dot((gu[0] * _sigmoid(gu[0]) * gu[1]).astype(jnp.bfloat16), wd_ref[lo:hi, :],
                                      preferred_element_type=jnp.float32)
    gu = up(*blocks[0])
    acc = None
    for prev, cur in zip(blocks[:-1], blocks[1:]):
        gu_next = up(*cur)
        y = down(gu, *prev)
        acc = y if acc is None else acc + y
        gu = gu_next
    y = down(gu, *blocks[-1])
    return y if acc is None else acc + y


def _ffn_kernel(x_ref, sh_ref, sc_ref, gate_ref, wgu_ref, wd_ref, o_ref):
    x = x_ref[...]
    h = _modulate(x, sh_ref[...], sc_ref[...]).astype(jnp.bfloat16)
    o_ref[...] = x + gate_ref[...] * _swiglu(h, wgu_ref, wd_ref)


def _ffn(x, mods, wgu, wd, layer, *, geom, tm):
    nb, nct_rows, seq = geom
    t = x.shape[0]
    nct, lt = nct_rows // tm, seq // tm
    once = pl.Buffered(1)
    return pl.pallas_call(
        _ffn_kernel,
        out_shape=jax.ShapeDtypeStruct((t, D), jnp.float32),
        grid=(t // tm,),
        in_specs=[pl.BlockSpec((tm, D), lambda i: (i, 0)),
                  _mod_spec(3, nct, lt, nb),
                  _mod_spec(4, nct, lt, nb),
                  _mod_spec(5, nct, lt, nb),
                  pl.BlockSpec((None,) + wgu.shape[1:], lambda i: (layer, 0, 0), pipeline_mode=once),
                  pl.BlockSpec((None,) + wd.shape[1:], lambda i: (layer, 0, 0), pipeline_mode=once)],
        out_specs=pl.BlockSpec((tm, D), lambda i: (i, 0)),
        name="dense_ffn",
        compiler_params=_cparams(("arbitrary",)),
    )(x, mods, mods, mods, wgu, wd)


def _moe_kernel(te_ref, nt_ref, rows0_ref, rows1_ref, rows2_ref, outrows_ref, h_hbm, wgu_ref, wd_ref, y_hbm,
                hbuf, hb_ref, obuf, gsem, ssem, *, tm):
    del te_ref
    i = pl.program_id(0)
    nt = nt_ref[0]
    g_slot, o_slot = i % 3, i % 2

    def gather(rows_ref, s):
        for r in range(tm):
            pltpu.make_async_copy(h_hbm.at[pl.ds(rows_ref[0, r], 1), :], hbuf.at[s, pl.ds(r, 1), :],
                                  gsem.at[s]).start()

    def scatter_prev():
        for r in range(tm):
            pltpu.make_async_copy(obuf.at[1 - o_slot, pl.ds(r, 1), :], y_hbm.at[pl.ds(outrows_ref[0, r], 1), :],
                                  ssem.at[1 - o_slot]).start()

    @pl.when(i == 0)
    def _():
        gather(rows0_ref, 0)
        gather(rows1_ref, 1)
        obuf[1] = jnp.zeros((tm, D), jnp.float32)

    @pl.when((i >= 1) & (i - 2 < nt))
    def _():
        pltpu.make_async_copy(obuf.at[o_slot], obuf.at[o_slot], ssem.at[o_slot]).wait()

    @pl.when(i < nt + 2)
    def _():
        pltpu.make_async_copy(hbuf.at[g_slot], hbuf.at[g_slot], gsem.at[g_slot]).wait()
        hb_ref[...] = hbuf[g_slot].astype(jnp.bfloat16)

    @pl.when(i < nt)
    def _():
        gather(rows2_ref, (i + 2) % 3)
        scatter_prev()
        obuf[o_slot] = _swiglu(hb_ref[...], wgu_ref, wd_ref)

    @pl.when(i == nt)
    def _():
        scatter_prev()


def _moe_experts(h, tile_expert, n_tiles, rows, outrows, wgu, wd, layer, n_out, *, tm):
    nt = rows.shape[0]
    fdim = wd.shape[2]
    assert nt >= 3
    smem_rows = lambda off: pl.BlockSpec((None, 1, tm), lambda i, te, n: (jnp.minimum(i + off, nt - 1), 0, 0),
                                         memory_space=pltpu.SMEM)
    once = pl.Buffered(1)
    grid_spec = pltpu.PrefetchScalarGridSpec(
        num_scalar_prefetch=2,
        grid=(nt,),
        in_specs=[smem_rows(0), smem_rows(1), smem_rows(2),
                  pl.BlockSpec((None, 1, tm), lambda i, te, n: (i, 0, 0), memory_space=pltpu.SMEM),
                  pl.BlockSpec(memory_space=pl.ANY),
                  pl.BlockSpec((None, None, D, 2 * fdim), lambda i, te, n: (layer, te[i], 0, 0), pipeline_mode=once),
                  pl.BlockSpec((None, None, fdim, D), lambda i, te, n: (layer, te[i], 0, 0), pipeline_mode=once)],
        out_specs=pl.BlockSpec(memory_space=pl.ANY),
        scratch_shapes=[pltpu.VMEM((3, tm, D), jnp.float32), pltpu.VMEM((tm, D), jnp.bfloat16),
                        pltpu.VMEM((2, tm, D), jnp.float32), pltpu.SemaphoreType.DMA((3,)),
                        pltpu.SemaphoreType.DMA((2,))])
    return pl.pallas_call(
        functools.partial(_moe_kernel, tm=tm),
        out_shape=jax.ShapeDtypeStruct((n_out, D), jnp.float32),
        grid_spec=grid_spec,
        name="moe_experts",
        compiler_params=_cparams(("arbitrary",)),
    )(tile_expert, n_tiles, rows, rows, rows, outrows, h, wgu, wd)


def _combine_kernel(y0_ref, y1_ref, x_ref, gate_ref, info_ref, o_ref):
    info = info_ref[...]
    y = info[:, 2:3] * y0_ref[...] + info[:, 3:4] * y1_ref[...]
    o_ref[...] = x_ref[...] + gate_ref[...] * y


def _moe_combine(x, y2, info, mods, *, geom, tmc):
    nb, nct_rows, seq = geom
    t = x.shape[0]
    nct, lt = nct_rows // tmc, seq // tmc
    return pl.pallas_call(
        _combine_kernel,
        out_shape=jax.ShapeDtypeStruct((t, D), jnp.float32),
        grid=(t // tmc,),
        in_specs=[pl.BlockSpec((None, tmc, D), lambda i: (0, i, 0)),
                  pl.BlockSpec((None, tmc, D), lambda i: (1, i, 0)),
                  pl.BlockSpec((tmc, D), lambda i: (i, 0)),
                  _mod_spec(5, nct, lt, nb),
                  pl.BlockSpec((tmc, LANES), lambda i: (i, 0))],
        out_specs=pl.BlockSpec((tmc, D), lambda i: (i, 0)),
        name="moe_combine",
        compiler_params=_cparams(("arbitrary",)),
    )(y2, y2, x, mods, info)


def _moe(x, mods, router_pad, wgu, wd, layer, *, geom, tm, tmc):
    tr = x.shape[0]
    h, info = _router(x, mods, router_pad, geom=geom, tm=tm)
    e = info[:, 0:2].astype(jnp.int32).reshape(-1)
    onehot = (e[:, None] == jnp.arange(N_EXPERTS, dtype=jnp.int32)[None, :]).astype(jnp.int32)
    csum = jnp.cumsum(onehot, axis=0)
    rank = jnp.sum(onehot * (csum - 1), axis=1)
    counts = csum[-1]
    padded = ((counts + tm - 1) // tm) * tm
    ends = jnp.cumsum(padded)
    dest = (ends - padded)[e] + rank
    nt = 2 * tr // tm + N_EXPERTS + 2
    tile_expert = jnp.minimum(jnp.sum(jnp.arange(nt, dtype=jnp.int32)[:, None] * tm >= ends[None, :], axis=1),
                              N_EXPERTS - 1).astype(jnp.int32)
    n_tiles = (ends[-1:] // tm).astype(jnp.int32)
    slot_of_row = jnp.full((nt * tm,), -1, jnp.int32).at[dest].set(jnp.arange(2 * tr, dtype=jnp.int32),
                                                                   unique_indices=True)
    tp = tr + tm // 2
    j = jnp.arange(nt * tm, dtype=jnp.int32) % tm
    spare = jnp.where(j < tm // 2, tr + j, tp + tr + j - tm // 2)
    out_row = (slot_of_row % 2) * tp + slot_of_row // 2
    rows = (jnp.maximum(slot_of_row, 0) // 2).reshape(nt, 1, tm)
    outrows = jnp.concatenate([spare[:tm], jnp.where(slot_of_row >= 0, out_row, spare)]).reshape(nt + 1, 1, tm)
    y = _moe_experts(h, tile_expert, n_tiles, rows, outrows, wgu, wd, layer, 2 * tp, tm=tm)
    return _moe_combine(x, y.reshape(2, tp, D), info, mods, geom=geom, tmc=tmc)


def kernel(x, c, ctx, c_ctx, w_mod, b_mod, hg_w_in, hg_lb_logits, hg_norm_w, hg_w_out, sw_w_qkv, sw_q_norm,
           sw_k_norm, sw_sink, sw_w_out, ff_w_gate_up, ff_w_down, moe_router, moe_w_gate_up, moe_w_down):
    nb, seq, _ = x.shape
    ctx_len = ctx.shape[1]
    depth = w_mod.shape[0]
    nct_rows = nb * ctx_len
    geom = (nb, nct_rows, seq)
    tm = 512
    tmb = 1024 if nct_rows % 1024 == 0 and seq % 1024 == 0 else tm
    bf = jnp.bfloat16

    xs = jnp.concatenate([ctx.reshape(nct_rows, D), x.reshape(nb * seq, D)], axis=0)
    cpad = jnp.concatenate([c, c_ctx[None, :], jnp.zeros((8 - nb - 1, D), jnp.float32)], axis=0)
    mods_all = _modvecs(cpad, w_mod, b_mod)

    p_lb = jax.nn.softmax(hg_lb_logits.astype(jnp.float32), axis=0)
    lower_bounds = jnp.cumsum(p_lb, axis=0) - p_lb[:1]
    hg_w_in_bf, ff_wgu_bf, ff_wd_bf = hg_w_in.astype(bf), ff_w_gate_up.astype(bf), ff_w_down.astype(bf)
    moe_wgu_bf, moe_wd_bf = moe_w_gate_up.astype(bf), moe_w_down.astype(bf)

    for i in range(depth):
        ctx_live = i < depth - 1
        mods = mods_all[i]
        j = i // 2
        if i % 2 == 0:
            p = _mod_matmul(xs, mods, hg_w_in_bf, j, k_shift=0, geom=geom, tm=tm, tn=1024, out_dtype=jnp.float32)
            of, ob = _hg_scan(p, lower_bounds[j, 0:1], lower_bounds[j, 1:2], geom)
            xs = _hg_out(of, ob, p, hg_norm_w[j][None, :], hg_w_out[j].astype(bf), xs, mods,
                         geom=geom, tm=tmb, tn=1024)
        else:
            q, kv = _qkv_proj(xs, mods, sw_w_qkv[j], sw_q_norm[j], sw_k_norm[j], geom=geom, tm=tmb)
            o = _attention(q, kv, sw_sink[j], geom, ctx_live)
            xs = _res_matmul(o, sw_w_out[j].astype(bf), xs, mods, geom=geom, tm=tmb, tn=1024)
            if not ctx_live:
                geom = (nb, 0, seq)
        if i % 2 == 0:
            xs = _ffn(xs, mods, ff_wgu_bf, ff_wd_bf, j, geom=geom, tm=tm)
        else:
            rpad = jnp.pad(moe_router[j], ((0, 0), (0, LANES - N_EXPERTS)))
            xs = _moe(xs, mods, rpad, moe_wgu_bf, moe_wd_bf, j, geom=geom, tm=tm, tmc=tm)
    return xs[xs.shape[0] - nb * seq:].reshape(nb, seq, D)
```

```python
import functools

import numpy as np
import jax
import jax.numpy as jnp
from jax import lax
from jax.experimental import pallas as pl
from jax.experimental.pallas import tpu as pltpu

D = 1024
EPS = 1e-6
NEG_BIG = -1e30
LOG2E = 1.4426950408889634
GRID_W = 64
ROPE_THETA = 10000.0

HG_HEADS = 8
HG_DK = 128
HG_FDIM = HG_HEADS * HG_DK
HG_CHUNK = 128
HG_LEVELS = 7
HG_ROWS = 256
HG_HB = 8

SW_HEADS = 16
SW_KV = 4
SW_GROUP = 4
SW_DH = 64
SW_BLOCK = 128

N_EXPERTS = 8
LANES = 128

VMEM_LIMIT = 56 * 1024 * 1024


def _cparams(sem):
    return pltpu.CompilerParams(dimension_semantics=sem, vmem_limit_bytes=VMEM_LIMIT)


def _sigmoid(x):
    return 1.0 / (1.0 + jnp.exp2(x * -LOG2E))


def _mod_row(i, nct, lt, nb):
    return jnp.where(i < nct, nb, (i - nct) // lt)


def _mod_spec(k, nct, lt, nb):
    return pl.BlockSpec((None, 1, D), lambda i, *_: (_mod_row(i, nct, lt, nb) * 6 + k, 0, 0))


def _modulate(x, shift, scale):
    ms = jnp.mean(x * x, axis=-1, keepdims=True)
    return (x * lax.rsqrt(ms + EPS)) * (1.0 + scale) + shift


def _modvec_kernel(c_ref, w_ref, b_ref, o_ref):
    c = c_ref[...]
    s = c * _sigmoid(c)
    o_ref[...] = jnp.dot(s, w_ref[...], precision=lax.Precision.HIGHEST,
                         preferred_element_type=jnp.float32) + b_ref[...]


def _modvecs(cpad, w_mod, b_mod):
    depth = w_mod.shape[0]
    tn = 1024
    out = pl.pallas_call(
        _modvec_kernel,
        out_shape=jax.ShapeDtypeStruct((depth, 8, 6 * D), jnp.float32),
        grid=(depth, 6 * D // tn),
        in_specs=[pl.BlockSpec((8, D), lambda l, j: (0, 0)),
                  pl.BlockSpec((None, D, tn), lambda l, j: (l, 0, j)),
                  pl.BlockSpec((None, 1, tn), lambda l, j: (l, 0, j))],
        out_specs=pl.BlockSpec((None, 8, tn), lambda l, j: (l, 0, j)),
        name="adaln_vectors",
        compiler_params=_cparams(("arbitrary", "arbitrary")),
    )(cpad, w_mod, b_mod.reshape(depth, 1, 6 * D))
    return out.reshape(depth, 8 * 6, 1, D)


def _modmm_kernel(x_ref, sh_ref, sc_ref, w_ref, o_ref, *, tn):
    h = _modulate(x_ref[...], sh_ref[...], sc_ref[...]).astype(jnp.bfloat16)
    for n0 in range(0, w_ref.shape[1], tn):
        o_ref[:, n0:n0 + tn] = jnp.dot(h, w_ref[:, n0:n0 + tn], preferred_element_type=jnp.float32).astype(o_ref.dtype)


def _mod_matmul(x, mods, w, layer, *, k_shift, geom, tm, tn, out_dtype):
    nb, nct_rows, seq = geom
    t, n = x.shape[0], w.shape[2]
    nct, lt = nct_rows // tm, seq // tm
    return pl.pallas_call(
        functools.partial(_modmm_kernel, tn=tn),
        out_shape=jax.ShapeDtypeStruct((t, n), out_dtype),
        grid=(t // tm,),
        in_specs=[pl.BlockSpec((tm, D), lambda i: (i, 0)),
                  _mod_spec(k_shift, nct, lt, nb),
                  _mod_spec(k_shift + 1, nct, lt, nb),
                  pl.BlockSpec((None,) + w.shape[1:], lambda i: (layer, 0, 0), pipeline_mode=pl.Buffered(1))],
        out_specs=pl.BlockSpec((tm, n), lambda i: (i, 0)),
        name="hg_in_proj",
        compiler_params=_cparams(("arbitrary",)),
    )(x, mods, mods, w)


def _hg_tables(fwd):
    c = HG_CHUNK
    t = np.arange(c)[:, None]
    u = np.arange(c)[None, :]
    reach = (u <= t) if fwd else (u >= t)
    x = t ^ u
    lvl = np.where(x > 0, np.floor(np.log2(np.maximum(x, 1))), HG_LEVELS)
    lvl = np.where(reach, lvl, -1).astype(np.int32)
    tri = np.concatenate([reach.astype(np.float32)] * 3, axis=1)
    return tri, lvl


def _hg_gates(q_raw, z, lb, tri_ref, a_ref):
    bf = jnp.bfloat16
    q = q_raw * _sigmoid(q_raw)
    sig = _sigmoid(z)
    fc = jnp.maximum(lb + (1.0 - lb) * sig, 1e-30)
    lf = jnp.log(fc) * LOG2E
    k = (1.0 - lb) * (1.0 - sig)
    hi = lf.astype(bf)
    rest = lf - hi.astype(jnp.float32)
    mid = rest.astype(bf)
    lo = (rest - mid.astype(jnp.float32)).astype(bf)
    a = jnp.dot(tri_ref[...], jnp.concatenate([hi, mid, lo], axis=0), preferred_element_type=jnp.float32)
    a_ref[...] = a
    return q, k, fc, a


def _hg_scores(q, k, fc, a, a_ref, lvl, fwd):
    c = HG_CHUNK
    bf = jnp.bfloat16

    def rows_of(row, n):
        return jnp.broadcast_to(a_ref[row:row + 1, :], (n, HG_DK))

    rowi = lax.broadcasted_iota(jnp.int32, (c, HG_DK), 0)
    nt = (((1,), (1,)), ((), ()))
    zeros8 = jnp.zeros((8, HG_DK), jnp.float32)

    def level_operands(l):
        w = 1 << l
        near = w - 1 if fwd else w
        if w < 8:
            upper = ((rowi >> l) & 1) == 1
            q_side = upper if fwd else jnp.logical_not(upper)
            if l == 0:
                qe, ke = q * fc, k
            else:
                if l == 1:
                    first = jnp.concatenate([rows_of(8 * g + near, 8) for g in range(c // 8)], axis=0)
                    second = jnp.concatenate([rows_of(8 * g + 4 + near, 8) for g in range(c // 8)], axis=0)
                    bnd = jnp.where((rowi & 4) == 0, first, second)
                else:
                    bnd = jnp.concatenate([rows_of(8 * g + near, 8) for g in range(c // 8)], axis=0)
                e = jnp.exp2(-jnp.abs(a - bnd))
                qe, ke = q * e, k * e
            return jnp.where(q_side, qe, 0.0).astype(bf), jnp.where(q_side, 0.0, ke).astype(bf)
        qp, kp = [], []
        for j in range(c // w):
            rs = slice(j * w, (j + 1) * w)
            bnd = rows_of((j // 2) * 2 * w + near, w)
            zero = jnp.concatenate([zeros8] * (w // 8), axis=0)
            if (j % 2 == 1) == fwd:
                qp.append(q[rs] * jnp.exp2(a[rs] - bnd))
                kp.append(zero)
            else:
                qp.append(zero)
                kp.append(k[rs] * jnp.exp2(bnd - a[rs]))
        return jnp.concatenate(qp, axis=0).astype(bf), jnp.concatenate(kp, axis=0).astype(bf)

    scores = jnp.where(lvl == HG_LEVELS,
                       lax.dot_general(q.astype(bf), k.astype(bf), nt, preferred_element_type=jnp.float32), 0.0)
    for l in range(HG_LEVELS):
        qm, km = level_operands(l)
        sc = lax.dot_general(qm, km, nt, preferred_element_type=jnp.float32)
        scores = jnp.where(lvl == l, sc, scores)
    return scores.astype(bf)


def _hg_finish(q, k, v, a, scores, st_ref, fwd):
    c = HG_CHUNK
    bf = jnp.bfloat16
    nt = (((1,), (1,)), ((), ()))
    last = c - 1 if fwd else 0
    e_in = jnp.exp2(a)
    e_out = jnp.exp2(a[last:last + 1, :] - a)
    st = st_ref[...]
    o = (jnp.dot(scores, v.astype(bf), preferred_element_type=jnp.float32)
         + lax.dot_general((q * e_in).astype(bf), st.astype(bf), nt, preferred_element_type=jnp.float32))
    st_ref[...] = (e_in[last:last + 1, :] * st
                   + jnp.dot(v.T.astype(bf), (k * e_out).astype(bf), preferred_element_type=jnp.float32))
    return o


def _hg_scan_kernel(qf_ref, vf_ref, zf_ref, qb_ref, vb_ref, zb_ref, lbf_ref, lbb_ref, trif_ref, trib_ref,
                    lvlf_ref, lvlb_ref, of_ref, ob_ref, sf_ref, sb_ref, a_ref):
    @pl.when(pl.program_id(2) == 0)
    def _():
        sf_ref[...] = jnp.zeros_like(sf_ref)
        sb_ref[...] = jnp.zeros_like(sb_ref)

    c = HG_CHUNK
    nch = HG_ROWS // c
    lvlf, lvlb = lvlf_ref[...], lvlb_ref[...]
    units = []
    for hh in range(HG_HB):
        cs = slice(hh * HG_DK, (hh + 1) * HG_DK)
        for step, ci in enumerate(range(nch)):
            units.append(dict(step=step, rows=slice(ci * c, (ci + 1) * c), cols=cs, fwd=True, q=qf_ref, v=vf_ref,
                              z=zf_ref, lb=lbf_ref, tri=trif_ref, lvl=lvlf, st=sf_ref.at[hh], out=of_ref))
        for step, ci in enumerate(reversed(range(nch))):
            units.append(dict(step=step, rows=slice(ci * c, (ci + 1) * c), cols=cs, fwd=False, q=qb_ref, v=vb_ref,
                              z=zb_ref, lb=lbb_ref, tri=trib_ref, lvl=lvlb, st=sb_ref.at[hh], out=ob_ref))
    for n, u in enumerate(units):
        u["a_ref"] = a_ref.at[n]
        u["q"], u["k"], u["fc"], u["a"] = _hg_gates(u["q"][u["rows"], u["cols"]], u["z"][u["rows"], u["cols"]],
                                                    u["lb"][:, u["cols"]], u["tri"], u["a_ref"])
    for u in units:
        u["scores"] = _hg_scores(u["q"], u["k"], u["fc"], u["a"], u["a_ref"], u["lvl"], u["fwd"])
    for step in range(nch):
        for u in units:
            if u["step"] == step:
                o = _hg_finish(u["q"], u["k"], u["v"][u["rows"], u["cols"]], u["a"], u["scores"], u["st"], u["fwd"])
                u["out"][u["rows"], u["cols"]] = o.astype(u["out"].dtype)


def _hg_scan(p, lbf, lbb, geom):
    nb, nct_rows, seq = geom
    t = p.shape[0]
    r = HG_ROWS
    cb, lb_ = (nct_rows // nb) // r, seq // r
    nsteps = cb + lb_
    lat0 = nct_rows // r

    def fblk(b, s):
        return jnp.where(s < cb, b * cb + s, lat0 + b * lb_ + (s - cb))

    def bblk(b, s):
        return jnp.where(s < cb, b * cb + (cb - 1 - s), lat0 + b * lb_ + (lb_ - 1 - (s - cb)))

    wcol = HG_HB * HG_DK
    ngrp = HG_HEADS // HG_HB
    units = 2 * HG_HB * (r // HG_CHUNK)

    def spec(blk, colblock):
        return pl.BlockSpec((r, wcol), lambda b, h, s: (blk(b, s), colblock * ngrp + h))

    lbspec = pl.BlockSpec((1, wcol), lambda b, h, s: (0, h))
    const = lambda shape: pl.BlockSpec(shape, lambda b, h, s: (0, 0))
    trif, lvlf = _hg_tables(True)
    trib, lvlb = _hg_tables(False)
    return pl.pallas_call(
        _hg_scan_kernel,
        out_shape=(jax.ShapeDtypeStruct((t, D), jnp.bfloat16), jax.ShapeDtypeStruct((t, D), jnp.bfloat16)),
        grid=(nb, ngrp, nsteps),
        in_specs=[spec(fblk, 0), spec(fblk, 1), spec(fblk, 2),
                  spec(bblk, 0), spec(bblk, 1), spec(bblk, 3),
                  lbspec, lbspec, const(trif.shape), const(trib.shape), const(lvlf.shape), const(lvlb.shape)],
        out_specs=(pl.BlockSpec((r, wcol), lambda b, h, s: (fblk(b, s), h)),
                   pl.BlockSpec((r, wcol), lambda b, h, s: (bblk(b, s), h))),
        scratch_shapes=[pltpu.VMEM((HG_HB, HG_DK, HG_DK), jnp.float32),
                        pltpu.VMEM((HG_HB, HG_DK, HG_DK), jnp.float32),
                        pltpu.VMEM((units, HG_CHUNK, HG_DK), jnp.float32)],
        name="hg_scan",
        compiler_params=_cparams(("arbitrary", "arbitrary", "arbitrary")),
    )(p, p, p, p, p, p, lbf, lbb, jnp.asarray(trif, jnp.bfloat16), jnp.asarray(trib, jnp.bfloat16),
      jnp.asarray(lvlf), jnp.asarray(lvlb))


def _hg_out_kernel(of_ref, ob_ref, g_ref, nw_ref, w_ref, x_ref, gate_ref, o_ref, h_ref):
    @pl.when(pl.program_id(1) == 0)
    def _():
        nw = nw_ref[...]
        for h in range(HG_HEADS):
            cs = slice(h * HG_DK, (h + 1) * HG_DK)
            o = of_ref[:, cs].astype(jnp.float32) + ob_ref[:, cs].astype(jnp.float32)
            y = o * lax.rsqrt(jnp.mean(o * o, axis=-1, keepdims=True) + EPS) * nw
            g = g_ref[:, cs]
            h_ref[:, cs] = (y * (g * _sigmoid(g))).astype(jnp.bfloat16)

    acc = jnp.dot(h_ref[...], w_ref[...], preferred_element_type=jnp.float32)
    o_ref[...] = x_ref[...] + gate_ref[...] * acc


def _hg_out(of, ob, p, norm_w, w, x, mods, *, geom, tm, tn):
    nb, nct_rows, seq = geom
    t = x.shape[0]
    nct, lt = nct_rows // tm, seq // tm
    gate = pl.BlockSpec((None, 1, tn), lambda i, j: (_mod_row(i, nct, lt, nb) * 6 + 2, 0, j))
    return pl.pallas_call(
        _hg_out_kernel,
        out_shape=jax.ShapeDtypeStruct((t, D), jnp.float32),
        grid=(t // tm, D // tn),
        in_specs=[pl.BlockSpec((tm, D), lambda i, j: (i, 0)),
                  pl.BlockSpec((tm, D), lambda i, j: (i, 0)),
                  pl.BlockSpec((tm, D), lambda i, j: (i, 4)),
                  pl.BlockSpec((1, HG_DK), lambda i, j: (0, 0)),
                  pl.BlockSpec((D, tn), lambda i, j: (0, j)),
                  pl.BlockSpec((tm, tn), lambda i, j: (i, j)),
                  gate],
        out_specs=pl.BlockSpec((tm, tn), lambda i, j: (i, j)),
        scratch_shapes=[pltpu.VMEM((tm, D), jnp.bfloat16)],
        name="hg_out_proj",
        compiler_params=_cparams(("arbitrary", "arbitrary")),
    )(of, ob, p, norm_w, w, x, mods)


def _resmm_kernel(a_ref, w_ref, x_ref, gate_ref, o_ref):
    acc = jnp.dot(a_ref[...], w_ref[...], preferred_element_type=jnp.float32)
    o_ref[...] = x_ref[...] + gate_ref[...] * acc


def _res_matmul(a, w, x, mods, *, geom, tm, tn):
    nb, nct_rows, seq = geom
    rows = a.shape[0]
    t0 = (x.shape[0] - rows) // tm
    nct, lt = nct_rows // tm, seq // tm
    gate = pl.BlockSpec((None, 1, tn), lambda i, j: (_mod_row(i + t0, nct, lt, nb) * 6 + 2, 0, j))
    return pl.pallas_call(
        _resmm_kernel,
        out_shape=jax.ShapeDtypeStruct((rows, D), jnp.float32),
        grid=(rows // tm, D // tn),
        in_specs=[pl.BlockSpec((tm, a.shape[1]), lambda i, j: (i, 0)),
                  pl.BlockSpec((a.shape[1], tn), lambda i, j: (0, j)),
                  pl.BlockSpec((tm, tn), lambda i, j: (i + t0, j)),
                  gate],
        out_specs=pl.BlockSpec((tm, tn), lambda i, j: (i, j)),
        name="attn_out_proj",
        compiler_params=_cparams(("arbitrary", "arbitrary")),
    )(a, w, x, mods)


def _rope_tables(seq, tm):
    rows = seq // GRID_W
    row = np.repeat(np.arange(rows, dtype=np.float32), GRID_W)
    col = np.tile(np.arange(GRID_W, dtype=np.float32), rows)
    nf = SW_DH // 4
    inv = (ROPE_THETA ** (-np.arange(nf, dtype=np.float32) / nf)).astype(np.float32)
    ang_r = row[:, None] * inv
    ang_c = col[:, None] * inv
    cos = np.concatenate([np.cos(ang_r), np.cos(ang_r), np.cos(ang_c), np.cos(ang_c)], axis=1)
    sin = np.concatenate([-np.sin(ang_r), np.sin(ang_r), -np.sin(ang_c), np.sin(ang_c)], axis=1)
    cos = np.concatenate([np.tile(cos, (1, 2)), np.ones((tm, LANES), np.float32)], axis=0)
    sin = np.concatenate([np.tile(sin, (1, 2)), np.zeros((tm, LANES), np.float32)], axis=0)
    return jnp.asarray(cos, jnp.float32), jnp.asarray(sin, jnp.float32)


def _norm_rope(y, ones_ref, nw, cos, sin, scale):
    ss = jnp.dot((y * y).astype(jnp.bfloat16), ones_ref[...], preferred_element_type=jnp.float32)
    y = y * lax.rsqrt(ss * (1.0 / SW_DH) + EPS) * nw
    first = (lax.broadcasted_iota(jnp.int32, (y.shape[0], LANES), 1) & 16) == 0
    pieces = []
    for g in range(y.shape[1] // LANES):
        yg = y[:, g * LANES:(g + 1) * LANES]
        partner = jnp.where(first, pltpu.roll(yg, LANES - 16, 1), pltpu.roll(yg, 16, 1))
        pieces.append((yg * cos + partner * sin) * scale)
    return pieces


def _q_kernel(x_ref, sh_ref, sc_ref, w_ref, nw_ref, ones_ref, cos_ref, sin_ref, o_ref):
    h = _modulate(x_ref[...], sh_ref[...], sc_ref[...]).astype(jnp.bfloat16)
    acc = jnp.dot(h, w_ref[...], preferred_element_type=jnp.float32)
    pieces = _norm_rope(acc, ones_ref, nw_ref[...], cos_ref[...], sin_ref[...], SW_DH ** -0.5 * LOG2E)
    for g, piece in enumerate(pieces):
        o_ref[:, g * LANES:(g + 1) * LANES] = piece.astype(o_ref.dtype)


def _kv_kernel(x_ref, sh_ref, sc_ref, w_ref, nw_ref, ones_ref, cos_ref, sin_ref, o_ref):
    h = _modulate(x_ref[...], sh_ref[...], sc_ref[...]).astype(jnp.bfloat16)
    acc = jnp.dot(h, w_ref[...], preferred_element_type=jnp.float32)
    nkv = SW_KV * SW_DH
    k_pieces = _norm_rope(acc[:, :nkv], ones_ref, nw_ref[...], cos_ref[...], sin_ref[...], 1.0)
    v_pieces = [acc[:, nkv + g * LANES:nkv + (g + 1) * LANES] for g in range(nkv // LANES)]
    low = lax.broadcasted_iota(jnp.int32, (acc.shape[0], LANES), 1) < SW_DH
    for base, pieces in ((0, k_pieces), (D, v_pieces)):
        for c, piece in enumerate(pieces):
            swapped = pltpu.roll(piece, SW_DH, 1)
            for half, rep in enumerate((jnp.where(low, piece, swapped), jnp.where(low, swapped, piece))):
                rep = rep.astype(o_ref.dtype)
                col = base + (2 * c + half) * SW_GROUP * SW_DH
                o_ref[:, col:col + LANES] = rep
                o_ref[:, col + LANES:col + 2 * LANES] = rep


def _qkv_proj(x, mods, w_qkv, q_norm, k_norm, *, geom, tm):
    nb, nct_rows, seq = geom
    t = x.shape[0]
    nct, lt = nct_rows // tm, seq // tm
    nkv = SW_KV * SW_DH
    cos, sin = _rope_tables(seq, tm)
    ones = lambda n: jnp.asarray(np.kron(np.eye(n // SW_DH, dtype=np.float32),
                                         np.ones((SW_DH, SW_DH), np.float32)), jnp.bfloat16)
    tab = lambda i: (jnp.where(i < nct, lt, (i - nct) % lt), 0)
    once = pl.Buffered(1)

    def call(body, w, nw, n_norm, n_out, name):
        return pl.pallas_call(
            body,
            out_shape=jax.ShapeDtypeStruct((t, n_out), jnp.bfloat16),
            grid=(t // tm,),
            in_specs=[pl.BlockSpec((tm, D), lambda i: (i, 0)),
                      _mod_spec(0, nct, lt, nb),
                      _mod_spec(1, nct, lt, nb),
                      pl.BlockSpec(w.shape, lambda i: (0, 0), pipeline_mode=once),
                      pl.BlockSpec((1, n_norm), lambda i: (0, 0)),
                      pl.BlockSpec((n_norm, n_norm), lambda i: (0, 0), pipeline_mode=once),
                      pl.BlockSpec((tm, LANES), tab),
                      pl.BlockSpec((tm, LANES), tab)],
            out_specs=pl.BlockSpec((tm, n_out), lambda i: (i, 0)),
            name=name,
            compiler_params=_cparams(("arbitrary",)),
        )(x, mods, mods, w, nw, ones(n_norm), cos, sin)

    bf = jnp.bfloat16
    q = call(_q_kernel, w_qkv[:, :D].astype(bf), jnp.tile(q_norm, SW_HEADS)[None, :], D, D, "q_proj")
    kv = call(_kv_kernel, w_qkv[:, D:].astype(bf), jnp.tile(k_norm, SW_KV)[None, :], nkv, 2 * D, "kv_proj")
    return q, kv


def _attn_items(items, sink_ref, kv, o_ref):
    nt = (((1,), (1,)), ((), ()))
    work = [(it, g) for it in items for g in range(SW_GROUP)]
    scores = []
    for (_, q, kcat, _, _), g in work:
        grp = lax.broadcasted_iota(jnp.int32, q.shape, 1) // SW_DH
        scores.append(lax.dot_general(jnp.where(grp == g, q, jnp.zeros_like(q)), kcat, nt,
                                      preferred_element_type=jnp.float32))
    probs, denoms = [], []
    for ((_, _, _, _, valid), g), s in zip(work, scores):
        sink = sink_ref[kv * SW_GROUP + g] * LOG2E
        if valid is not None:
            s = jnp.where(valid, s, NEG_BIG)
        m = jnp.maximum(jnp.max(s, axis=-1, keepdims=True), sink)
        p = jnp.exp2(s - m)
        denoms.append(jnp.sum(p, axis=-1, keepdims=True) + jnp.exp2(sink - m))
        probs.append(p.astype(jnp.bfloat16))
    outs = {}
    for n, ((rows, q, _, vcat, _), g) in enumerate(work):
        grp = lax.broadcasted_iota(jnp.int32, q.shape, 1) // SW_DH
        og = jnp.dot(probs[n], vcat, preferred_element_type=jnp.float32) / denoms[n]
        outs[rows] = og if g == 0 else jnp.where(grp == g, og, outs[rows])
    for rows, out in outs.items():
        o_ref[rows[0]:rows[1], :] = out.astype(o_ref.dtype)


def _attn_kernel(sink_ref, q_ref, kp_ref, kc_ref, kn_ref, kx_ref, vp_ref, vc_ref, vn_ref, vx_ref, o_ref,
                 *, nstep, ctx_out):
    kv, j = pl.program_id(1), pl.program_id(2)
    blk = SW_BLOCK

    def latent():
        kx, vx = kx_ref[...], vx_ref[...]
        k0, k1, v0, v1 = kc_ref[0:blk, :], kc_ref[blk:2 * blk, :], vc_ref[0:blk, :], vc_ref[blk:2 * blk, :]
        ns = 3 * blk + kx.shape[0]
        t = lax.broadcasted_iota(jnp.int32, (blk, ns), 0)
        s = lax.broadcasted_iota(jnp.int32, (blk, ns), 1)

        def window(has_prev, has_next):
            lo = jnp.maximum(t, jnp.where(has_prev, 0, blk))
            hi = jnp.minimum(t + 2 * blk, jnp.where(has_next, 3 * blk - 1, 2 * blk - 1))
            return ((s >= lo) & (s <= hi)) | (s >= 3 * blk)

        items = [((0, blk), q_ref[0:blk, :], jnp.concatenate([kp_ref[...], k0, k1, kx], axis=0),
                  jnp.concatenate([vp_ref[...], v0, v1, vx], axis=0), window(j > 0, True)),
                 ((blk, 2 * blk), q_ref[blk:2 * blk, :], jnp.concatenate([k0, k1, kn_ref[...], kx], axis=0),
                  jnp.concatenate([v0, v1, vn_ref[...], vx], axis=0), window(True, j < nstep - 1))]
        _attn_items(items, sink_ref, kv, o_ref)

    if not ctx_out:
        latent()
        return
    pl.when(j < nstep)(latent)

    @pl.when(j >= nstep)
    def _():
        _attn_items([((0, 2 * blk), q_ref[...], kx_ref[...], vx_ref[...], None)], sink_ref, kv, o_ref)


def _attention(q, kv, sink, geom, ctx_out):
    nb, nct_rows, seq = geom
    ctx_len = nct_rows // nb
    pair = 2 * SW_BLOCK
    assert ctx_len == pair and seq % pair == 0
    nblk = seq // SW_BLOCK
    nstep = seq // pair
    lat0 = nct_rows // SW_BLOCK
    w = SW_GROUP * SW_DH
    smem = pl.BlockSpec(memory_space=pltpu.SMEM)

    def qrow(b, j):
        return jnp.where(j < nstep, nct_rows // pair + b * nstep + j, b)

    def edge(off, colblock):
        return pl.BlockSpec((SW_BLOCK, w), lambda b, kv, j: (
            lat0 + b * nblk + jnp.clip(2 * jnp.minimum(j, nstep - 1) + off, 0, nblk - 1), colblock * SW_KV + kv))

    def center(colblock):
        return pl.BlockSpec((pair, w), lambda b, kv, j: (qrow(b, jnp.minimum(j, nstep - 1)), colblock * SW_KV + kv))

    def ctxkv(colblock):
        return pl.BlockSpec((ctx_len, w), lambda b, kv, j: (b, colblock * SW_KV + kv))

    if ctx_out:
        steps, out_rows = nstep + 1, q.shape[0]
        out_spec = pl.BlockSpec((pair, w), lambda b, kv, j: (qrow(b, j), kv))
    else:
        steps, out_rows = nstep, nb * seq
        out_spec = pl.BlockSpec((pair, w), lambda b, kv, j: (b * nstep + j, kv))
    return pl.pallas_call(
        functools.partial(_attn_kernel, nstep=nstep, ctx_out=ctx_out),
        out_shape=jax.ShapeDtypeStruct((out_rows, D), jnp.bfloat16),
        grid=(nb, SW_KV, steps),
        in_specs=[smem, pl.BlockSpec((pair, w), lambda b, kv, j: (qrow(b, j), kv)),
                  edge(-1, 0), center(0), edge(2, 0), ctxkv(0),
                  edge(-1, 1), center(1), edge(2, 1), ctxkv(1)],
        out_specs=out_spec,
        name="attention",
        compiler_params=_cparams(("arbitrary", "arbitrary", "arbitrary")),
    )(sink, q, kv, kv, kv, kv, kv, kv, kv, kv)


def _router_kernel(x_ref, sh_ref, sc_ref, r_ref, h_ref, info_ref):
    h = _modulate(x_ref[...], sh_ref[...], sc_ref[...])
    h_ref[...] = h
    logits = jnp.dot(h, r_ref[...], precision=lax.Precision.HIGHEST, preferred_element_type=jnp.float32)
    lane = lax.broadcasted_iota(jnp.int32, logits.shape, 1).astype(jnp.float32)
    logits = jnp.where(lane < N_EXPERTS, logits, -jnp.inf)
    l1 = jnp.max(logits, axis=-1, keepdims=True)
    i1 = jnp.min(jnp.where(logits == l1, lane, float(LANES)), axis=-1, keepdims=True)
    rest = jnp.where(lane == i1, -jnp.inf, logits)
    l2 = jnp.max(rest, axis=-1, keepdims=True)
    i2 = jnp.min(jnp.where(rest == l2, lane, float(LANES)), axis=-1, keepdims=True)
    w1 = 1.0 / (1.0 + jnp.exp(l2 - l1))
    info_ref[...] = jnp.where(lane == 0.0, i1, jnp.where(lane == 1.0, i2, jnp.where(lane == 2.0, w1, 1.0 - w1)))


def _router(x, mods, router_pad, *, geom, tm):
    nb, nct_rows, seq = geom
    t = x.shape[0]
    nct, lt = nct_rows // tm, seq // tm
    return pl.pallas_call(
        _router_kernel,
        out_shape=(jax.ShapeDtypeStruct((t, D), jnp.float32), jax.ShapeDtypeStruct((t, LANES), jnp.float32)),
        grid=(t // tm,),
        in_specs=[pl.BlockSpec((tm, D), lambda i: (i, 0)), _mod_spec(3, nct, lt, nb), _mod_spec(4, nct, lt, nb),
                  pl.BlockSpec((D, LANES), lambda i: (0, 0))],
        out_specs=(pl.BlockSpec((tm, D), lambda i: (i, 0)), pl.BlockSpec((tm, LANES), lambda i: (i, 0))),
        name="moe_router",
        compiler_params=_cparams(("arbitrary",)),
    )(x, mods, mods, router_pad)


FFN_SUB = 512


def _swiglu(h, wgu_ref, wd_ref):
    fdim = wd_ref.shape[0]
    bounds = list(range(0, fdim, FFN_SUB)) + [fdim]
    blocks = list(zip(bounds[:-1], bounds[1:]))
    up = lambda lo, hi: (jnp.dot(h, wgu_ref[:, lo:hi], preferred_element_type=jnp.float32),
                         jnp.dot(h, wgu_ref[:, fdim + lo:fdim + hi], preferred_element_type=jnp.float32))
    down = lambda gu, lo, hi: jnp.dot((gu[0] * _sigmoid(gu[0]) * gu[1]).astype(jnp.bfloat16), wd_ref[lo:hi, :],
                                      preferred_element_type=jnp.float32)
    gu = up(*blocks[0])
    acc = None
    for prev, cur in zip(blocks[:-1], blocks[1:]):
        gu_next = up(*cur)
        y = down(gu, *prev)
        acc = y if acc is None else acc + y
        gu = gu_next
    y = down(gu, *blocks[-1])
    return y if acc is None else acc + y


def _ffn_kernel(x_ref, sh_ref, sc_ref, gate_ref, wgu_ref, wd_ref, o_ref):
    x = x_ref[...]
    h = _modulate(x, sh_ref[...], sc_ref[...]).astype(jnp.bfloat16)
    o_ref[...] = x + gate_ref[...] * _swiglu(h, wgu_ref, wd_ref)


def _ffn(x, mods, wgu, wd, layer, *, geom, tm):
    nb, nct_rows, seq = geom
    t = x.shape[0]
    nct, lt = nct_rows // tm, seq // tm
    once = pl.Buffered(1)
    return pl.pallas_call(
        _ffn_kernel,
        out_shape=jax.ShapeDtypeStruct((t, D), jnp.float32),
        grid=(t // tm,),
        in_specs=[pl.BlockSpec((tm, D), lambda i: (i, 0)),
                  _mod_spec(3, nct, lt, nb),
                  _mod_spec(4, nct, lt, nb),
                  _mod_spec(5, nct, lt, nb),
                  pl.BlockSpec((None,) + wgu.shape[1:], lambda i: (layer, 0, 0), pipeline_mode=once),
                  pl.BlockSpec((None,) + wd.shape[1:], lambda i: (layer, 0, 0), pipeline_mode=once)],
        out_specs=pl.BlockSpec((tm, D), lambda i: (i, 0)),
        name="dense_ffn",
        compiler_params=_cparams(("arbitrary",)),
    )(x, mods, mods, mods, wgu, wd)


def _moe_kernel(te_ref, nt_ref, rows0_ref, rows1_ref, rows2_ref, outrows_ref, h_hbm, wgu_ref, wd_ref, y_hbm,
                hbuf, hb_ref, obuf, gsem, ssem, *, tm):
    del te_ref
    i = pl.program_id(0)
    nt = nt_ref[0]
    g_slot, o_slot = i % 3, i % 2

    def gather(rows_ref, s):
        for r in range(tm):
            pltpu.make_async_copy(h_hbm.at[pl.ds(rows_ref[0, r], 1), :], hbuf.at[s, pl.ds(r, 1), :],
                                  gsem.at[s]).start()

    def scatter_prev():
        for r in range(tm):
            pltpu.make_async_copy(obuf.at[1 - o_slot, pl.ds(r, 1), :], y_hbm.at[pl.ds(outrows_ref[0, r], 1), :],
                                  ssem.at[1 - o_slot]).start()

    @pl.when(i == 0)
    def _():
        gather(rows0_ref, 0)
        gather(rows1_ref, 1)
        obuf[1] = jnp.zeros((tm, D), jnp.float32)

    @pl.when((i >= 1) & (i - 2 < nt))
    def _():
        pltpu.make_async_copy(obuf.at[o_slot], obuf.at[o_slot], ssem.at[o_slot]).wait()

    @pl.when(i < nt + 2)
    def _():
        pltpu.make_async_copy(hbuf.at[g_slot], hbuf.at[g_slot], gsem.at[g_slot]).wait()
        hb_ref[...] = hbuf[g_slot].astype(jnp.bfloat16)

    @pl.when(i < nt)
    def _():
        gather(rows2_ref, (i + 2) % 3)
        scatter_prev()
        obuf[o_slot] = _swiglu(hb_ref[...], wgu_ref, wd_ref)

    @pl.when(i == nt)
    def _():
        scatter_prev()


def _moe_experts(h, tile_expert, n_tiles, rows, outrows, wgu, wd, layer, n_out, *, tm):
    nt = rows.shape[0]
    fdim = wd.shape[2]
    assert nt >= 3
    smem_rows = lambda off: pl.BlockSpec((None, 1, tm), lambda i, te, n: (jnp.minimum(i + off, nt - 1), 0, 0),
                                         memory_space=pltpu.SMEM)
    once = pl.Buffered(1)
    grid_spec = pltpu.PrefetchScalarGridSpec(
        num_scalar_prefetch=2,
        grid=(nt,),
        in_specs=[smem_rows(0), smem_rows(1), smem_rows(2),
                  pl.BlockSpec((None, 1, tm), lambda i, te, n: (i, 0, 0), memory_space=pltpu.SMEM),
                  pl.BlockSpec(memory_space=pl.ANY),
                  pl.BlockSpec((None, None, D, 2 * fdim), lambda i, te, n: (layer, te[i], 0, 0), pipeline_mode=once),
                  pl.BlockSpec((None, None, fdim, D), lambda i, te, n: (layer, te[i], 0, 0), pipeline_mode=once)],
        out_specs=pl.BlockSpec(memory_space=pl.ANY),
        scratch_shapes=[pltpu.VMEM((3, tm, D), jnp.float32), pltpu.VMEM((tm, D), jnp.bfloat16),
                        pltpu.VMEM((2, tm, D), jnp.float32), pltpu.SemaphoreType.DMA((3,)),
                        pltpu.SemaphoreType.DMA((2,))])
    return pl.pallas_call(
        functools.partial(_moe_kernel, tm=tm),
        out_shape=jax.ShapeDtypeStruct((n_out, D), jnp.float32),
        grid_spec=grid_spec,
        name="moe_experts",
        compiler_params=_cparams(("arbitrary",)),
    )(tile_expert, n_tiles, rows, rows, rows, outrows, h, wgu, wd)


def _combine_kernel(y0_ref, y1_ref, x_ref, gate_ref, info_ref, o_ref):
    info = info_ref[...]
    y = info[:, 2:3] * y0_ref[...] + info[:, 3:4] * y1_ref[...]
    o_ref[...] = x_ref[...] + gate_ref[...] * y


def _moe_combine(x, y2, info, mods, *, geom, tmc):
    nb, nct_rows, seq = geom
    t = x.shape[0]
    nct, lt = nct_rows // tmc, seq // tmc
    return pl.pallas_call(
        _combine_kernel,
        out_shape=jax.ShapeDtypeStruct((t, D), jnp.float32),
        grid=(t // tmc,),
        in_specs=[pl.BlockSpec((None, tmc, D), lambda i: (0, i, 0)),
                  pl.BlockSpec((None, tmc, D), lambda i: (1, i, 0)),
                  pl.BlockSpec((tmc, D), lambda i: (i, 0)),
                  _mod_spec(5, nct, lt, nb),
                  pl.BlockSpec((tmc, LANES), lambda i: (i, 0))],
        out_specs=pl.BlockSpec((tmc, D), lambda i: (i, 0)),
        name="moe_combine",
        compiler_params=_cparams(("arbitrary",)),
    )(y2, y2, x, mods, info)


def _moe(x, mods, router_pad, wgu, wd, layer, *, geom, tm, tmc):
    tr = x.shape[0]
    h, info = _router(x, mods, router_pad, geom=geom, tm=tm)
    e = info[:, 0:2].astype(jnp.int32).reshape(-1)
    onehot = (e[:, None] == jnp.arange(N_EXPERTS, dtype=jnp.int32)[None, :]).astype(jnp.int32)
    csum = jnp.cumsum(onehot, axis=0)
    rank = jnp.sum(onehot * (csum - 1), axis=1)
    counts = csum[-1]
    padded = ((counts + tm - 1) // tm) * tm
    ends = jnp.cumsum(padded)
    dest = (ends - padded)[e] + rank
    nt = 2 * tr // tm + N_EXPERTS + 2
    tile_expert = jnp.minimum(jnp.sum(jnp.arange(nt, dtype=jnp.int32)[:, None] * tm >= ends[None, :], axis=1),
                              N_EXPERTS - 1).astype(jnp.int32)
    n_tiles = (ends[-1:] // tm).astype(jnp.int32)
    slot_of_row = jnp.full((nt * tm,), -1, jnp.int32).at[dest].set(jnp.arange(2 * tr, dtype=jnp.int32),
                                                                   unique_indices=True)
    tp = tr + tm // 2
    j = jnp.arange(nt * tm, dtype=jnp.int32) % tm
    spare = jnp.where(j < tm // 2, tr + j, tp + tr + j - tm // 2)
    out_row = (slot_of_row % 2) * tp + slot_of_row // 2
    rows = (jnp.maximum(slot_of_row, 0) // 2).reshape(nt, 1, tm)
    outrows = jnp.concatenate([spare[:tm], jnp.where(slot_of_row >= 0, out_row, spare)]).reshape(nt + 1, 1, tm)
    y = _moe_experts(h, tile_expert, n_tiles, rows, outrows, wgu, wd, layer, 2 * tp, tm=tm)
    return _moe_combine(x, y.reshape(2, tp, D), info, mods, geom=geom, tmc=tmc)


def kernel(x, c, ctx, c_ctx, w_mod, b_mod, hg_w_in, hg_lb_logits, hg_norm_w, hg_w_out, sw_w_qkv, sw_q_norm,
           sw_k_norm, sw_sink, sw_w_out, ff_w_gate_up, ff_w_down, moe_router, moe_w_gate_up, moe_w_down):
    nb, seq, _ = x.shape
    ctx_len = ctx.shape[1]
    depth = w_mod.shape[0]
    nct_rows = nb * ctx_len
    geom = (nb, nct_rows, seq)
    tm = 512
    tmb = 1024 if nct_rows % 1024 == 0 and seq % 1024 == 0 else tm
    bf = jnp.bfloat16

    xs = jnp.concatenate([ctx.reshape(nct_rows, D), x.reshape(nb * seq, D)], axis=0)
    cpad = jnp.concatenate([c, c_ctx[None, :], jnp.zeros((8 - nb - 1, D), jnp.float32)], axis=0)
    mods_all = _modvecs(cpad, w_mod, b_mod)

    p_lb = jax.nn.softmax(hg_lb_logits.astype(jnp.float32), axis=0)
    lower_bounds = jnp.cumsum(p_lb, axis=0) - p_lb[:1]
    hg_w_in_bf, ff_wgu_bf, ff_wd_bf = hg_w_in.astype(bf), ff_w_gate_up.astype(bf), ff_w_down.astype(bf)
    moe_wgu_bf, moe_wd_bf = moe_w_gate_up.astype(bf), moe_w_down.astype(bf)

    for i in range(depth):
        ctx_live = i < depth - 1
        mods = mods_all[i]
        j = i // 2
        if i % 2 == 0:
            p = _mod_matmul(xs, mods, hg_w_in_bf, j, k_shift=0, geom=geom, tm=tm, tn=1024, out_dtype=jnp.float32)
            of, ob = _hg_scan(p, lower_bounds[j, 0:1], lower_bounds[j, 1:2], geom)
            xs = _hg_out(of, ob, p, hg_norm_w[j][None, :], hg_w_out[j].astype(bf), xs, mods,
                         geom=geom, tm=tmb, tn=1024)
        else:
            q, kv = _qkv_proj(xs, mods, sw_w_qkv[j], sw_q_norm[j], sw_k_norm[j], geom=geom, tm=tmb)
            o = _attention(q, kv, sw_sink[j], geom, ctx_live)
            xs = _res_matmul(o, sw_w_out[j].astype(bf), xs, mods, geom=geom, tm=tmb, tn=1024)
            if not ctx_live:
                geom = (nb, 0, seq)
        if i % 2 == 0:
            xs = _ffn(xs, mods, ff_wgu_bf, ff_wd_bf, j, geom=geom, tm=tm)
        else:
            rpad = jnp.pad(moe_router[j], ((0, 0), (0, LANES - N_EXPERTS)))
            xs = _moe(xs, mods, rpad, moe_wgu_bf, moe_wd_bf, j, geom=geom, tm=tm, tmc=tm)
    return xs[xs.shape[0] - nb * seq:].reshape(nb, seq, D)
```

```python
import functools

import numpy as np
import jax
import jax.numpy as jnp
from jax import lax
from jax.experimental import pallas as pl
from jax.experimental.pallas import tpu as pltpu

D = 1024
EPS = 1e-6
NEG_BIG = -1e30
LOG2E = 1.4426950408889634
GRID_W = 64
ROPE_THETA = 10000.0

HG_HEADS = 8
HG_DK = 128
HG_FDIM = HG_HEADS * HG_DK
HG_CHUNK = 128
HG_LEVELS = 7
HG_ROWS = 256
HG_HB = 8

SW_HEADS = 16
SW_KV = 4
SW_GROUP = 4
SW_DH = 64
SW_BLOCK = 128

N_EXPERTS = 8
LANES = 128

VMEM_LIMIT = 56 * 1024 * 1024


def _cparams(sem):
    return pltpu.CompilerParams(dimension_semantics=sem, vmem_limit_bytes=VMEM_LIMIT)


def _sigmoid(x):
    return 1.0 / (1.0 + jnp.exp2(x * -LOG2E))


def _mod_row(i, nct, lt, nb):
    return jnp.where(i < nct, nb, (i - nct) // lt)


def _mod_spec(k, nct, lt, nb):
    return pl.BlockSpec((None, 1, D), lambda i, *_: (_mod_row(i, nct, lt, nb) * 6 + k, 0, 0))


def _modulate(x, shift, scale):
    ms = jnp.mean(x * x, axis=-1, keepdims=True)
    return (x * lax.rsqrt(ms + EPS)) * (1.0 + scale) + shift


def _modvec_kernel(c_ref, w_ref, b_ref, o_ref):
    c = c_ref[...]
    s = c * _sigmoid(c)
    o_ref[...] = jnp.dot(s, w_ref[...], precision=lax.Precision.HIGHEST,
                         preferred_element_type=jnp.float32) + b_ref[...]


def _modvecs(cpad, w_mod, b_mod):
    depth = w_mod.shape[0]
    tn = 1024
    out = pl.pallas_call(
        _modvec_kernel,
        out_shape=jax.ShapeDtypeStruct((depth, 8, 6 * D), jnp.float32),
        grid=(depth, 6 * D // tn),
        in_specs=[pl.BlockSpec((8, D), lambda l, j: (0, 0)),
                  pl.BlockSpec((None, D, tn), lambda l, j: (l, 0, j)),
                  pl.BlockSpec((None, 1, tn), lambda l, j: (l, 0, j))],
        out_specs=pl.BlockSpec((None, 8, tn), lambda l, j: (l, 0, j)),
        name="adaln_vectors",
        compiler_params=_cparams(("arbitrary", "arbitrary")),
    )(cpad, w_mod, b_mod.reshape(depth, 1, 6 * D))
    return out.reshape(depth, 8 * 6, 1, D)


def _modmm_kernel(x_ref, sh_ref, sc_ref, w_ref, o_ref, *, tn):
    h = _modulate(x_ref[...], sh_ref[...], sc_ref[...]).astype(jnp.bfloat16)
    for n0 in range(0, w_ref.shape[1], tn):
        o_ref[:, n0:n0 + tn] = jnp.dot(h, w_ref[:, n0:n0 + tn], preferred_element_type=jnp.float32).astype(o_ref.dtype)


def _mod_matmul(x, mods, w, layer, *, k_shift, geom, tm, tn, out_dtype):
    nb, nct_rows, seq = geom
    t, n = x.shape[0], w.shape[2]
    nct, lt = nct_rows // tm, seq // tm
    return pl.pallas_call(
        functools.partial(_modmm_kernel, tn=tn),
        out_shape=jax.ShapeDtypeStruct((t, n), out_dtype),
        grid=(t // tm,),
        in_specs=[pl.BlockSpec((tm, D), lambda i: (i, 0)),
                  _mod_spec(k_shift, nct, lt, nb),
                  _mod_spec(k_shift + 1, nct, lt, nb),
                  pl.BlockSpec((None,) + w.shape[1:], lambda i: (layer, 0, 0), pipeline_mode=pl.Buffered(1))],
        out_specs=pl.BlockSpec((tm, n), lambda i: (i, 0)),
        name="hg_in_proj",
        compiler_params=_cparams(("arbitrary",)),
    )(x, mods, mods, w)


def _hg_tables(fwd):
    c = HG_CHUNK
    t = np.arange(c)[:, None]
    u = np.arange(c)[None, :]
    reach = (u <= t) if fwd else (u >= t)
    x = t ^ u
    lvl = np.where(x > 0, np.floor(np.log2(np.maximum(x, 1))), HG_LEVELS)
    lvl = np.where(reach, lvl, -1).astype(np.int32)
    tri = np.concatenate([reach.astype(np.float32)] * 3, axis=1)
    return tri, lvl


def _hg_gates(q_raw, z, lb, tri_ref, a_ref):
    bf = jnp.bfloat16
    q = q_raw * _sigmoid(q_raw)
    sig = _sigmoid(z)
    fc = jnp.maximum(lb + (1.0 - lb) * sig, 1e-30)
    lf = jnp.log(fc) * LOG2E
    k = (1.0 - lb) * (1.0 - sig)
    hi = lf.astype(bf)
    rest = lf - hi.astype(jnp.float32)
    mid = rest.astype(bf)
    lo = (rest - mid.astype(jnp.float32)).astype(bf)
    a = jnp.dot(tri_ref[...], jnp.concatenate([hi, mid, lo], axis=0), preferred_element_type=jnp.float32)
    a_ref[...] = a
    return q, k, fc, a


def _hg_scores(q, k, fc, a, a_ref, lvl, fwd):
    c = HG_CHUNK
    bf = jnp.bfloat16

    def rows_of(row, n):
        return jnp.broadcast_to(a_ref[row:row + 1, :], (n, HG_DK))

    rowi = lax.broadcasted_iota(jnp.int32, (c, HG_DK), 0)
    nt = (((1,), (1,)), ((), ()))
    zeros8 = jnp.zeros((8, HG_DK), jnp.float32)

    def level_operands(l):
        w = 1 << l
        near = w - 1 if fwd else w
        if w < 8:
            upper = ((rowi >> l) & 1) == 1
            q_side = upper if fwd else jnp.logical_not(upper)
            if l == 0:
                qe, ke = q * fc, k
            else:
                if l == 1:
                    first = jnp.concatenate([rows_of(8 * g + near, 8) for g in range(c // 8)], axis=0)
                    second = jnp.concatenate([rows_of(8 * g + 4 + near, 8) for g in range(c // 8)], axis=0)
                    bnd = jnp.where((rowi & 4) == 0, first, second)
                else:
                    bnd = jnp.concatenate([rows_of(8 * g + near, 8) for g in range(c // 8)], axis=0)
                e = jnp.exp2(-jnp.abs(a - bnd))
                qe, ke = q * e, k * e
            return jnp.where(q_side, qe, 0.0).astype(bf), jnp.where(q_side, 0.0, ke).astype(bf)
        qp, kp = [], []
        for j in range(c // w):
            rs = slice(j * w, (j + 1) * w)
            bnd = rows_of((j // 2) * 2 * w + near, w)
            zero = jnp.concatenate([zeros8] * (w // 8), axis=0)
            if (j % 2 == 1) == fwd:
                qp.append(q[rs] * jnp.exp2(a[rs] - bnd))
                kp.append(zero)
            else:
                qp.append(zero)
                kp.append(k[rs] * jnp.exp2(bnd - a[rs]))
        return jnp.concatenate(qp, axis=0).astype(bf), jnp.concatenate(kp, axis=0).astype(bf)

    scores = jnp.where(lvl == HG_LEVELS,
                       lax.dot_general(q.astype(bf), k.astype(bf), nt, preferred_element_type=jnp.float32), 0.0)
    for l in range(HG_LEVELS):
        qm, km = level_operands(l)
        sc = lax.dot_general(qm, km, nt, preferred_element_type=jnp.float32)
        scores = jnp.where(lvl == l, sc, scores)
    return scores.astype(bf)


def _hg_finish(q, k, v, a, scores, st_ref, fwd):
    c = HG_CHUNK
    bf = jnp.bfloat16
    nt = (((1,), (1,)), ((), ()))
    last = c - 1 if fwd else 0
    e_in = jnp.exp2(a)
    e_out = jnp.exp2(a[last:last + 1, :] - a)
    st = st_ref[...]
    o = (jnp.dot(scores, v.astype(bf), preferred_element_type=jnp.float32)
         + lax.dot_general((q * e_in).astype(bf), st.astype(bf), nt, preferred_element_type=jnp.float32))
    st_ref[...] = (e_in[last:last + 1, :] * st
                   + jnp.dot(v.T.astype(bf), (k * e_out).astype(bf), preferred_element_type=jnp.float32))
    return o


def _hg_scan_kernel(qf_ref, vf_ref, zf_ref, qb_ref, vb_ref, zb_ref, lbf_ref, lbb_ref, trif_ref, trib_ref,
                    lvlf_ref, lvlb_ref, of_ref, ob_ref, sf_ref, sb_ref, a_ref):
    @pl.when(pl.program_id(2) == 0)
    def _():
        sf_ref[...] = jnp.zeros_like(sf_ref)
        sb_ref[...] = jnp.zeros_like(sb_ref)

    c = HG_CHUNK
    nch = HG_ROWS // c
    lvlf, lvlb = lvlf_ref[...], lvlb_ref[...]
    units = []
    for hh in range(HG_HB):
        cs = slice(hh * HG_DK, (hh + 1) * HG_DK)
        for step, ci in enumerate(range(nch)):
            units.append(dict(step=step, rows=slice(ci * c, (ci + 1) * c), cols=cs, fwd=True, q=qf_ref, v=vf_ref,
                              z=zf_ref, lb=lbf_ref, tri=trif_ref, lvl=lvlf, st=sf_ref.at[hh], out=of_ref))
        for step, ci in enumerate(reversed(range(nch))):
            units.append(dict(step=step, rows=slice(ci * c, (ci + 1) * c), cols=cs, fwd=False, q=qb_ref, v=vb_ref,
                              z=zb_ref, lb=lbb_ref, tri=trib_ref, lvl=lvlb, st=sb_ref.at[hh], out=ob_ref))
    for n, u in enumerate(units):
        u["a_ref"] = a_ref.at[n]
        u["q"], u["k"], u["fc"], u["a"] = _hg_gates(u["q"][u["rows"], u["cols"]], u["z"][u["rows"], u["cols"]],
                                                    u["lb"][:, u["cols"]], u["tri"], u["a_ref"])
    for u in units:
        u["scores"] = _hg_scores(u["q"], u["k"], u["fc"], u["a"], u["a_ref"], u["lvl"], u["fwd"])
    for step in range(nch):
        for u in units:
            if u["step"] == step:
                o = _hg_finish(u["q"], u["k"], u["v"][u["rows"], u["cols"]], u["a"], u["scores"], u["st"], u["fwd"])
                u["out"][u["rows"], u["cols"]] = o.astype(u["out"].dtype)


def _hg_scan(p, lbf, lbb, geom):
    nb, nct_rows, seq = geom
    t = p.shape[0]
    r = HG_ROWS
    cb, lb_ = (nct_rows // nb) // r, seq // r
    nsteps = cb + lb_
    lat0 = nct_rows // r

    def fblk(b, s):
        return jnp.where(s < cb, b * cb + s, lat0 + b * lb_ + (s - cb))

    def bblk(b, s):
        return jnp.where(s < cb, b * cb + (cb - 1 - s), lat0 + b * lb_ + (lb_ - 1 - (s - cb)))

    wcol = HG_HB * HG_DK
    ngrp = HG_HEADS // HG_HB
    units = 2 * HG_HB * (r // HG_CHUNK)

    def spec(blk, colblock):
        return pl.BlockSpec((r, wcol), lambda b, h, s: (blk(b, s), colblock * ngrp + h))

    lbspec = pl.BlockSpec((1, wcol), lambda b, h, s: (0, h))
    const = lambda shape: pl.BlockSpec(shape, lambda b, h, s: (0, 0))
    trif, lvlf = _hg_tables(True)
    trib, lvlb = _hg_tables(False)
    return pl.pallas_call(
        _hg_scan_kernel,
        out_shape=(jax.ShapeDtypeStruct((t, D), jnp.bfloat16), jax.ShapeDtypeStruct((t, D), jnp.bfloat16)),
        grid=(nb, ngrp, nsteps),
        in_specs=[spec(fblk, 0), spec(fblk, 1), spec(fblk, 2),
                  spec(bblk, 0), spec(bblk, 1), spec(bblk, 3),
                  lbspec, lbspec, const(trif.shape), const(trib.shape), const(lvlf.shape), const(lvlb.shape)],
        out_specs=(pl.BlockSpec((r, wcol), lambda b, h, s: (fblk(b, s), h)),
                   pl.BlockSpec((r, wcol), lambda b, h, s: (bblk(b, s), h))),
        scratch_shapes=[pltpu.VMEM((HG_HB, HG_DK, HG_DK), jnp.float32),
                        pltpu.VMEM((HG_HB, HG_DK, HG_DK), jnp.float32),
                        pltpu.VMEM((units, HG_CHUNK, HG_DK), jnp.float32)],
        name="hg_scan",
        compiler_params=_cparams(("arbitrary", "arbitrary", "arbitrary")),
    )(p, p, p, p, p, p, lbf, lbb, jnp.asarray(trif, jnp.bfloat16), jnp.asarray(trib, jnp.bfloat16),
      jnp.asarray(lvlf), jnp.asarray(lvlb))


def _hg_out_kernel(of_ref, ob_ref, g_ref, nw_ref, w_ref, x_ref, gate_ref, o_ref, h_ref):
    @pl.when(pl.program_id(1) == 0)
    def _():
        nw = nw_ref[...]
        for h in range(HG_HEADS):
            cs = slice(h * HG_DK, (h + 1) * HG_DK)
            o = of_ref[:, cs].astype(jnp.float32) + ob_ref[:, cs].astype(jnp.float32)
            y = o * lax.rsqrt(jnp.mean(o * o, axis=-1, keepdims=True) + EPS) * nw
            g = g_ref[:, cs]
            h_ref[:, cs] = (y * (g * _sigmoid(g))).astype(jnp.bfloat16)

    acc = jnp.dot(h_ref[...], w_ref[...], preferred_element_type=jnp.float32)
    o_ref[...] = x_ref[...] + gate_ref[...] * acc


def _hg_out(of, ob, p, norm_w, w, x, mods, *, geom, tm, tn):
    nb, nct_rows, seq = geom
    t = x.shape[0]
    nct, lt = nct_rows // tm, seq // tm
    gate = pl.BlockSpec((None, 1, tn), lambda i, j: (_mod_row(i, nct, lt, nb) * 6 + 2, 0, j))
    return pl.pallas_call(
        _hg_out_kernel,
        out_shape=jax.ShapeDtypeStruct((t, D), jnp.float32),
        grid=(t // tm, D // tn),
        in_specs=[pl.BlockSpec((tm, D), lambda i, j: (i, 0)),
                  pl.BlockSpec((tm, D), lambda i, j: (i, 0)),
                  pl.BlockSpec((tm, D), lambda i, j: (i, 4)),
                  pl.BlockSpec((1, HG_DK), lambda i, j: (0, 0)),
                  pl.BlockSpec((D, tn), lambda i, j: (0, j)),
                  pl.BlockSpec((tm, tn), lambda i, j: (i, j)),
                  gate],
        out_specs=pl.BlockSpec((tm, tn), lambda i, j: (i, j)),
        scratch_shapes=[pltpu.VMEM((tm, D), jnp.bfloat16)],
        name="hg_out_proj",
        compiler_params=_cparams(("arbitrary", "arbitrary")),
    )(of, ob, p, norm_w, w, x, mods)


def _resmm_kernel(a_ref, w_ref, x_ref, gate_ref, o_ref):
    acc = jnp.dot(a_ref[...], w_ref[...], preferred_element_type=jnp.float32)
    o_ref[...] = x_ref[...] + gate_ref[...] * acc


def _res_matmul(a, w, x, mods, *, geom, tm, tn):
    nb, nct_rows, seq = geom
    rows = a.shape[0]
    t0 = (x.shape[0] - rows) // tm
    nct, lt = nct_rows // tm, seq // tm
    gate = pl.BlockSpec((None, 1, tn), lambda i, j: (_mod_row(i + t0, nct, lt, nb) * 6 + 2, 0, j))
    return pl.pallas_call(
        _resmm_kernel,
        out_shape=jax.ShapeDtypeStruct((rows, D), jnp.float32),
        grid=(rows // tm, D // tn),
        in_specs=[pl.BlockSpec((tm, a.shape[1]), lambda i, j: (i, 0)),
                  pl.BlockSpec((a.shape[1], tn), lambda i, j: (0, j)),
                  pl.BlockSpec((tm, tn), lambda i, j: (i + t0, j)),
                  gate],
        out_specs=pl.BlockSpec((tm, tn), lambda i, j: (i, j)),
        name="attn_out_proj",
        compiler_params=_cparams(("arbitrary", "arbitrary")),
    )(a, w, x, mods)


def _rope_tables(seq, tm):
    rows = seq // GRID_W
    row = np.repeat(np.arange(rows, dtype=np.float32), GRID_W)
    col = np.tile(np.arange(GRID_W, dtype=np.float32), rows)
    nf = SW_DH // 4
    inv = (ROPE_THETA ** (-np.arange(nf, dtype=np.float32) / nf)).astype(np.float32)
    ang_r = row[:, None] * inv
    ang_c = col[:, None] * inv
    cos = np.concatenate([np.cos(ang_r), np.cos(ang_r), np.cos(ang_c), np.cos(ang_c)], axis=1)
    sin = np.concatenate([-np.sin(ang_r), np.sin(ang_r), -np.sin(ang_c), np.sin(ang_c)], axis=1)
    cos = np.concatenate([np.tile(cos, (1, 2)), np.ones((tm, LANES), np.float32)], axis=0)
    sin = np.concatenate([np.tile(sin, (1, 2)), np.zeros((tm, LANES), np.float32)], axis=0)
    return jnp.asarray(cos, jnp.float32), jnp.asarray(sin, jnp.float32)


def _norm_rope(y, ones_ref, nw, cos, sin, scale):
    ss = jnp.dot((y * y).astype(jnp.bfloat16), ones_ref[...], preferred_element_type=jnp.float32)
    y = y * lax.rsqrt(ss * (1.0 / SW_DH) + EPS) * nw
    first = (lax.broadcasted_iota(jnp.int32, (y.shape[0], LANES), 1) & 16) == 0
    pieces = []
    for g in range(y.shape[1] // LANES):
        yg = y[:, g * LANES:(g + 1) * LANES]
        partner = jnp.where(first, pltpu.roll(yg, LANES - 16, 1), pltpu.roll(yg, 16, 1))
        pieces.append((yg * cos + partner * sin) * scale)
    return pieces


def _q_kernel(x_ref, sh_ref, sc_ref, w_ref, nw_ref, ones_ref, cos_ref, sin_ref, o_ref):
    h = _modulate(x_ref[...], sh_ref[...], sc_ref[...]).astype(jnp.bfloat16)
    acc = jnp.dot(h, w_ref[...], preferred_element_type=jnp.float32)
    pieces = _norm_rope(acc, ones_ref, nw_ref[...], cos_ref[...], sin_ref[...], SW_DH ** -0.5 * LOG2E)
    for g, piece in enumerate(pieces):
        o_ref[:, g * LANES:(g + 1) * LANES] = piece.astype(o_ref.dtype)


def _kv_kernel(x_ref, sh_ref, sc_ref, w_ref, nw_ref, ones_ref, cos_ref, sin_ref, o_ref):
    h = _modulate(x_ref[...], sh_ref[...], sc_ref[...]).astype(jnp.bfloat16)
    acc = jnp.dot(h, w_ref[...], preferred_element_type=jnp.float32)
    nkv = SW_KV * SW_DH
    k_pieces = _norm_rope(acc[:, :nkv], ones_ref, nw_ref[...], cos_ref[...], sin_ref[...], 1.0)
    v_pieces = [acc[:, nkv + g * LANES:nkv + (g + 1) * LANES] for g in range(nkv // LANES)]
    low = lax.broadcasted_iota(jnp.int32, (acc.shape[0], LANES), 1) < SW_DH
    for base, pieces in ((0, k_pieces), (D, v_pieces)):
        for c, piece in enumerate(pieces):
            swapped = pltpu.roll(piece, SW_DH, 1)
            for half, rep in enumerate((jnp.where(low, piece, swapped), jnp.where(low, swapped, piece))):
                rep = rep.astype(o_ref.dtype)
                col = base + (2 * c + half) * SW_GROUP * SW_DH
                o_ref[:, col:col + LANES] = rep
                o_ref[:, col + LANES:col + 2 * LANES] = rep


def _qkv_proj(x, mods, w_qkv, q_norm, k_norm, *, geom, tm):
    nb, nct_rows, seq = geom
    t = x.shape[0]
    nct, lt = nct_rows // tm, seq // tm
    nkv = SW_KV * SW_DH
    cos, sin = _rope_tables(seq, tm)
    ones = lambda n: jnp.asarray(np.kron(np.eye(n // SW_DH, dtype=np.float32),
                                         np.ones((SW_DH, SW_DH), np.float32)), jnp.bfloat16)
    tab = lambda i: (jnp.where(i < nct, lt, (i - nct) % lt), 0)
    once = pl.Buffered(1)

    def call(body, w, nw, n_norm, n_out, name):
        return pl.pallas_call(
            body,
            out_shape=jax.ShapeDtypeStruct((t, n_out), jnp.bfloat16),
            grid=(t // tm,),
            in_specs=[pl.BlockSpec((tm, D), lambda i: (i, 0)),
                      _mod_spec(0, nct, lt, nb),
                      _mod_spec(1, nct, lt, nb),
                      pl.BlockSpec(w.shape, lambda i: (0, 0), pipeline_mode=once),
                      pl.BlockSpec((1, n_norm), lambda i: (0, 0)),
                      pl.BlockSpec((n_norm, n_norm), lambda i: (0, 0), pipeline_mode=once),
                      pl.BlockSpec((tm, LANES), tab),
                      pl.BlockSpec((tm, LANES), tab)],
            out_specs=pl.BlockSpec((tm, n_out), lambda i: (i, 0)),
            name=name,
            compiler_params=_cparams(("arbitrary",)),
        )(x, mods, mods, w, nw, ones(n_norm), cos, sin)

    bf = jnp.bfloat16
    q = call(_q_kernel, w_qkv[:, :D].astype(bf), jnp.tile(q_norm, SW_HEADS)[None, :], D, D, "q_proj")
    kv = call(_kv_kernel, w_qkv[:, D:].astype(bf), jnp.tile(k_norm, SW_KV)[None, :], nkv, 2 * D, "kv_proj")
    return q, kv


def _attn_items(items, sink_ref, kv, o_ref):
    nt = (((1,), (1,)), ((), ()))
    work = [(it, g) for it in items for g in range(SW_GROUP)]
    scores = []
    for (_, q, kcat, _, _), g in work:
        grp = lax.broadcasted_iota(jnp.int32, q.shape, 1) // SW_DH
        scores.append(lax.dot_general(jnp.where(grp == g, q, jnp.zeros_like(q)), kcat, nt,
                                      preferred_element_type=jnp.float32))
    probs, denoms = [], []
    for ((_, _, _, _, valid), g), s in zip(work, scores):
        sink = sink_ref[kv * SW_GROUP + g] * LOG2E
        if valid is not None:
            s = jnp.where(valid, s, NEG_BIG)
        m = jnp.maximum(jnp.max(s, axis=-1, keepdims=True), sink)
        p = jnp.exp2(s - m)
        denoms.append(jnp.sum(p, axis=-1, keepdims=True) + jnp.exp2(sink - m))
        probs.append(p.astype(jnp.bfloat16))
    outs = {}
    for n, ((rows, q, _, vcat, _), g) in enumerate(work):
        grp = lax.broadcasted_iota(jnp.int32, q.shape, 1) // SW_DH
        og = jnp.dot(probs[n], vcat, preferred_element_type=jnp.float32) / denoms[n]
        outs[rows] = og if g == 0 else jnp.where(grp == g, og, outs[rows])
    for rows, out in outs.items():
        o_ref[rows[0]:rows[1], :] = out.astype(o_ref.dtype)


def _attn_kernel(sink_ref, q_ref, kp_ref, kc_ref, kn_ref, kx_ref, vp_ref, vc_ref, vn_ref, vx_ref, o_ref,
                 *, nstep, ctx_out):
    kv, j = pl.program_id(1), pl.program_id(2)
    blk = SW_BLOCK

    def latent():
        kx, vx = kx_ref[...], vx_ref[...]
        k0, k1, v0, v1 = kc_ref[0:blk, :], kc_ref[blk:2 * blk, :], vc_ref[0:blk, :], vc_ref[blk:2 * blk, :]
        ns = 3 * blk + kx.shape[0]
        t = lax.broadcasted_iota(jnp.int32, (blk, ns), 0)
        s = lax.broadcasted_iota(jnp.int32, (blk, ns), 1)

        def window(has_prev, has_next):
            lo = jnp.maximum(t, jnp.where(has_prev, 0, blk))
            hi = jnp.minimum(t + 2 * blk, jnp.where(has_next, 3 * blk - 1, 2 * blk - 1))
            return ((s >= lo) & (s <= hi)) | (s >= 3 * blk)

        items = [((0, blk), q_ref[0:blk, :], jnp.concatenate([kp_ref[...], k0, k1, kx], axis=0),
                  jnp.concatenate([vp_ref[...], v0, v1, vx], axis=0), window(j > 0, True)),
                 ((blk, 2 * blk), q_ref[blk:2 * blk, :], jnp.concatenate([k0, k1, kn_ref[...], kx], axis=0),
                  jnp.concatenate([v0, v1, vn_ref[...], vx], axis=0), window(True, j < nstep - 1))]
        _attn_items(items, sink_ref, kv, o_ref)

    if not ctx_out:
        latent()
        return
    pl.when(j < nstep)(latent)

    @pl.when(j >= nstep)
    def _():
        _attn_items([((0, 2 * blk), q_ref[...], kx_ref[...], vx_ref[...], None)], sink_ref, kv, o_ref)


def _attention(q, kv, sink, geom, ctx_out):
    nb, nct_rows, seq = geom
    ctx_len = nct_rows // nb
    pair = 2 * SW_BLOCK
    assert ctx_len == pair and seq % pair == 0
    nblk = seq // SW_BLOCK
    nstep = seq // pair
    lat0 = nct_rows // SW_BLOCK
    w = SW_GROUP * SW_DH
    smem = pl.BlockSpec(memory_space=pltpu.SMEM)

    def qrow(b, j):
        return jnp.where(j < nstep, nct_rows // pair + b * nstep + j, b)

    def edge(off, colblock):
        return pl.BlockSpec((SW_BLOCK, w), lambda b, kv, j: (
            lat0 + b * nblk + jnp.clip(2 * jnp.minimum(j, nstep - 1) + off, 0, nblk - 1), colblock * SW_KV + kv))

    def center(colblock):
        return pl.BlockSpec((pair, w), lambda b, kv, j: (qrow(b, jnp.minimum(j, nstep - 1)), colblock * SW_KV + kv))

    def ctxkv(colblock):
        return pl.BlockSpec((ctx_len, w), lambda b, kv, j: (b, colblock * SW_KV + kv))

    if ctx_out:
        steps, out_rows = nstep + 1, q.shape[0]
        out_spec = pl.BlockSpec((pair, w), lambda b, kv, j: (qrow(b, j), kv))
    else:
        steps, out_rows = nstep, nb * seq
        out_spec = pl.BlockSpec((pair, w), lambda b, kv, j: (b * nstep + j, kv))
    return pl.pallas_call(
        functools.partial(_attn_kernel, nstep=nstep, ctx_out=ctx_out),
        out_shape=jax.ShapeDtypeStruct((out_rows, D), jnp.bfloat16),
        grid=(nb, SW_KV, steps),
        in_specs=[smem, pl.BlockSpec((pair, w), lambda b, kv, j: (qrow(b, j), kv)),
                  edge(-1, 0), center(0), edge(2, 0), ctxkv(0),
                  edge(-1, 1), center(1), edge(2, 1), ctxkv(1)],
        out_specs=out_spec,
        name="attention",
        compiler_params=_cparams(("arbitrary", "arbitrary", "arbitrary")),
    )(sink, q, kv, kv, kv, kv, kv, kv, kv, kv)


def _router_kernel(x_ref, sh_ref, sc_ref, r_ref, h_ref, info_ref):
    h = _modulate(x_ref[...], sh_ref[...], sc_ref[...])
    h_ref[...] = h
    logits = jnp.dot(h, r_ref[...], precision=lax.Precision.HIGHEST, preferred_element_type=jnp.float32)
    lane = lax.broadcasted_iota(jnp.int32, logits.shape, 1).astype(jnp.float32)
    logits = jnp.where(lane < N_EXPERTS, logits, -jnp.inf)
    l1 = jnp.max(logits, axis=-1, keepdims=True)
    i1 = jnp.min(jnp.where(logits == l1, lane, float(LANES)), axis=-1, keepdims=True)
    rest = jnp.where(lane == i1, -jnp.inf, logits)
    l2 = jnp.max(rest, axis=-1, keepdims=True)
    i2 = jnp.min(jnp.where(rest == l2, lane, float(LANES)), axis=-1, keepdims=True)
    w1 = 1.0 / (1.0 + jnp.exp(l2 - l1))
    info_ref[...] = jnp.where(lane == 0.0, i1, jnp.where(lane == 1.0, i2, jnp.where(lane == 2.0, w1, 1.0 - w1)))


def _router(x, mods, router_pad, *, geom, tm):
    nb, nct_rows, seq = geom
    t = x.shape[0]
    nct, lt = nct_rows // tm, seq // tm
    return pl.pallas_call(
        _router_kernel,
        out_shape=(jax.ShapeDtypeStruct((t, D), jnp.float32), jax.ShapeDtypeStruct((t, LANES), jnp.float32)),
        grid=(t // tm,),
        in_specs=[pl.BlockSpec((tm, D), lambda i: (i, 0)), _mod_spec(3, nct, lt, nb), _mod_spec(4, nct, lt, nb),
                  pl.BlockSpec((D, LANES), lambda i: (0, 0))],
        out_specs=(pl.BlockSpec((tm, D), lambda i: (i, 0)), pl.BlockSpec((tm, LANES), lambda i: (i, 0))),
        name="moe_router",
        compiler_params=_cparams(("arbitrary",)),
    )(x, mods, mods, router_pad)


FFN_SUB = 512


def _swiglu(h, wgu_ref, wd_ref):
    fdim = wd_ref.shape[0]
    bounds = list(range(0, fdim, FFN_SUB)) + [fdim]
    blocks = list(zip(bounds[:-1], bounds[1:]))
    up = lambda lo, hi: (jnp.dot(h, wgu_ref[:, lo:hi], preferred_element_type=jnp.float32),
                         jnp.dot(h, wgu_ref[:, fdim + lo:fdim + hi], preferred_element_type=jnp.float32))
    down = lambda gu, lo, hi: jnp.dot((gu[0] * _sigmoid(gu[0]) * gu[1]).astype(jnp.bfloat16), wd_ref[lo:hi, :],
                                      preferred_element_type=jnp.float32)
    gu = up(*blocks[0])
    acc = None
    for prev, cur in zip(blocks[:-1], blocks[1:]):
        gu_next = up(*cur)
        y = down(gu, *prev)
        acc = y if acc is None else acc + y
        gu = gu_next
    y = down(gu, *blocks[-1])
    return y if acc is None else acc + y


def _ffn_kernel(x_ref, sh_ref, sc_ref, gate_ref, wgu_ref, wd_ref, o_ref):
    x = x_ref[...]
    h = _modulate(x, sh_ref[...], sc_ref[...]).astype(jnp.bfloat16)
    o_ref[...] = x + gate_ref[...] * _swiglu(h, wgu_ref, wd_ref)


def _ffn(x, mods, wgu, wd, layer, *, geom, tm):
    nb, nct_rows, seq = geom
    t = x.shape[0]
    nct, lt = nct_rows // tm, seq // tm
    once = pl.Buffered(1)
    return pl.pallas_call(
        _ffn_kernel,
        out_shape=jax.ShapeDtypeStruct((t, D), jnp.float32),
        grid=(t // tm,),
        in_specs=[pl.BlockSpec((tm, D), lambda i: (i, 0)),
                  _mod_spec(3, nct, lt, nb),
                  _mod_spec(4, nct, lt, nb),
                  _mod_spec(5, nct, lt, nb),
                  pl.BlockSpec((None,) + wgu.shape[1:], lambda i: (layer, 0, 0), pipeline_mode=once),
                  pl.BlockSpec((None,) + wd.shape[1:], lambda i: (layer, 0, 0), pipeline_mode=once)],
        out_specs=pl.BlockSpec((tm, D), lambda i: (i, 0)),
        name="dense_ffn",
        compiler_params=_cparams(("arbitrary",)),
    )(x, mods, mods, mods, wgu, wd)


def _moe_kernel(te_ref, nt_ref, rows0_ref, rows1_ref, rows2_ref, outrows_ref, h_hbm, wgu_ref, wd_ref, y_hbm,
                hbuf, hb_ref, obuf, gsem, ssem, *, tm):
    del te_ref
    i = pl.program_id(0)
    nt = nt_ref[0]
    g_slot, o_slot = i % 3, i % 2

    def gather(rows_ref, s):
        for r in range(tm):
            pltpu.make_async_copy(h_hbm.at[pl.ds(rows_ref[0, r], 1), :], hbuf.at[s, pl.ds(r, 1), :],
                                  gsem.at[s]).start()

    def scatter_prev():
        for r in range(tm):
            pltpu.make_async_copy(obuf.at[1 - o_slot, pl.ds(r, 1), :], y_hbm.at[pl.ds(outrows_ref[0, r], 1), :],
                                  ssem.at[1 - o_slot]).start()

    @pl.when(i == 0)
    def _():
        gather(rows0_ref, 0)
        gather(rows1_ref, 1)
        obuf[1] = jnp.zeros((tm, D), jnp.float32)

    @pl.when((i >= 1) & (i - 2 < nt))
    def _():
        pltpu.make_async_copy(obuf.at[o_slot], obuf.at[o_slot], ssem.at[o_slot]).wait()

    @pl.when(i < nt + 2)
    def _():
        pltpu.make_async_copy(hbuf.at[g_slot], hbuf.at[g_slot], gsem.at[g_slot]).wait()

    @pl.when(i < nt)
    def _():
        gather(rows2_ref, (i + 2) % 3)
        scatter_prev()
        hb_ref[...] = hbuf[g_slot].astype(jnp.bfloat16)
        obuf[o_slot] = _swiglu(hb_ref[...], wgu_ref, wd_ref)

    @pl.when(i == nt)
    def _():
        scatter_prev()


def _moe_experts(h, tile_expert, n_tiles, rows, outrows, wgu, wd, layer, n_out, *, tm):
    nt = rows.shape[0]
    fdim = wd.shape[2]
    assert nt >= 3
    smem_rows = lambda off: pl.BlockSpec((None, 1, tm), lambda i, te, n: (jnp.minimum(i + off, nt - 1), 0, 0),
                                         memory_space=pltpu.SMEM)
    once = pl.Buffered(1)
    grid_spec = pltpu.PrefetchScalarGridSpec(
        num_scalar_prefetch=2,
        grid=(nt,),
        in_specs=[smem_rows(0), smem_rows(1), smem_rows(2),
                  pl.BlockSpec((None, 1, tm), lambda i, te, n: (i, 0, 0), memory_space=pltpu.SMEM),
                  pl.BlockSpec(memory_space=pl.ANY),
                  pl.BlockSpec((None, None, D, 2 * fdim), lambda i, te, n: (layer, te[i], 0, 0), pipeline_mode=once),
                  pl.BlockSpec((None, None, fdim, D), lambda i, te, n: (layer, te[i], 0, 0), pipeline_mode=once)],
        out_specs=pl.BlockSpec(memory_space=pl.ANY),
        scratch_shapes=[pltpu.VMEM((3, tm, D), jnp.float32), pltpu.VMEM((tm, D), jnp.bfloat16),
                        pltpu.VMEM((2, tm, D), jnp.float32), pltpu.SemaphoreType.DMA((3,)),
                        pltpu.SemaphoreType.DMA((2,))])
    return pl.pallas_call(
        functools.partial(_moe_kernel, tm=tm),
        out_shape=jax.ShapeDtypeStruct((n_out, D), jnp.float32),
        grid_spec=grid_spec,
        name="moe_experts",
        compiler_params=_cparams(("arbitrary",)),
    )(tile_expert, n_tiles, rows, rows, rows, outrows, h, wgu, wd)


def _combine_kernel(y0_ref, y1_ref, x_ref, gate_ref, info_ref, o_ref):
    info = info_ref[...]
    y = info[:, 2:3] * y0_ref[...] + info[:, 3:4] * y1_ref[...]
    o_ref[...] = x_ref[...] + gate_ref[...] * y


def _moe_combine(x, y2, info, mods, *, geom, tmc):
    nb, nct_rows, seq = geom
    t = x.shape[0]
    nct, lt = nct_rows // tmc, seq // tmc
    return pl.pallas_call(
        _combine_kernel,
        out_shape=jax.ShapeDtypeStruct((t, D), jnp.float32),
        grid=(t // tmc,),
        in_specs=[pl.BlockSpec((None, tmc, D), lambda i: (0, i, 0)),
                  pl.BlockSpec((None, tmc, D), lambda i: (1, i, 0)),
                  pl.BlockSpec((tmc, D), lambda i: (i, 0)),
                  _mod_spec(5, nct, lt, nb),
                  pl.BlockSpec((tmc, LANES), lambda i: (i, 0))],
        out_specs=pl.BlockSpec((tmc, D), lambda i: (i, 0)),
        name="moe_combine",
        compiler_params=_cparams(("arbitrary",)),
    )(y2, y2, x, mods, info)


def _moe(x, mods, router_pad, wgu, wd, layer, *, geom, tm, tmc):
    tr = x.shape[0]
    h, info = _router(x, mods, router_pad, geom=geom, tm=tm)
    e = info[:, 0:2].astype(jnp.int32).reshape(-1)
    onehot = (e[:, None] == jnp.arange(N_EXPERTS, dtype=jnp.int32)[None, :]).astype(jnp.int32)
    csum = jnp.cumsum(onehot, axis=0)
    rank = jnp.sum(onehot * (csum - 1), axis=1)
    counts = csum[-1]
    padded = ((counts + tm - 1) // tm) * tm
    ends = jnp.cumsum(padded)
    dest = (ends - padded)[e] + rank
    nt = 2 * tr // tm + N_EXPERTS + 2
    tile_expert = jnp.minimum(jnp.sum(jnp.arange(nt, dtype=jnp.int32)[:, None] * tm >= ends[None, :], axis=1),
                              N_EXPERTS - 1).astype(jnp.int32)
    n_tiles = (ends[-1:] // tm).astype(jnp.int32)
    slot_of_row = jnp.full((nt * tm,), -1, jnp.int32).at[dest].set(jnp.arange(2 * tr, dtype=jnp.int32),
                                                                   unique_indices=True)
    tp = tr + tm // 2
    j = jnp.arange(nt * tm, dtype=jnp.int32) % tm
    spare = jnp.where(j < tm // 2, tr + j, tp + tr + j - tm // 2)
    out_row = (slot_of_row % 2) * tp + slot_of_row // 2
    rows = (jnp.maximum(slot_of_row, 0) // 2).reshape(nt, 1, tm)
    outrows = jnp.concatenate([spare[:tm], jnp.where(slot_of_row >= 0, out_row, spare)]).reshape(nt + 1, 1, tm)
    y = _moe_experts(h, tile_expert, n_tiles, rows, outrows, wgu, wd, layer, 2 * tp, tm=tm)
    return _moe_combine(x, y.reshape(2, tp, D), info, mods, geom=geom, tmc=tmc)


def kernel(x, c, ctx, c_ctx, w_mod, b_mod, hg_w_in, hg_lb_logits, hg_norm_w, hg_w_out, sw_w_qkv, sw_q_norm,
           sw_k_norm, sw_sink, sw_w_out, ff_w_gate_up, ff_w_down, moe_router, moe_w_gate_up, moe_w_down):
    nb, seq, _ = x.shape
    ctx_len = ctx.shape[1]
    depth = w_mod.shape[0]
    nct_rows = nb * ctx_len
    geom = (nb, nct_rows, seq)
    tm = 512
    tmb = 1024 if nct_rows % 1024 == 0 and seq % 1024 == 0 else tm
    bf = jnp.bfloat16

    xs = jnp.concatenate([ctx.reshape(nct_rows, D), x.reshape(nb * seq, D)], axis=0)
    cpad = jnp.concatenate([c, c_ctx[None, :], jnp.zeros((8 - nb - 1, D), jnp.float32)], axis=0)
    mods_all = _modvecs(cpad, w_mod, b_mod)

    p_lb = jax.nn.softmax(hg_lb_logits.astype(jnp.float32), axis=0)
    lower_bounds = jnp.cumsum(p_lb, axis=0) - p_lb[:1]
    hg_w_in_bf, ff_wgu_bf, ff_wd_bf = hg_w_in.astype(bf), ff_w_gate_up.astype(bf), ff_w_down.astype(bf)
    moe_wgu_bf, moe_wd_bf = moe_w_gate_up.astype(bf), moe_w_down.astype(bf)

    for i in range(depth):
        ctx_live = i < depth - 1
        mods = mods_all[i]
        j = i // 2
        if i % 2 == 0:
            p = _mod_matmul(xs, mods, hg_w_in_bf, j, k_shift=0, geom=geom, tm=tm, tn=1024, out_dtype=jnp.float32)
            of, ob = _hg_scan(p, lower_bounds[j, 0:1], lower_bounds[j, 1:2], geom)
            xs = _hg_out(of, ob, p, hg_norm_w[j][None, :], hg_w_out[j].astype(bf), xs, mods,
                         geom=geom, tm=tmb, tn=1024)
        else:
            q, kv = _qkv_proj(xs, mods, sw_w_qkv[j], sw_q_norm[j], sw_k_norm[j], geom=geom, tm=tmb)
            o = _attention(q, kv, sw_sink[j], geom, ctx_live)
            xs = _res_matmul(o, sw_w_out[j].astype(bf), xs, mods, geom=geom, tm=tmb, tn=1024)
            if not ctx_live:
                geom = (nb, 0, seq)
        if i % 2 == 0:
            xs = _ffn(xs, mods, ff_wgu_bf, ff_wd_bf, j, geom=geom, tm=tm)
        else:
            rpad = jnp.pad(moe_router[j], ((0, 0), (0, LANES - N_EXPERTS)))
            xs = _moe(xs, mods, rpad, moe_wgu_bf, moe_wd_bf, j, geom=geom, tm=tm, tmc=tm)
    return xs[xs.shape[0] - nb * seq:].reshape(nb, seq, D)
```
